```python
import jax
import jax.numpy as jnp
from jax import lax
import numpy as np

D_MODEL = 1024
BATCH = 8
SEQ = 4096
DEPTH = 4

N_MIXERS = 4
N_RET = (DEPTH + 3) // N_MIXERS
N_GLA = (DEPTH + 2) // N_MIXERS
N_LRU = (DEPTH + 1) // N_MIXERS
N_NSA = DEPTH // N_MIXERS
D_FF = 4 * D_MODEL
NORM_EPS = 1e-6
NEG_INF = -1e30

RET_HEADS = 4
RET_DK = D_MODEL // RET_HEADS
RET_DV = 2 * RET_DK
RET_CHUNK = 128
RET_IN = 2 * RET_HEADS * RET_DK + 2 * RET_HEADS * RET_DV

GLA_HEADS = 4
GLA_DK = D_MODEL // (2 * GLA_HEADS)
GLA_DV = D_MODEL // GLA_HEADS
GLA_RANK = 16
GLA_TAU = 16.0
GLA_CHUNK = 64
GLA_IN = 2 * GLA_HEADS * GLA_DK + 2 * GLA_HEADS * GLA_DV + GLA_RANK

LRU_WIDTH = D_MODEL
LRU_BLOCKS = 8
LRU_BS = LRU_WIDTH // LRU_BLOCKS
CONV_W = 4
LRU_C = 8.0

NSA_HEADS = 16
NSA_GROUPS = 2
NSA_HPG = NSA_HEADS // NSA_GROUPS
NSA_DK = 64
NSA_DV = 64
CMP_L = 32
CMP_STRIDE = 16
CMP_HID = 2 * NSA_DK
SLC_L = 64
SLC_TOP = 16
WIN = 512
NSA_QBLK = 64
FORCE_SCORE = 1e6
NSA_IN = NSA_HEADS * NSA_DK + 3 * NSA_GROUPS * (NSA_DK + NSA_DV) + 3 * NSA_HEADS

kernel_name = 'hybrid_ret_gla_rglru_nsa_trunk'


def rmsnorm(x, g):
    xf = x.astype(jnp.float32)
    y = xf * lax.rsqrt(jnp.mean(xf * xf, axis=-1, keepdims=True) + NORM_EPS)
    return (y * g.astype(jnp.float32)).astype(x.dtype)


def head_norm(o, g, center):
    of = o.astype(jnp.float32)
    if center:
        of = of - jnp.mean(of, axis=-1, keepdims=True)
    y = of * lax.rsqrt(jnp.mean(of * of, axis=-1, keepdims=True) + NORM_EPS)
    return y.reshape(*o.shape[:-2], -1) * g.astype(jnp.float32)


def masked_softmax(s, mask):
    s = jnp.where(mask, s.astype(jnp.float32), NEG_INF)
    m = jnp.max(s, axis=-1, keepdims=True)
    e = jnp.where(mask, jnp.exp(s - m), 0.0)
    return e / jnp.maximum(jnp.sum(e, axis=-1, keepdims=True), 1e-30)


def retention(h, w_in, gn, w_out):
    B, S, _ = h.shape
    H, DK, DV, C = RET_HEADS, RET_DK, RET_DV, RET_CHUNK
    n = S // C
    q, k, v, g = jnp.split(h @ w_in, [H * DK, 2 * H * DK, 2 * H * DK + H * DV], axis=-1)

    def to_chunks(t, d):
        return t.astype(jnp.float32).reshape(B, n, C, H, d).transpose(1, 0, 3, 2, 4)

    q = to_chunks(q, DK)
    k = to_chunks(k, DK) * DK ** -0.5
    v = to_chunks(v, DV)
    log_gamma = jnp.log1p(-jnp.exp2(-5.0 - jnp.arange(H, dtype=jnp.float32)))
    pos = jnp.arange(C, dtype=jnp.float32)
    diff = pos[:, None] - pos[None, :]
    decay_intra = jnp.where(diff >= 0, jnp.exp(log_gamma[:, None, None] * jnp.maximum(diff, 0.0)), 0.0)
    decay_q = jnp.exp(log_gamma[:, None] * (pos + 1.0))[None, :, :, None]
    decay_k = jnp.exp(log_gamma[:, None] * (C - 1.0 - pos))[None, :, :, None]
    decay_chunk = jnp.exp(log_gamma * C)[None, :, None, None]

    def step(state, inp):
        qc, kc, vc = inp
        scores = jnp.einsum('bhid,bhjd->bhij', qc, kc) * decay_intra
        o = jnp.einsum('bhij,bhje->bhie', scores, vc) + jnp.einsum('bhid,bhde->bhie', qc, state) * decay_q
        state = state * decay_chunk + jnp.einsum('bhjd,bhje->bhde', kc * decay_k, vc)
        return state, o

    _, o = lax.scan(step, jnp.zeros((B, H, DK, DV), jnp.float32), (q, k, v))
    o = o.transpose(1, 0, 3, 2, 4).reshape(B, S, H, DV)
    o = head_norm(o, gn, True) * jax.nn.silu(g.astype(jnp.float32))
    return (o @ w_out).astype(h.dtype)


def gla(h, w_in, w_gate_up, b_gate, gn, w_out):
    B, S, _ = h.shape
    H, DK, DV, C = GLA_HEADS, GLA_DK, GLA_DV, GLA_CHUNK
    n = S // C
    cuts = np.cumsum([H * DK, H * DK, H * DV, H * DV]).tolist()
    q, k, v, r, gl = jnp.split(h @ w_in, cuts, axis=-1)
    log_alpha = jax.nn.log_sigmoid((gl @ w_gate_up + b_gate).astype(jnp.float32)) / GLA_TAU

    def to_chunks(t, d):
        return t.astype(jnp.float32).reshape(B, n, C, H, d).transpose(1, 0, 3, 2, 4)

    q = to_chunks(q, DK) * DK ** -0.5
    k = to_chunks(k, DK)
    v = to_chunks(v, DV)
    la = to_chunks(log_alpha, DK)
    causal = jnp.tril(jnp.ones((C, C), dtype=bool))[:, :, None]

    def step(state, inp):
        qc, kc, vc, lac = inp
        b = jnp.cumsum(lac, axis=2)
        rel = jnp.where(causal, b[:, :, :, None, :] - b[:, :, None, :, :], -jnp.inf)
        A = jnp.einsum('bhid,bhjd,bhijd->bhij', qc, kc, jnp.exp(rel))
        b_last = b[:, :, -1:, :]
        o = jnp.einsum('bhij,bhje->bhie', A, vc) + jnp.einsum('bhid,bhde->bhie', qc * jnp.exp(b), state)
        state = state * jnp.exp(b_last)[:, :, 0, :, None] + jnp.einsum('bhjd,bhje->bhde', kc * jnp.exp(b_last - b), vc)
        return state, o

    _, o = lax.scan(step, jnp.zeros((B, H, DK, DV), jnp.float32), (q, k, v, la))
    o = o.transpose(1, 0, 3, 2, 4).reshape(B, S, H, DV)
    o = head_norm(o, gn, False) * jax.nn.silu(r.astype(jnp.float32))
    return (o @ w_out).astype(h.dtype)


def rglru_block(h, w_in, conv_w, conv_b, w_a, b_a, w_x, b_x, lam, w_out):
    B, S, _ = h.shape
    xb, yb = jnp.split(h @ w_in, 2, axis=-1)
    y = jax.nn.gelu(yb.astype(jnp.float32))
    xp = jnp.pad(xb, ((0, 0), (CONV_W - 1, 0), (0, 0)))
    xc = conv_b
    for tap in range(CONV_W):
        xc = xc + xp[:, tap:tap + S] * conv_w[tap]
    xr = xc.reshape(B, S, LRU_BLOCKS, LRU_BS)
    gate_r = jax.nn.sigmoid(jnp.einsum('bsnc,ncd->bsnd', xr, w_a).reshape(B, S, LRU_WIDTH) + b_a).astype(jnp.float32)
    gate_i = jax.nn.sigmoid(jnp.einsum('bsnc,ncd->bsnd', xr, w_x).reshape(B, S, LRU_WIDTH) + b_x).astype(jnp.float32)
    log_a = -LRU_C * gate_r * jax.nn.softplus(-lam.astype(jnp.float32))
    a = jnp.exp(log_a)
    u = jnp.sqrt(-jnp.expm1(2.0 * log_a)) * gate_i * xc.astype(jnp.float32)

    def combine(left, right):
        a1, b1 = left
        a2, b2 = right
        return a1 * a2, a2 * b1 + b2

    _, hs = lax.associative_scan(combine, (a, u), axis=1)
    return ((hs * y) @ w_out).astype(h.dtype)


def nsa(h, w_in, pe_k, w1_k, w2_k, pe_v, w1_v, w2_v, w_out):
    B, S, _ = h.shape
    H, G, HPG, DK, DV, QB = NSA_HEADS, NSA_GROUPS, NSA_HPG, NSA_DK, NSA_DV, NSA_QBLK
    sizes = [H * DK] + [G * DK, G * DV] * 3 + [3 * H]
    cuts = np.cumsum(sizes)[:-1].tolist()
    q, kc, vc, ks, vs, kw, vw, gl = jnp.split(h @ w_in, cuts, axis=-1)
    q = q.reshape(B, S, G, HPG, DK).transpose(0, 2, 3, 1, 4)
    gates = jax.nn.sigmoid(gl.astype(jnp.float32)).reshape(B, S, 3, G, HPG).transpose(0, 3, 4, 1, 2)

    def heads(t, d):
        return t.reshape(B, S, G, d).transpose(0, 2, 1, 3)

    n_cmp = (S - CMP_L) // CMP_STRIDE + 1
    cmp_start = jnp.arange(n_cmp) * CMP_STRIDE
    tok_idx = cmp_start[:, None] + jnp.arange(CMP_L)[None, :]

    def compress(t, pe, w1, w2, d):
        blocks = heads(t, d)[:, :, tok_idx] + pe
        return jax.nn.gelu(blocks.reshape(B, G, n_cmp, CMP_L * d) @ w1) @ w2

    k_cmp = compress(kc, pe_k, w1_k, w2_k, DK)
    v_cmp = compress(vc, pe_v, w1_v, w2_v, DV)
    cmp_end = cmp_start + CMP_L - 1

    n_slc = S // SLC_L
    k_slc = heads(ks, DK).reshape(B, G, n_slc, SLC_L, DK)
    v_slc = heads(vs, DV).reshape(B, G, n_slc, SLC_L, DV)
    slc_start = jnp.arange(n_slc) * SLC_L
    overlap = ((cmp_start[:, None] < slc_start[None, :] + SLC_L)
               & (cmp_start[:, None] + CMP_L > slc_start[None, :])).astype(jnp.float32)
    n_top = min(SLC_TOP, n_slc)
    blk_ids = jnp.arange(n_slc)

    k_win = jnp.pad(heads(kw, DK), ((0, 0), (0, 0), (WIN, 0), (0, 0)))
    v_win = jnp.pad(heads(vw, DV), ((0, 0), (0, 0), (WIN, 0), (0, 0)))

    slopes = jnp.exp2(-8.0 * (jnp.arange(H, dtype=jnp.float32) + 1.0) / H).reshape(1, G, HPG, 1, 1)
    scale = DK ** -0.5
    bi = jnp.arange(B)[:, None, None, None]
    gi = jnp.arange(G)[None, :, None, None]

    def block(i):
        q0 = i * QB
        qb = lax.dynamic_slice_in_dim(q, q0, QB, axis=3)
        t = q0 + jnp.arange(QB)
        dist_c = (t[:, None] - cmp_end[None, :]).astype(jnp.float32)
        s_c = jnp.einsum('bghqd,bgnd->bghqn', qb, k_cmp) * scale - slopes * dist_c
        p_c = masked_softmax(s_c, cmp_end[None, :] <= t[:, None])
        o_c = jnp.einsum('bghqn,bgnd->bghqd', p_c, v_cmp)
        imp = jnp.einsum('bghqn,nj->bgqj', p_c, overlap)
        cur = t // SLC_L
        forced = (blk_ids[None, :] == 0) | (blk_ids[None, :] == cur[:, None]) | (blk_ids[None, :] == cur[:, None] - 1)
        imp = jnp.where(forced, FORCE_SCORE, imp)
        imp = jnp.where(blk_ids[None, :] <= cur[:, None], imp, -1.0)
        _, idx = lax.top_k(imp, n_top)
        kg = k_slc[bi, gi, idx]
        vg = v_slc[bi, gi, idx].reshape(B, G, QB, n_top * SLC_L, DV)
        pos = idx[..., None] * SLC_L + jnp.arange(SLC_L)
        dist_s = (t[:, None, None] - pos).astype(jnp.float32)[:, :, None]
        s_s = jnp.einsum('bghqd,bgqkld->bghqkl', qb, kg) * scale - slopes[..., None] * dist_s
        m_s = (pos <= t[:, None, None]).reshape(B, G, QB, n_top * SLC_L)[:, :, None]
        p_s = masked_softmax(s_s.reshape(B, G, HPG, QB, n_top * SLC_L), m_s)
        o_s = jnp.einsum('bghqm,bgqmd->bghqd', p_s, vg)
        kwb = lax.dynamic_slice_in_dim(k_win, q0, WIN + QB, axis=2)
        vwb = lax.dynamic_slice_in_dim(v_win, q0, WIN + QB, axis=2)
        spos = q0 - WIN + jnp.arange(WIN + QB)
        dist_w = t[:, None] - spos[None, :]
        m_w = (spos[None, :] >= 0) & (dist_w >= 0) & (dist_w < WIN)
        s_w = jnp.einsum('bghqd,bgsd->bghqs', qb, kwb) * scale - slopes * dist_w.astype(jnp.float32)
        p_w = masked_softmax(s_w, m_w)
        o_w = jnp.einsum('bghqs,bgsd->bghqd', p_w, vwb)
        gb = lax.dynamic_slice_in_dim(gates, q0, QB, axis=3)
        return gb[..., 0:1] * o_c + gb[..., 1:2] * o_s + gb[..., 2:3] * o_w

    o = lax.map(block, jnp.arange(S // QB))
    o = o.transpose(1, 0, 4, 2, 3, 5).reshape(B, S, H * DV)
    return (o @ w_out).astype(h.dtype)


def setup_inputs(seed: int = 0) -> dict:
    key = jax.random.key(seed)
    ks = iter(jax.random.split(key, 40))

    def w(shape, fan_in):
        return jax.random.normal(next(ks), shape, jnp.float32) * fan_in ** -0.5

    def gain(shape):
        return 1.0 + 0.02 * jax.random.normal(next(ks), shape, jnp.float32)

    def small(shape, s):
        return s * jax.random.normal(next(ks), shape, jnp.float32)

    x = jax.random.normal(next(ks), (BATCH, SEQ, D_MODEL), jnp.float32)
    u = jax.random.uniform(next(ks), (N_LRU, LRU_WIDTH), jnp.float32, 0.9, 0.999)
    a_base = u ** (1.0 / LRU_C)
    lru_lambda = jnp.log(a_base) - jnp.log1p(-a_base)
    return {
        'x': x,
        'norm_mix_pre': gain((DEPTH, D_MODEL)),
        'norm_mix_post': gain((DEPTH, D_MODEL)),
        'norm_mlp_pre': gain((DEPTH, D_MODEL)),
        'norm_mlp_post': gain((DEPTH, D_MODEL)),
        'mlp_w_up': w((DEPTH, D_MODEL, D_FF), D_MODEL),
        'mlp_w_down': w((DEPTH, D_FF, D_MODEL), D_FF),
        'ret_w_in': w((N_RET, D_MODEL, RET_IN), D_MODEL),
        'ret_gn': gain((N_RET, RET_HEADS * RET_DV)),
        'ret_w_out': w((N_RET, RET_HEADS * RET_DV, D_MODEL), RET_HEADS * RET_DV),
        'gla_w_in': w((N_GLA, D_MODEL, GLA_IN), D_MODEL),
        'gla_w_gate_up': w((N_GLA, GLA_RANK, GLA_HEADS * GLA_DK), GLA_RANK),
        'gla_b_gate': small((N_GLA, GLA_HEADS * GLA_DK), 0.5),
        'gla_gn': gain((N_GLA, GLA_HEADS * GLA_DV)),
        'gla_w_out': w((N_GLA, GLA_HEADS * GLA_DV, D_MODEL), GLA_HEADS * GLA_DV),
        'lru_w_in': w((N_LRU, D_MODEL, 2 * LRU_WIDTH), D_MODEL),
        'lru_conv_w': w((N_LRU, CONV_W, LRU_WIDTH), CONV_W),
        'lru_conv_b': small((N_LRU, LRU_WIDTH), 0.01),
        'lru_w_a': w((N_LRU, LRU_BLOCKS, LRU_BS, LRU_BS), LRU_BS),
        'lru_b_a': small((N_LRU, LRU_WIDTH), 0.01),
        'lru_w_x': w((N_LRU, LRU_BLOCKS, LRU_BS, LRU_BS), LRU_BS),
        'lru_b_x': small((N_LRU, LRU_WIDTH), 0.01),
        'lru_lambda': lru_lambda,
        'lru_w_out': w((N_LRU, LRU_WIDTH, D_MODEL), LRU_WIDTH),
        'nsa_w_in': w((N_NSA, D_MODEL, NSA_IN), D_MODEL),
        'nsa_pe_k': small((N_NSA, CMP_L, NSA_DK), 0.1),
        'nsa_w1_k': w((N_NSA, CMP_L * NSA_DK, CMP_HID), CMP_L * NSA_DK),
        'nsa_w2_k': w((N_NSA, CMP_HID, NSA_DK), CMP_HID),
        'nsa_pe_v': small((N_NSA, CMP_L, NSA_DV), 0.1),
        'nsa_w1_v': w((N_NSA, CMP_L * NSA_DV, CMP_HID), CMP_L * NSA_DV),
        'nsa_w2_v': w((N_NSA, CMP_HID, NSA_DV), CMP_HID),
        'nsa_w_out': w((N_NSA, NSA_HEADS * NSA_DV, D_MODEL), NSA_HEADS * NSA_DV),
    }


def reference(x, norm_mix_pre, norm_mix_post, norm_mlp_pre, norm_mlp_post, mlp_w_up, mlp_w_down,
              ret_w_in, ret_gn, ret_w_out,
              gla_w_in, gla_w_gate_up, gla_b_gate, gla_gn, gla_w_out,
              lru_w_in, lru_conv_w, lru_conv_b, lru_w_a, lru_b_a, lru_w_x, lru_b_x, lru_lambda, lru_w_out,
              nsa_w_in, nsa_pe_k, nsa_w1_k, nsa_w2_k, nsa_pe_v, nsa_w1_v, nsa_w2_v, nsa_w_out):
    for i in range(DEPTH):
        m, j = i % N_MIXERS, i // N_MIXERS
        h = rmsnorm(x, norm_mix_pre[i])
        if m == 0:
            h = retention(h, ret_w_in[j], ret_gn[j], ret_w_out[j])
        elif m == 1:
            h = gla(h, gla_w_in[j], gla_w_gate_up[j], gla_b_gate[j], gla_gn[j], gla_w_out[j])
        elif m == 2:
            h = rglru_block(h, lru_w_in[j], lru_conv_w[j], lru_conv_b[j], lru_w_a[j], lru_b_a[j],
                            lru_w_x[j], lru_b_x[j], lru_lambda[j], lru_w_out[j])
        else:
            h = nsa(h, nsa_w_in[j], nsa_pe_k[j], nsa_w1_k[j], nsa_w2_k[j],
                    nsa_pe_v[j], nsa_w1_v[j], nsa_w2_v[j], nsa_w_out[j])
        x = x + rmsnorm(h, norm_mix_post[i])
        h = rmsnorm(x, norm_mlp_pre[i])
        h = jnp.square(jax.nn.relu(h @ mlp_w_up[i])) @ mlp_w_down[i]
        x = x + rmsnorm(h, norm_mlp_post[i])
    return x
```

```python
import functools
import math

import numpy as np
import jax
import jax.numpy as jnp
from jax import lax
from jax.experimental import pallas as pl
from jax.experimental.pallas import tpu as pltpu

F32 = jnp.float32
BF16 = jnp.bfloat16
HIGHEST = lax.Precision.HIGHEST

D_MODEL = 1024
D_FF = 4 * D_MODEL
NORM_EPS = 1e-6
NEG_INF = -1e30

RET_HEADS, RET_DK, RET_DV = 4, 256, 512
RET_CHUNK = 128

GLA_HEADS, GLA_DK, GLA_DV = 4, 128, 256
GLA_RANK = 16
GLA_TAU = 16.0
GLA_CHUNK = 128

LRU_WIDTH = D_MODEL
LRU_BLOCKS = 8
LRU_BS = LRU_WIDTH // LRU_BLOCKS
CONV_W = 4
LRU_C = 8.0
LRU_TCHUNK = 256

NSA_HEADS, NSA_GROUPS, NSA_HPG = 16, 2, 8
NSA_DK = 64
NSA_DV = 64
CMP_L, CMP_STRIDE = 32, 16
CMP_HID = 2 * NSA_DK
SLC_L, SLC_TOP = 64, 16
WIN = 512
FORCE_SCORE = 1e6
NSA_Q = 128
NSA_TK = 512

LANES = 128
SUBLANES = 8
VMEM_LIMIT = 56 * 1024 * 1024


def _cparams(*sem):
    return pltpu.CompilerParams(dimension_semantics=sem, vmem_limit_bytes=VMEM_LIMIT)


def _rms(x, g):
    return x * lax.rsqrt(jnp.mean(x * x, axis=-1, keepdims=True) + NORM_EPS) * g


def _dot(a, b):
    return jnp.dot(a, b, preferred_element_type=F32)


def _dot_nt(a, b):
    return lax.dot_general(a, b, (((1,), (1,)), ((), ())), preferred_element_type=F32)


def _dot_tn(a, b):
    return lax.dot_general(a, b, (((0,), (0,)), ((), ())), preferred_element_type=F32)


def _gelu_tanh(x):
    return 0.5 * x * (1.0 + jnp.tanh(math.sqrt(2.0 / math.pi) * (x + 0.044715 * (x * x * x))))


def _sigmoid(x):
    return 1.0 / (1.0 + jnp.exp(-x))


def _norm_proj_kernel(x_ref, g_ref, w_ref, o_ref, h_ref):
    @pl.when(pl.program_id(1) == 0)
    def _():
        h_ref[...] = _rms(x_ref[...], g_ref[...]).astype(BF16)

    o_ref[...] = _dot(h_ref[...], w_ref[...])


def norm_proj(x2d, g, w_bf16, tn, tm=1024):
    T, D = x2d.shape
    N = w_bf16.shape[1]
    assert T % tm == 0 and N % tn == 0
    return pl.pallas_call(
        _norm_proj_kernel,
        grid=(T // tm, N // tn),
        in_specs=[
            pl.BlockSpec((tm, D), lambda i, j: (i, 0)),
            pl.BlockSpec((1, D), lambda i, j: (0, 0)),
            pl.BlockSpec((D, tn), lambda i, j: (0, j)),
        ],
        out_specs=pl.BlockSpec((tm, tn), lambda i, j: (i, j)),
        out_shape=jax.ShapeDtypeStruct((T, N), F32),
        scratch_shapes=[pltpu.VMEM((tm, D), BF16)],
        compiler_params=_cparams("parallel", "arbitrary"),
        name="norm_proj",
    )(x2d, g.reshape(1, D), w_bf16)


def _proj_norm_res_kernel(a_ref, w_ref, g_ref, x_ref, o_ref):
    h = _dot(a_ref[...].astype(BF16), w_ref[...])
    o_ref[...] = x_ref[...] + _rms(h, g_ref[...])


def proj_norm_res(a2d, w_bf16, g, x2d, tm=512):
    T, K = a2d.shape
    D = w_bf16.shape[1]
    assert T % tm == 0
    return pl.pallas_call(
        _proj_norm_res_kernel,
        grid=(T // tm,),
        in_specs=[
            pl.BlockSpec((tm, K), lambda i: (i, 0)),
            pl.BlockSpec((K, D), lambda i: (0, 0)),
            pl.BlockSpec((1, D), lambda i: (0, 0)),
            pl.BlockSpec((tm, D), lambda i: (i, 0)),
        ],
        out_specs=pl.BlockSpec((tm, D), lambda i: (i, 0)),
        out_shape=jax.ShapeDtypeStruct((T, D), F32),
        compiler_params=_cparams("parallel"),
        name="proj_norm_res",
    )(a2d, w_bf16, g.reshape(1, D), x2d)


MLP_FCHUNK = 512


def _mlp_kernel(x_ref, g1_ref, wu_ref, wd_ref, g2_ref, o_ref):
    x = x_ref[...]
    h = _rms(x, g1_ref[...]).astype(BF16)
    acc = jnp.zeros(x.shape, F32)
    for c in range(D_FF // MLP_FCHUNK):
        sl = slice(c * MLP_FCHUNK, (c + 1) * MLP_FCHUNK)
        u = jnp.maximum(_dot(h, wu_ref[:, sl]), 0.0)
        acc = acc + _dot((u * u).astype(BF16), wd_ref[sl, :])
    o_ref[...] = x + _rms(acc, g2_ref[...])


def mlp_block(x2d, g1, wu_bf16, wd_bf16, g2, tm=512):
    T, D = x2d.shape
    F = wu_bf16.shape[1]
    return pl.pallas_call(
        _mlp_kernel,
        grid=(T // tm,),
        in_specs=[
            pl.BlockSpec((tm, D), lambda i: (i, 0)),
            pl.BlockSpec((1, D), lambda i: (0, 0)),
            pl.BlockSpec((D, F), lambda i: (0, 0)),
            pl.BlockSpec((F, D), lambda i: (0, 0)),
            pl.BlockSpec((1, D), lambda i: (0, 0)),
        ],
        out_specs=pl.BlockSpec((tm, D), lambda i: (i, 0)),
        out_shape=jax.ShapeDtypeStruct((T, D), F32),
        compiler_params=_cparams("parallel"),
        name="mlp",
    )(x2d, g1.reshape(1, D), wu_bf16, wd_bf16, g2.reshape(1, D))


def _ret_kernel(q_ref, k_ref, v_ref, g_ref, gn_ref, di_ref, dq_ref, dk_ref, dc_ref, o_ref, st_ref):
    @pl.when(pl.program_id(2) == 0)
    def _():
        st_ref[...] = jnp.zeros_like(st_ref)

    q = q_ref[...]
    k = k_ref[...] * (RET_DK ** -0.5)
    vb = v_ref[...].astype(BF16)
    st = st_ref[...]
    scores = _dot_nt(q.astype(BF16), k.astype(BF16)) * di_ref[...]
    o = _dot(scores.astype(BF16), vb) + _dot((q * dq_ref[...]).astype(BF16), st.astype(BF16))
    st_ref[...] = st * dc_ref[...] + _dot_tn((k * dk_ref[...]).astype(BF16), vb)
    o = o - jnp.mean(o, axis=-1, keepdims=True)
    gate = g_ref[...]
    o_ref[...] = _rms(o, gn_ref[...]) * (gate * _sigmoid(gate))


def retention_mix(proj, gn, B, S):
    H, DK, DV, C = RET_HEADS, RET_DK, RET_DV, RET_CHUNK
    log_gamma = jnp.log1p(-jnp.exp2(-5.0 - jnp.arange(H, dtype=F32)))
    pos = jnp.arange(C, dtype=F32)
    diff = pos[:, None] - pos[None, :]
    d_intra = jnp.where(diff >= 0, jnp.exp(log_gamma[:, None, None] * jnp.maximum(diff, 0.0)), 0.0)
    d_q = jnp.exp(log_gamma[:, None] * (pos + 1.0))[:, :, None]
    d_k = jnp.exp(log_gamma[:, None] * (C - 1.0 - pos))[:, :, None]
    d_c = jnp.exp(log_gamma * C)[:, None, None]
    kq, kk = H * DK // DK, 0
    return pl.pallas_call(
        _ret_kernel,
        grid=(B, H, S // C),
        in_specs=[
            pl.BlockSpec((None, C, DK), lambda b, h, c: (b, c, h)),
            pl.BlockSpec((None, C, DK), lambda b, h, c: (b, c, H + h)),
            pl.BlockSpec((None, C, DV), lambda b, h, c: (b, c, (2 * H * DK) // DV + h)),
            pl.BlockSpec((None, C, DV), lambda b, h, c: (b, c, (2 * H * DK) // DV + H + h)),
            pl.BlockSpec((1, DV), lambda b, h, c: (0, h)),
            pl.BlockSpec((None, C, C), lambda b, h, c: (h, 0, 0)),
            pl.BlockSpec((None, C, 1), lambda b, h, c: (h, 0, 0)),
            pl.BlockSpec((None, C, 1), lambda b, h, c: (h, 0, 0)),
            pl.BlockSpec((None, 1, 1), lambda b, h, c: (h, 0, 0)),
        ],
        out_specs=pl.BlockSpec((None, C, DV), lambda b, h, c: (b, c, h)),
        out_shape=jax.ShapeDtypeStruct((B, S, H * DV), F32),
        scratch_shapes=[pltpu.VMEM((DK, DV), F32)],
        compiler_params=_cparams("parallel", "parallel", "arbitrary"),
        name="retention",
    )(proj, proj, proj, proj, gn.reshape(1, H * DV), d_intra, d_q, d_k, d_c)


def _gla_level_sizes(C):
    sizes, s = [], 2
    while s <= C:
        sizes.append(s)
        s *= 2
    return sizes


def _gla_constants(C):
    t = np.arange(C)[:, None]
    r = np.arange(C)[None, :]
    mats = [(r <= t)]
    mats.append(r > t)
    masks = [np.eye(C, dtype=bool)]
    for bs in _gla_level_sizes(C):
        half = bs // 2
        mid = (t // bs) * bs + half - 1
        second = (t % bs) >= half
        m = np.where(second, (r > mid) & (r <= t), (r > t) & (r <= mid))
        mats.append(m)
        same = (t // bs) == (r // bs)
        masks.append(same & second & ((r % bs) < half))
    return (np.concatenate(mats, 0).astype(np.float32), np.stack(masks, 0).astype(np.float32))


def _gla_kernel(q_ref, k_ref, v_ref, r_ref, gl_ref, wg_ref, bg_ref, gn_ref, mall_ref, mask_ref, o_ref, st_ref):
    C, DK = GLA_CHUNK, GLA_DK

    @pl.when(pl.program_id(2) == 0)
    def _():
        st_ref[...] = jnp.zeros_like(st_ref)

    z = jnp.dot(gl_ref[...], wg_ref[...], preferred_element_type=F32, precision=HIGHEST) + bg_ref[...]
    la = (jnp.minimum(z, 0.0) - jnp.log1p(jnp.exp(-jnp.abs(z)))) * (1.0 / GLA_TAU)
    la_hi = la.astype(BF16)
    la_lo = (la - la_hi.astype(F32)).astype(BF16)
    sums = _dot(mall_ref[...], jnp.concatenate([la_hi, la_lo], axis=1))
    expo = sums[:, :DK] + sums[:, DK:]
    decay = jnp.exp(expo)
    e_b = decay[0:C]
    e_rest = decay[C:2 * C]
    e_last = decay[C - 1:C]

    q = q_ref[...] * (DK ** -0.5)
    k = k_ref[...]
    vb = v_ref[...].astype(BF16)
    a = _dot_nt(q.astype(BF16), k.astype(BF16)) * mask_ref[0]
    for lvl in range(len(_gla_level_sizes(C))):
        e = decay[(2 + lvl) * C:(3 + lvl) * C]
        a = a + _dot_nt((q * e).astype(BF16), (k * e).astype(BF16)) * mask_ref[1 + lvl]
    st = st_ref[...]
    o = _dot(a.astype(BF16), vb) + _dot_nt((q * e_b).astype(BF16), st.astype(BF16))
    st_ref[...] = st * e_last + _dot_tn(vb, (k * e_rest).astype(BF16))
    gate = r_ref[...]
    o_ref[...] = _rms(o, gn_ref[...]) * (gate * _sigmoid(gate))


def gla_mix(proj, w_gate_pad, b_gate, gn, B, S):
    H, DK, DV, C = GLA_HEADS, GLA_DK, GLA_DV, GLA_CHUNK
    mall, masks = _gla_constants(C)
    nlev = masks.shape[0]
    v0 = (2 * H * DK) // DV
    gl_blk = (2 * H * DK + 2 * H * DV) // LANES
    return pl.pallas_call(
        _gla_kernel,
        grid=(B, H, S // C),
        in_specs=[
            pl.BlockSpec((None, C, DK), lambda b, h, c: (b, c, h)),
            pl.BlockSpec((None, C, DK), lambda b, h, c: (b, c, H + h)),
            pl.BlockSpec((None, C, DV), lambda b, h, c: (b, c, v0 + h)),
            pl.BlockSpec((None, C, DV), lambda b, h, c: (b, c, v0 + H + h)),
            pl.BlockSpec((None, C, LANES), lambda b, h, c: (b, c, gl_blk)),
            pl.BlockSpec((LANES, DK), lambda b, h, c: (0, h)),
            pl.BlockSpec((1, DK), lambda b, h, c: (0, h)),
            pl.BlockSpec((1, DV), lambda b, h, c: (0, h)),
            pl.BlockSpec(mall.shape, lambda b, h, c: (0, 0)),
            pl.BlockSpec(masks.shape, lambda b, h, c: (0, 0, 0)),
        ],
        out_specs=pl.BlockSpec((None, C, DV), lambda b, h, c: (b, c, h)),
        out_shape=jax.ShapeDtypeStruct((B, S, H * DV), F32),
        scratch_shapes=[pltpu.VMEM((DV, DK), F32)],
        compiler_params=_cparams("parallel", "parallel", "arbitrary"),
        name="gla",
    )(proj, proj, proj, proj, proj, w_gate_pad, b_gate.reshape(1, H * DK), gn.reshape(1, H * DV),
      jnp.asarray(mall, BF16), jnp.asarray(masks, F32))


def _lru_kernel(x_ref, y_ref, cw_ref, cb_ref, wa_ref, ba_ref, wx_ref, bx_ref, lam_ref, o_ref,
                xbuf, a_s, u_s, hcar):
    TC, W = LRU_TCHUNK, LRU_WIDTH

    @pl.when(pl.program_id(1) == 0)
    def _():
        xbuf[0:SUBLANES, :] = jnp.zeros((SUBLANES, W), F32)
        hcar[...] = jnp.zeros_like(hcar)

    xbuf[SUBLANES:SUBLANES + TC, :] = x_ref[...]
    xc = cb_ref[...]
    for tap in range(CONV_W):
        off = SUBLANES - (CONV_W - 1) + tap
        xc = xc + xbuf[off:off + TC, :] * cw_ref[tap:tap + 1, :]
    xbuf[0:SUBLANES, :] = xbuf[TC:TC + SUBLANES, :]

    xcb = xc.astype(BF16)
    for n in range(LRU_BLOCKS):
        sl = slice(n * LRU_BS, (n + 1) * LRU_BS)
        a_s[:, sl] = _dot(xcb[:, sl], wa_ref[n])
        u_s[:, sl] = _dot(xcb[:, sl], wx_ref[n])
    gate_r = _sigmoid(a_s[...] + ba_ref[...])
    gate_i = _sigmoid(u_s[...] + bx_ref[...])
    nl = -lam_ref[...]
    softplus = jnp.maximum(nl, 0.0) + jnp.log1p(jnp.exp(-jnp.abs(nl)))
    a = jnp.exp(-LRU_C * gate_r * softplus)
    a_s[...] = a
    u_s[...] = jnp.sqrt(1.0 - a * a) * gate_i * xc

    row = lax.broadcasted_iota(jnp.int32, (SUBLANES, W), 0)

    def body(r, h):
        sl = pl.ds(pl.multiple_of(r * SUBLANES, SUBLANES), SUBLANES)
        a8 = a_s[sl, :]
        u8 = u_s[sl, :]
        d = 1
        while d < SUBLANES:
            keep = row >= d
            a_sh = jnp.where(keep, pltpu.roll(a8, d, 0), 1.0)
            u_sh = jnp.where(keep, pltpu.roll(u8, d, 0), 0.0)
            u8 = a8 * u_sh + u8
            a8 = a8 * a_sh
            d *= 2
        h8 = a8 * h + u8
        u_s[sl, :] = h8
        return jnp.broadcast_to(h8[SUBLANES - 1:SUBLANES, :], (SUBLANES, W))

    hcar[...] = lax.fori_loop(0, TC // SUBLANES, body, hcar[...])
    o_ref[...] = u_s[...] * _gelu_tanh(y_ref[...])


def lru_mix(proj, conv_w, conv_b, w_a, b_a, w_x, b_x, lam, B, S):
    TC, W = LRU_TCHUNK, LRU_WIDTH
    row = lambda a: a.reshape(1, W)
    vec = pl.BlockSpec((1, W), lambda b, t: (0, 0))
    blk = pl.BlockSpec((LRU_BLOCKS, LRU_BS, LRU_BS), lambda b, t: (0, 0, 0))
    return pl.pallas_call(
        _lru_kernel,
        grid=(B, S // TC),
        in_specs=[
            pl.BlockSpec((None, TC, W), lambda b, t: (b, t, 0)),
            pl.BlockSpec((None, TC, W), lambda b, t: (b, t, 1)),
            pl.BlockSpec((CONV_W, W), lambda b, t: (0, 0)),
            vec, blk, vec, blk, vec, vec,
        ],
        out_specs=pl.BlockSpec((None, TC, W), lambda b, t: (b, t, 0)),
        out_shape=jax.ShapeDtypeStruct((B, S, W), F32),
        scratch_shapes=[
            pltpu.VMEM((TC + SUBLANES, W), F32),
            pltpu.VMEM((TC, W), F32),
            pltpu.VMEM((TC, W), F32),
            pltpu.VMEM((SUBLANES, W), F32),
        ],
        compiler_params=_cparams("parallel", "arbitrary"),
        name="rglru",
    )(proj, proj, conv_w, row(conv_b), w_a.astype(BF16), row(b_a), w_x.astype(BF16), row(b_x), row(lam))


def _nsa_cmp_kernel(kc_ref, vc_ref, pek_ref, w1k_ref, w2k_ref, pev_ref, w1v_ref, w2v_ref, ko_ref, vo_ref, *, nblk):
    def compress(t_ref, pe_ref, w1_ref, w2_ref, o_ref, d):
        ya = [jnp.zeros((nblk, CMP_HID), F32) for _ in range(NSA_GROUPS)]
        yb = [jnp.zeros((nblk, CMP_HID), F32) for _ in range(NSA_GROUPS)]
        for l in range(CMP_STRIDE):
            both = t_ref[pl.ds(l, nblk, stride=CMP_STRIDE), :]
            for g in range(NSA_GROUPS):
                rows = both[:, g * d:(g + 1) * d]
                lo = (rows + pe_ref[l:l + 1, :]).astype(BF16)
                hi = (rows + pe_ref[CMP_STRIDE + l:CMP_STRIDE + l + 1, :]).astype(BF16)
                ya[g] = ya[g] + _dot(lo, w1_ref[l * d:(l + 1) * d, :])
                yb[g] = yb[g] + _dot(hi, w1_ref[(CMP_STRIDE + l) * d:(CMP_STRIDE + l + 1) * d, :])
        for g in range(NSA_GROUPS):
            shifted = jnp.concatenate([yb[g][1:], jnp.zeros((1, CMP_HID), F32)], axis=0)
            hid = _gelu_tanh(ya[g] + shifted).astype(BF16)
            o_ref[g] = _dot(hid, w2_ref[...])

    compress(kc_ref, pek_ref, w1k_ref, w2k_ref, ko_ref, NSA_DK)
    compress(vc_ref, pev_ref, w1v_ref, w2v_ref, vo_ref, NSA_DV)


def nsa_compress(proj, pe_k, w1_k, w2_k, pe_v, w1_v, w2_v, B, S):
    nblk = S // CMP_STRIDE
    G = NSA_GROUPS
    kc_blk = (NSA_HEADS * NSA_DK) // LANES
    full2 = lambda a: pl.BlockSpec(a.shape, lambda b: (0, 0))
    w1k, w2k, w1v, w2v = (a.astype(BF16) for a in (w1_k, w2_k, w1_v, w2_v))
    out_sd = jax.ShapeDtypeStruct((B, G, nblk, NSA_DK), F32)
    return pl.pallas_call(
        functools.partial(_nsa_cmp_kernel, nblk=nblk),
        grid=(B,),
        in_specs=[
            pl.BlockSpec((None, S, LANES), lambda b: (b, 0, kc_blk)),
            pl.BlockSpec((None, S, LANES), lambda b: (b, 0, kc_blk + 1)),
            full2(pe_k), full2(w1k), full2(w2k), full2(pe_v), full2(w1v), full2(w2v),
        ],
        out_specs=[pl.BlockSpec((None, G, nblk, NSA_DK), lambda b: (b, 0, 0, 0))] * 2,
        out_shape=[out_sd, out_sd],
        compiler_params=_cparams("parallel"),
        name="nsa_compress",
    )(proj, proj, pe_k, w1k, w2k, pe_v, w1v, w2v)


def _alibi_slope(head):
    return float(2.0 ** (-8.0 * (head + 1.0) / NSA_HEADS))


def _nsa_kernel(q_ref, gate_ref, kc_ref, vc_ref, ks_ref, vs_ref, kw_ref, vw_ref, ov_ref, ex_ref, o_ref,
                m_s, l_s, acc_s, *, seq):
    Q, TK, HPG, DK, DV = NSA_Q, NSA_TK, NSA_HPG, NSA_DK, NSA_DV
    ncmp = seq // CMP_STRIDE
    nslc = seq // SLC_L
    scale = DK ** -0.5
    qi = pl.program_id(1)
    q0 = qi * Q
    t_col = q0 + lax.broadcasted_iota(jnp.int32, (Q, 1), 0)
    gates = _sigmoid(gate_ref[...])

    for g in range(NSA_GROUPS):
        qg = [q_ref[:, (g * HPG + h) * DK:(g * HPG + h + 1) * DK].astype(BF16) for h in range(HPG)]
        slopes = [_alibi_slope(g * HPG + h) for h in range(HPG)]

        kcmp = kc_ref[g].astype(BF16)
        vcmp = vc_ref[g].astype(BF16)
        cend = CMP_STRIDE * lax.broadcasted_iota(jnp.int32, (1, ncmp), 1) + (CMP_L - 1)
        valid_c = cend <= t_col
        dist_c = (t_col - cend).astype(F32)
        psum = jnp.zeros((Q, ncmp), F32)
        o_cmp = []
        for h in range(HPG):
            s = _dot_nt(qg[h], kcmp) * scale - slopes[h] * dist_c
            s = jnp.where(valid_c, s, NEG_INF)
            m = jnp.max(s, axis=-1, keepdims=True)
            e = jnp.where(valid_c, jnp.exp(s - m), 0.0)
            p = e / jnp.maximum(jnp.sum(e, axis=-1, keepdims=True), 1e-30)
            psum = psum + p
            o_cmp.append(_dot(p.astype(BF16), vcmp))

        imp = jnp.dot(psum, ov_ref[...], preferred_element_type=F32, precision=HIGHEST)
        jrow = lax.broadcasted_iota(jnp.int32, (1, nslc), 1)
        cur = t_col // SLC_L
        forced = (jrow == 0) | (jrow == cur) | (jrow == cur - 1)
        imp = jnp.where(forced, FORCE_SCORE, imp)
        imp = jnp.where(jrow <= cur, imp, -1.0)
        imp_t = imp.T
        jcol = lax.broadcasted_iota(jnp.int32, (nslc, Q), 0)
        rank = jnp.zeros((nslc, Q), F32)
        for j2 in range(nslc):
            other = imp_t[j2:j2 + 1, :]
            rank = rank + jnp.where(jcol > j2, jnp.where(other >= imp_t, 1.0, 0.0),
                                    jnp.where(other > imp_t, 1.0, 0.0))
        sel = jnp.where(rank < float(min(SLC_TOP, nslc)), 1.0, 0.0).T.astype(BF16)

        for h in range(HPG):
            m_s[h] = jnp.full((Q, 1), NEG_INF, F32)
            l_s[h] = jnp.zeros((Q, 1), F32)
            acc_s[h] = jnp.zeros((Q, DV), F32)

        def tile(kt, carry):
            k0 = pl.multiple_of(kt * TK, TK)
            kt_b = ks_ref[pl.ds(k0, TK), g * DK:(g + 1) * DK].astype(BF16)
            vt_b = vs_ref[pl.ds(k0, TK), g * DV:(g + 1) * DV].astype(BF16)
            pos = k0 + lax.broadcasted_iota(jnp.int32, (1, TK), 1)
            picked = _dot(sel, ex_ref[:, pl.ds(k0, TK)])
            ok = (picked > 0.5) & (pos <= t_col)
            posf = pos.astype(F32)
            for h in range(HPG):
                s = _dot_nt(qg[h], kt_b) * scale + slopes[h] * posf
                s = jnp.where(ok, s, NEG_INF)
                m_old = m_s[h]
                m_new = jnp.maximum(m_old, jnp.max(s, axis=-1, keepdims=True))
                alpha = jnp.exp(m_old - m_new)
                p = jnp.where(ok, jnp.exp(s - m_new), 0.0)
                l_s[h] = alpha * l_s[h] + jnp.sum(p, axis=-1, keepdims=True)
                acc_s[h] = alpha * acc_s[h] + _dot(p.astype(BF16), vt_b)
                m_s[h] = m_new
            return carry

        lax.fori_loop(0, (q0 + Q + TK - 1) // TK, tile, 0)

        w0 = pl.multiple_of(jnp.maximum(q0 - WIN, 0), Q)
        kw_b = kw_ref[pl.ds(w0, WIN + Q), g * DK:(g + 1) * DK].astype(BF16)
        vw_b = vw_ref[pl.ds(w0, WIN + Q), g * DV:(g + 1) * DV].astype(BF16)
        wpos = w0 + lax.broadcasted_iota(jnp.int32, (1, WIN + Q), 1)
        dist_w = t_col - wpos
        ok_w = (dist_w >= 0) & (dist_w < WIN)
        wposf = wpos.astype(F32)
        for h in range(HPG):
            s = _dot_nt(qg[h], kw_b) * scale + slopes[h] * wposf
            s = jnp.where(ok_w, s, NEG_INF)
            m = jnp.max(s, axis=-1, keepdims=True)
            e = jnp.where(ok_w, jnp.exp(s - m), 0.0)
            o_w = _dot(e.astype(BF16), vw_b) / jnp.maximum(jnp.sum(e, axis=-1, keepdims=True), 1e-30)
            o_sel = acc_s[h] / jnp.maximum(l_s[h], 1e-30)
            col = g * HPG + h
            out = (gates[:, col:col + 1] * o_cmp[h]
                   + gates[:, NSA_HEADS + col:NSA_HEADS + col + 1] * o_sel
                   + gates[:, 2 * NSA_HEADS + col:2 * NSA_HEADS + col + 1] * o_w)
            o_ref[:, col * DV:(col + 1) * DV] = out


def nsa_attend(proj, k_cmp, v_cmp, B, S):
    Q, G = NSA_Q, NSA_GROUPS
    ncmp, nslc = S // CMP_STRIDE, S // SLC_L
    assert S >= WIN + Q and S % NSA_TK == 0
    n = np.arange(ncmp)[:, None]
    j = np.arange(nslc)[None, :]
    overlap = ((n * CMP_STRIDE < (j + 1) * SLC_L) & (n * CMP_STRIDE + CMP_L > j * SLC_L)
               & (n < (S - CMP_L) // CMP_STRIDE + 1)).astype(np.float32)
    expand = (np.arange(S)[None, :] // SLC_L == np.arange(nslc)[:, None]).astype(np.float32)
    qw = NSA_HEADS * NSA_DK
    kv0 = qw // LANES
    seqblk = lambda c: pl.BlockSpec((None, S, LANES), lambda b, i: (b, 0, c))
    cmpblk = pl.BlockSpec((None, G, ncmp, NSA_DK), lambda b, i: (b, 0, 0, 0))
    return pl.pallas_call(
        functools.partial(_nsa_kernel, seq=S),
        grid=(B, S // Q),
        in_specs=[
            pl.BlockSpec((None, Q, qw), lambda b, i: (b, i, 0)),
            pl.BlockSpec((None, Q, LANES), lambda b, i: (b, i, kv0 + 6)),
            cmpblk, cmpblk,
            seqblk(kv0 + 2), seqblk(kv0 + 3), seqblk(kv0 + 4), seqblk(kv0 + 5),
            pl.BlockSpec((ncmp, nslc), lambda b, i: (0, 0)),
            pl.BlockSpec((nslc, S), lambda b, i: (0, 0)),
        ],
        out_specs=pl.BlockSpec((None, Q, NSA_HEADS * NSA_DV), lambda b, i: (b, i, 0)),
        out_shape=jax.ShapeDtypeStruct((B, S, NSA_HEADS * NSA_DV), F32),
        scratch_shapes=[
            pltpu.VMEM((NSA_HPG, Q, 1), F32),
            pltpu.VMEM((NSA_HPG, Q, 1), F32),
            pltpu.VMEM((NSA_HPG, Q, NSA_DV), F32),
        ],
        compiler_params=_cparams("parallel", "arbitrary"),
        name="nsa_attend",
    )(proj, proj, k_cmp, v_cmp, proj, proj, proj, proj, jnp.asarray(overlap, F32), jnp.asarray(expand, BF16))


def _pad_cols(w, n):
    return jnp.pad(w, ((0, 0), (0, n - w.shape[1])))


def kernel(x, norm_mix_pre, norm_mix_post, norm_mlp_pre, norm_mlp_post, mlp_w_up, mlp_w_down, ret_w_in, ret_gn, ret_w_out, gla_w_in, gla_w_gate_up, gla_b_gate, gla_gn, gla_w_out, lru_w_in, lru_conv_w, lru_conv_b, lru_w_a, lru_b_a, lru_w_x, lru_b_x, lru_lambda, lru_w_out, nsa_w_in, nsa_pe_k, nsa_w1_k, nsa_w2_k, nsa_pe_v, nsa_w1_v, nsa_w2_v, nsa_w_out):
    B, S, D = x.shape
    T = B * S
    depth = norm_mix_pre.shape[0]
    x2 = x.reshape(T, D)
    for i in range(depth):
        m, j = i % 4, i // 4
        if m == 0:
            proj = norm_proj(x2, norm_mix_pre[i], ret_w_in[j].astype(BF16), tn=512)
            mix = retention_mix(proj.reshape(B, S, -1), ret_gn[j], B, S)
            w_out = ret_w_out[j]
        elif m == 1:
            npad = -(-gla_w_in.shape[2] // (5 * LANES)) * (5 * LANES)
            proj = norm_proj(x2, norm_mix_pre[i], _pad_cols(gla_w_in[j], npad).astype(BF16), tn=5 * LANES)
            w_gate_pad = jnp.pad(gla_w_gate_up[j], ((0, LANES - GLA_RANK), (0, 0)))
            mix = gla_mix(proj.reshape(B, S, -1), w_gate_pad, gla_b_gate[j], gla_gn[j], B, S)
            w_out = gla_w_out[j]
        elif m == 2:
            proj = norm_proj(x2, norm_mix_pre[i], lru_w_in[j].astype(BF16), tn=512)
            mix = lru_mix(proj.reshape(B, S, -1), lru_conv_w[j], lru_conv_b[j], lru_w_a[j], lru_b_a[j],
                          lru_w_x[j], lru_b_x[j], lru_lambda[j], B, S)
            w_out = lru_w_out[j]
        else:
            npad = -(-nsa_w_in.shape[2] // (5 * LANES)) * (5 * LANES)
            proj = norm_proj(x2, norm_mix_pre[i], _pad_cols(nsa_w_in[j], npad).astype(BF16), tn=5 * LANES)
            proj = proj.reshape(B, S, -1)
            k_cmp, v_cmp = nsa_compress(proj, nsa_pe_k[j], nsa_w1_k[j], nsa_w2_k[j],
                                        nsa_pe_v[j], nsa_w1_v[j], nsa_w2_v[j], B, S)
            mix = nsa_attend(proj, k_cmp, v_cmp, B, S)
            w_out = nsa_w_out[j]
        x2 = proj_norm_res(mix.reshape(T, -1), w_out.astype(BF16), norm_mix_post[i], x2)
        x2 = mlp_block(x2, norm_mlp_pre[i], mlp_w_up[i].astype(BF16), mlp_w_down[i].astype(BF16),
                       norm_mlp_post[i])
    return x2.reshape(B, S, D)
```

```python
import functools
import math

import numpy as np
import jax
import jax.numpy as jnp
from jax import lax
from jax.experimental import pallas as pl
from jax.experimental.pallas import tpu as pltpu

F32 = jnp.float32
BF16 = jnp.bfloat16
HIGHEST = lax.Precision.HIGHEST

D_MODEL = 1024
D_FF = 4 * D_MODEL
NORM_EPS = 1e-6
NEG_INF = -1e30

RET_HEADS, RET_DK, RET_DV = 4, 256, 512
RET_CHUNK = 128

GLA_HEADS, GLA_DK, GLA_DV = 4, 128, 256
GLA_RANK = 16
GLA_TAU = 16.0
GLA_CHUNK = 128

LRU_WIDTH = D_MODEL
LRU_BLOCKS = 8
LRU_BS = LRU_WIDTH // LRU_BLOCKS
CONV_W = 4
LRU_C = 8.0
LRU_TCHUNK = 256

NSA_HEADS, NSA_GROUPS, NSA_HPG = 16, 2, 8
NSA_DK = 64
NSA_DV = 64
CMP_L, CMP_STRIDE = 32, 16
CMP_HID = 2 * NSA_DK
SLC_L, SLC_TOP = 64, 16
WIN = 512
FORCE_SCORE = 1e6
NSA_Q = 128
NSA_TK = 512
NSA_TAIL = 64
NSA_BIG = 1e30
LOG2E = math.log2(math.e)

LANES = 128
SUBLANES = 8
VMEM_LIMIT = 56 * 1024 * 1024


def _cparams(*sem):
    return pltpu.CompilerParams(dimension_semantics=sem, vmem_limit_bytes=VMEM_LIMIT)


def _rms(x, g):
    return x * lax.rsqrt(jnp.mean(x * x, axis=-1, keepdims=True) + NORM_EPS) * g


def _dot(a, b):
    return jnp.dot(a, b, preferred_element_type=F32)


def _dot_nt(a, b):
    return lax.dot_general(a, b, (((1,), (1,)), ((), ())), preferred_element_type=F32)


def _dot_tn(a, b):
    return lax.dot_general(a, b, (((0,), (0,)), ((), ())), preferred_element_type=F32)


def _gelu_tanh(x):
    return 0.5 * x * (1.0 + jnp.tanh(math.sqrt(2.0 / math.pi) * (x + 0.044715 * (x * x * x))))


def _sigmoid(x):
    return 1.0 / (1.0 + jnp.exp(-x))


def _norm_proj_kernel(x_ref, g_ref, w_ref, o_ref, h_ref):
    @pl.when(pl.program_id(1) == 0)
    def _():
        h_ref[...] = _rms(x_ref[...], g_ref[...]).astype(BF16)

    o_ref[...] = _dot(h_ref[...], w_ref[...]).astype(o_ref.dtype)


def _norm_proj_extra_kernel(x_ref, g_ref, w_ref, we_ref, o_ref, oe_ref, h_ref):
    j = pl.program_id(1)

    @pl.when(j == 0)
    def _():
        h_ref[...] = _rms(x_ref[...], g_ref[...]).astype(BF16)

    @pl.when(j < pl.num_programs(1) - 1)
    def _():
        o_ref[...] = _dot(h_ref[...], w_ref[...]).astype(o_ref.dtype)

    @pl.when(j == pl.num_programs(1) - 1)
    def _():
        oe_ref[...] = _dot(h_ref[...], we_ref[...])


def norm_proj(x2d, g, w_bf16, tn, w_extra=None, tm=1024):
    T, D = x2d.shape
    N = w_bf16.shape[1]
    assert T % tm == 0 and N % tn == 0
    nj = N // tn
    if w_extra is None:
        return pl.pallas_call(
            _norm_proj_kernel,
            grid=(T // tm, nj),
            in_specs=[
                pl.BlockSpec((tm, D), lambda i, j: (i, 0)),
                pl.BlockSpec((1, D), lambda i, j: (0, 0)),
                pl.BlockSpec((D, tn), lambda i, j: (0, j)),
            ],
            out_specs=pl.BlockSpec((tm, tn), lambda i, j: (i, j)),
            out_shape=jax.ShapeDtypeStruct((T, N), BF16),
            scratch_shapes=[pltpu.VMEM((tm, D), BF16)],
            compiler_params=_cparams("parallel", "arbitrary"),
            name="norm_proj",
        )(x2d, g.reshape(1, D), w_bf16)
    ne = w_extra.shape[1]
    last = nj - 1
    return pl.pallas_call(
        _norm_proj_extra_kernel,
        grid=(T // tm, nj + 1),
        in_specs=[
            pl.BlockSpec((tm, D), lambda i, j: (i, 0)),
            pl.BlockSpec((1, D), lambda i, j: (0, 0)),
            pl.BlockSpec((D, tn), lambda i, j: (0, jnp.minimum(j, last))),
            pl.BlockSpec((D, ne), lambda i, j: (0, 0)),
        ],
        out_specs=[pl.BlockSpec((tm, tn), lambda i, j: (i, jnp.minimum(j, last))),
                   pl.BlockSpec((tm, ne), lambda i, j: (i, 0))],
        out_shape=[jax.ShapeDtypeStruct((T, N), BF16), jax.ShapeDtypeStruct((T, ne), F32)],
        scratch_shapes=[pltpu.VMEM((tm, D), BF16)],
        compiler_params=_cparams("parallel", "arbitrary"),
        name="norm_proj_extra",
    )(x2d, g.reshape(1, D), w_bf16, w_extra)


def _proj_norm_res_kernel(a_ref, w_ref, g_ref, x_ref, o_ref):
    h = _dot(a_ref[...], w_ref[...])
    o_ref[...] = x_ref[...] + _rms(h, g_ref[...])


def proj_norm_res(a2d, w_bf16, g, x2d, tm=512):
    T, K = a2d.shape
    D = w_bf16.shape[1]
    assert T % tm == 0
    return pl.pallas_call(
        _proj_norm_res_kernel,
        grid=(T // tm,),
        in_specs=[
            pl.BlockSpec((tm, K), lambda i: (i, 0)),
            pl.BlockSpec((K, D), lambda i: (0, 0)),
            pl.BlockSpec((1, D), lambda i: (0, 0)),
            pl.BlockSpec((tm, D), lambda i: (i, 0)),
        ],
        out_specs=pl.BlockSpec((tm, D), lambda i: (i, 0)),
        out_shape=jax.ShapeDtypeStruct((T, D), F32),
        compiler_params=_cparams("parallel"),
        name="proj_norm_res",
    )(a2d, w_bf16, g.reshape(1, D), x2d)


MLP_FCHUNK = 512


def _mlp_kernel(x_ref, g1_ref, wu_ref, wd_ref, g2_ref, o_ref):
    x = x_ref[...]
    h = _rms(x, g1_ref[...]).astype(BF16)
    acc = jnp.zeros(x.shape, F32)
    for c in range(D_FF // MLP_FCHUNK):
        sl = slice(c * MLP_FCHUNK, (c + 1) * MLP_FCHUNK)
        u = jnp.maximum(_dot(h, wu_ref[:, sl]), 0.0)
        acc = acc + _dot((u * u).astype(BF16), wd_ref[sl, :])
    o_ref[...] = x + _rms(acc, g2_ref[...])


def mlp_block(x2d, g1, wu_bf16, wd_bf16, g2, tm=512):
    T, D = x2d.shape
    F = wu_bf16.shape[1]
    return pl.pallas_call(
        _mlp_kernel,
        grid=(T // tm,),
        in_specs=[
            pl.BlockSpec((tm, D), lambda i: (i, 0)),
            pl.BlockSpec((1, D), lambda i: (0, 0)),
            pl.BlockSpec((D, F), lambda i: (0, 0)),
            pl.BlockSpec((F, D), lambda i: (0, 0)),
            pl.BlockSpec((1, D), lambda i: (0, 0)),
        ],
        out_specs=pl.BlockSpec((tm, D), lambda i: (i, 0)),
        out_shape=jax.ShapeDtypeStruct((T, D), F32),
        compiler_params=_cparams("parallel"),
        name="mlp",
    )(x2d, g1.reshape(1, D), wu_bf16, wd_bf16, g2.reshape(1, D))


def _ret_kernel(q_ref, k_ref, v_ref, g_ref, gn_ref, di_ref, dq_ref, dk_ref, dc_ref, o_ref, st_ref):
    @pl.when(pl.program_id(2) == 0)
    def _():
        st_ref[...] = jnp.zeros_like(st_ref)

    qb = q_ref[...]
    kb = k_ref[...] * (RET_DK ** -0.5)
    vb = v_ref[...]
    st = st_ref[...]
    scores = _dot_nt(qb, kb) * di_ref[...]
    o = _dot(scores.astype(BF16), vb) + _dot((qb.astype(F32) * dq_ref[...]).astype(BF16), st.astype(BF16))
    st_ref[...] = st * dc_ref[...] + _dot_tn((kb.astype(F32) * dk_ref[...]).astype(BF16), vb)
    o = o - jnp.mean(o, axis=-1, keepdims=True)
    gate = g_ref[...].astype(F32)
    o_ref[...] = (_rms(o, gn_ref[...]) * (gate * _sigmoid(gate))).astype(o_ref.dtype)


def retention_mix(proj, gn, B, S):
    H, DK, DV, C = RET_HEADS, RET_DK, RET_DV, RET_CHUNK
    log_gamma = jnp.log1p(-jnp.exp2(-5.0 - jnp.arange(H, dtype=F32)))
    pos = jnp.arange(C, dtype=F32)
    diff = pos[:, None] - pos[None, :]
    d_intra = jnp.where(diff >= 0, jnp.exp(log_gamma[:, None, None] * jnp.maximum(diff, 0.0)), 0.0)
    d_q = jnp.exp(log_gamma[:, None] * (pos + 1.0))[:, :, None]
    d_k = jnp.exp(log_gamma[:, None] * (C - 1.0 - pos))[:, :, None]
    d_c = jnp.exp(log_gamma * C)[:, None, None]
    kq, kk = H * DK // DK, 0
    return pl.pallas_call(
        _ret_kernel,
        grid=(B, H, S // C),
        in_specs=[
            pl.BlockSpec((None, C, DK), lambda b, h, c: (b, c, h)),
            pl.BlockSpec((None, C, DK), lambda b, h, c: (b, c, H + h)),
            pl.BlockSpec((None, C, DV), lambda b, h, c: (b, c, (2 * H * DK) // DV + h)),
            pl.BlockSpec((None, C, DV), lambda b, h, c: (b, c, (2 * H * DK) // DV + H + h)),
            pl.BlockSpec((1, DV), lambda b, h, c: (0, h)),
            pl.BlockSpec((None, C, C), lambda b, h, c: (h, 0, 0)),
            pl.BlockSpec((None, C, 1), lambda b, h, c: (h, 0, 0)),
            pl.BlockSpec((None, C, 1), lambda b, h, c: (h, 0, 0)),
            pl.BlockSpec((None, 1, 1), lambda b, h, c: (h, 0, 0)),
        ],
        out_specs=pl.BlockSpec((None, C, DV), lambda b, h, c: (b, c, h)),
        out_shape=jax.ShapeDtypeStruct((B, S, H * DV), BF16),
        scratch_shapes=[pltpu.VMEM((DK, DV), F32)],
        compiler_params=_cparams("parallel", "parallel", "arbitrary"),
        name="retention",
    )(proj, proj, proj, proj, gn.reshape(1, H * DV), d_intra, d_q, d_k, d_c)


def _gla_level_sizes(C):
    sizes, s = [], 2
    while s <= C:
        sizes.append(s)
        s *= 2
    return sizes


def _gla_constants(C):
    t = np.arange(C)[:, None]
    r = np.arange(C)[None, :]
    mats = [(r <= t)]
    mats.append(r > t)
    masks = [np.eye(C, dtype=bool)]
    for bs in _gla_level_sizes(C):
        half = bs // 2
        mid = (t // bs) * bs + half - 1
        second = (t % bs) >= half
        m = np.where(second, (r > mid) & (r <= t), (r > t) & (r <= mid))
        mats.append(m)
        same = (t // bs) == (r // bs)
        masks.append(same & second & ((r % bs) < half))
    return (np.concatenate(mats, 0).astype(np.float32), np.stack(masks, 0).astype(np.float32))


def _gla_kernel(q_ref, k_ref, v_ref, r_ref, gl_ref, wg_ref, bg_ref, gn_ref, mall_ref, mask_ref, o_ref, st_ref):
    C, DK = GLA_CHUNK, GLA_DK

    @pl.when(pl.program_id(2) == 0)
    def _():
        st_ref[...] = jnp.zeros_like(st_ref)

    z = jnp.dot(gl_ref[...], wg_ref[...], preferred_element_type=F32, precision=HIGHEST) + bg_ref[...]
    la = (jnp.minimum(z, 0.0) - jnp.log1p(jnp.exp(-jnp.abs(z)))) * (1.0 / GLA_TAU)
    la_hi = la.astype(BF16)
    la_lo = (la - la_hi.astype(F32)).astype(BF16)
    sums = _dot(mall_ref[...], jnp.concatenate([la_hi, la_lo], axis=1))
    expo = sums[:, :DK] + sums[:, DK:]
    decay = jnp.exp(expo)
    e_b = decay[0:C]
    e_rest = decay[C:2 * C]
    e_last = decay[C - 1:C]

    q = q_ref[...].astype(F32) * (DK ** -0.5)
    k = k_ref[...].astype(F32)
    vb = v_ref[...]
    a = _dot_nt(q.astype(BF16), k_ref[...]) * mask_ref[0]
    for lvl in range(len(_gla_level_sizes(C))):
        e = decay[(2 + lvl) * C:(3 + lvl) * C]
        a = a + _dot_nt((q * e).astype(BF16), (k * e).astype(BF16)) * mask_ref[1 + lvl]
    st = st_ref[...]
    o = _dot(a.astype(BF16), vb) + _dot_nt((q * e_b).astype(BF16), st.astype(BF16))
    st_ref[...] = st * e_last + _dot_tn(vb, (k * e_rest).astype(BF16))
    gate = r_ref[...].astype(F32)
    o_ref[...] = (_rms(o, gn_ref[...]) * (gate * _sigmoid(gate))).astype(o_ref.dtype)


def gla_mix(proj, gl, w_gate_pad, b_gate, gn, B, S):
    H, DK, DV, C = GLA_HEADS, GLA_DK, GLA_DV, GLA_CHUNK
    mall, masks = _gla_constants(C)
    v0 = (2 * H * DK) // DV
    return pl.pallas_call(
        _gla_kernel,
        grid=(B, H, S // C),
        in_specs=[
            pl.BlockSpec((None, C, DK), lambda b, h, c: (b, c, h)),
            pl.BlockSpec((None, C, DK), lambda b, h, c: (b, c, H + h)),
            pl.BlockSpec((None, C, DV), lambda b, h, c: (b, c, v0 + h)),
            pl.BlockSpec((None, C, DV), lambda b, h, c: (b, c, v0 + H + h)),
            pl.BlockSpec((None, C, LANES), lambda b, h, c: (b, c, 0)),
            pl.BlockSpec((LANES, DK), lambda b, h, c: (0, h)),
            pl.BlockSpec((1, DK), lambda b, h, c: (0, h)),
            pl.BlockSpec((1, DV), lambda b, h, c: (0, h)),
            pl.BlockSpec(mall.shape, lambda b, h, c: (0, 0)),
            pl.BlockSpec(masks.shape, lambda b, h, c: (0, 0, 0)),
        ],
        out_specs=pl.BlockSpec((None, C, DV), lambda b, h, c: (b, c, h)),
        out_shape=jax.ShapeDtypeStruct((B, S, H * DV), BF16),
        scratch_shapes=[pltpu.VMEM((DV, DK), F32)],
        compiler_params=_cparams("parallel", "parallel", "arbitrary"),
        name="gla",
    )(proj, proj, proj, proj, gl, w_gate_pad, b_gate.reshape(1, H * DK), gn.reshape(1, H * DV),
      jnp.asarray(mall, BF16), jnp.asarray(masks, F32))


def _lru_kernel(x_ref, y_ref, cw_ref, cb_ref, wa_ref, ba_ref, wx_ref, bx_ref, lam_ref, o_ref,
                xbuf, a_s, u_s, hcar):
    TC, W = LRU_TCHUNK, LRU_WIDTH

    @pl.when(pl.program_id(1) == 0)
    def _():
        xbuf[0:SUBLANES, :] = jnp.zeros((SUBLANES, W), F32)
        hcar[...] = jnp.zeros_like(hcar)

    xbuf[SUBLANES:SUBLANES + TC, :] = x_ref[...].astype(F32)
    xc = cb_ref[...]
    for tap in range(CONV_W):
        off = SUBLANES - (CONV_W - 1) + tap
        xc = xc + xbuf[off:off + TC, :] * cw_ref[tap:tap + 1, :]
    xbuf[0:SUBLANES, :] = xbuf[TC:TC + SUBLANES, :]

    xcb = xc.astype(BF16)
    for n in range(LRU_BLOCKS):
        sl = slice(n * LRU_BS, (n + 1) * LRU_BS)
        a_s[:, sl] = _dot(xcb[:, sl], wa_ref[n])
        u_s[:, sl] = _dot(xcb[:, sl], wx_ref[n])
    gate_r = _sigmoid(a_s[...] + ba_ref[...])
    gate_i = _sigmoid(u_s[...] + bx_ref[...])
    nl = -lam_ref[...]
    softplus = jnp.maximum(nl, 0.0) + jnp.log1p(jnp.exp(-jnp.abs(nl)))
    a = jnp.exp(-LRU_C * gate_r * softplus)
    a_s[...] = a
    u_s[...] = jnp.sqrt(1.0 - a * a) * gate_i * xc

    row = lax.broadcasted_iota(jnp.int32, (SUBLANES, W), 0)

    def body(r, h):
        sl = pl.ds(pl.multiple_of(r * SUBLANES, SUBLANES), SUBLANES)
        a8 = a_s[sl, :]
        u8 = u_s[sl, :]
        d = 1
        while d < SUBLANES:
            keep = row >= d
            a_sh = jnp.where(keep, pltpu.roll(a8, d, 0), 1.0)
            u_sh = jnp.where(keep, pltpu.roll(u8, d, 0), 0.0)
            u8 = a8 * u_sh + u8
            a8 = a8 * a_sh
            d *= 2
        h8 = a8 * h + u8
        u_s[sl, :] = h8
        return jnp.broadcast_to(h8[SUBLANES - 1:SUBLANES, :], (SUBLANES, W))

    hcar[...] = lax.fori_loop(0, TC // SUBLANES, body, hcar[...])
    o_ref[...] = (u_s[...] * _gelu_tanh(y_ref[...].astype(F32))).astype(o_ref.dtype)


def lru_mix(proj, conv_w, conv_b, w_a, b_a, w_x, b_x, lam, B, S):
    TC, W = LRU_TCHUNK, LRU_WIDTH
    row = lambda a: a.reshape(1, W)
    vec = pl.BlockSpec((1, W), lambda b, t: (0, 0))
    blk = pl.BlockSpec((LRU_BLOCKS, LRU_BS, LRU_BS), lambda b, t: (0, 0, 0))
    return pl.pallas_call(
        _lru_kernel,
        grid=(B, S // TC),
        in_specs=[
            pl.BlockSpec((None, TC, W), lambda b, t: (b, t, 0)),
            pl.BlockSpec((None, TC, W), lambda b, t: (b, t, 1)),
            pl.BlockSpec((CONV_W, W), lambda b, t: (0, 0)),
            vec, blk, vec, blk, vec, vec,
        ],
        out_specs=pl.BlockSpec((None, TC, W), lambda b, t: (b, t, 0)),
        out_shape=jax.ShapeDtypeStruct((B, S, W), BF16),
        scratch_shapes=[
            pltpu.VMEM((TC + SUBLANES, W), F32),
            pltpu.VMEM((TC, W), F32),
            pltpu.VMEM((TC, W), F32),
            pltpu.VMEM((SUBLANES, W), F32),
        ],
        compiler_params=_cparams("parallel", "arbitrary"),
        name="rglru",
    )(proj, proj, conv_w, row(conv_b), w_a.astype(BF16), row(b_a), w_x.astype(BF16), row(b_x), row(lam))


def _nsa_cmp_kernel(kc_ref, vc_ref, pek_ref, w1k_ref, w2k_ref, pev_ref, w1v_ref, w2v_ref, ko_ref, vo_ref,
                    buf, *, nblk):
    def compress(t_ref, pe_ref, w1_ref, w2_ref, o_ref, d):
        buf[...] = t_ref[...].astype(F32)
        ya = [jnp.zeros((nblk, CMP_HID), F32) for _ in range(NSA_GROUPS)]
        yb = [jnp.zeros((nblk, CMP_HID), F32) for _ in range(NSA_GROUPS)]
        for l in range(CMP_STRIDE):
            both = buf[pl.ds(l, nblk, stride=CMP_STRIDE), :]
            for g in range(NSA_GROUPS):
                rows = both[:, g * d:(g + 1) * d]
                lo = (rows + pe_ref[l:l + 1, :]).astype(BF16)
                hi = (rows + pe_ref[CMP_STRIDE + l:CMP_STRIDE + l + 1, :]).astype(BF16)
                ya[g] = ya[g] + _dot(lo, w1_ref[l * d:(l + 1) * d, :])
                yb[g] = yb[g] + _dot(hi, w1_ref[(CMP_STRIDE + l) * d:(CMP_STRIDE + l + 1) * d, :])
        for g in range(NSA_GROUPS):
            shifted = jnp.concatenate([yb[g][1:], jnp.zeros((1, CMP_HID), F32)], axis=0)
            hid = _gelu_tanh(ya[g] + shifted).astype(BF16)
            o_ref[g] = _dot(hid, w2_ref[...])

    compress(kc_ref, pek_ref, w1k_ref, w2k_ref, ko_ref, NSA_DK)
    compress(vc_ref, pev_ref, w1v_ref, w2v_ref, vo_ref, NSA_DV)


def nsa_compress(proj, pe_k, w1_k, w2_k, pe_v, w1_v, w2_v, B, S):
    nblk = S // CMP_STRIDE
    G = NSA_GROUPS
    kc_blk = (NSA_HEADS * NSA_DK) // LANES
    full2 = lambda a: pl.BlockSpec(a.shape, lambda b: (0, 0))
    w1k, w2k, w1v, w2v = (a.astype(BF16) for a in (w1_k, w2_k, w1_v, w2_v))
    out_sd = jax.ShapeDtypeStruct((B, G, nblk, NSA_DK), F32)
    return pl.pallas_call(
        functools.partial(_nsa_cmp_kernel, nblk=nblk),
        grid=(B,),
        in_specs=[
            pl.BlockSpec((None, S, LANES), lambda b: (b, 0, kc_blk)),
            pl.BlockSpec((None, S, LANES), lambda b: (b, 0, kc_blk + 1)),
            full2(pe_k), full2(w1k), full2(w2k), full2(pe_v), full2(w1v), full2(w2v),
        ],
        out_specs=[pl.BlockSpec((None, G, nblk, NSA_DK), lambda b: (b, 0, 0, 0))] * 2,
        out_shape=[out_sd, out_sd],
        scratch_shapes=[pltpu.VMEM((S, LANES), F32)],
        compiler_params=_cparams("parallel"),
        name="nsa_compress",
    )(proj, proj, pe_k, w1k, w2k, pe_v, w1v, w2v)


def _alibi_slope(head):
    return float(2.0 ** (-8.0 * (head + 1.0) / NSA_HEADS))


def _nsa_kernel(q_ref, gate_ref, kc_ref, vc_ref, ks_ref, vs_ref, kw_ref, vw_ref,
                ovt_ref, qt_ref, kt_ref, ct_ref, vt_ref, kind_ref, o_ref,
                qa_s, kca_s, vct_s, ksa_s, vst_s, kwa_s, vwt_s, m_s, acc_s, *, seq):
    Q, TK, HPG, DK, DV, G = NSA_Q, NSA_TK, NSA_HPG, NSA_DK, NSA_DV, NSA_GROUPS
    R = HPG * Q
    W = WIN + Q
    ncmp = seq // CMP_STRIDE
    nslc = seq // SLC_L
    scale = DK ** -0.5
    qi = pl.program_id(1)
    q0 = qi * Q
    t_row = q0 + lax.broadcasted_iota(jnp.int32, (1, Q), 1)
    gates_t = _sigmoid(gate_ref[...].astype(F32)).T
    per_head = lambda a: jnp.concatenate([a] * HPG, axis=1)

    @pl.when(qi == 0)
    def _():
        for g in range(G):
            ks = slice(g * DK, (g + 1) * DK)
            vs = slice(g * DV, (g + 1) * DV)
            ksa_s[g, :, 0:LANES] = jnp.concatenate([ks_ref[:, ks], kt_ref[...]], axis=1)
            ksa_s[g, :, LANES:2 * LANES] = kind_ref[...]
            kwa_s[g] = jnp.concatenate([kw_ref[:, ks], kt_ref[...]], axis=1)
            vst_s[g] = jnp.concatenate([vs_ref[:, vs], vt_ref[...]], axis=1).astype(F32).T.astype(BF16)
            vwt_s[g] = jnp.concatenate([vw_ref[:, vs], vt_ref[...]], axis=1).astype(F32).T.astype(BF16)
            kca_s[g] = jnp.concatenate([kc_ref[g].astype(BF16), ct_ref[...]], axis=1)
            vct_s[g] = jnp.concatenate([vc_ref[g], jnp.zeros((ncmp, NSA_TAIL), F32)], axis=1).T.astype(BF16)

    o_cmp_t = []
    for g in range(G):
        for h in range(HPG):
            col = g * HPG + h
            qh = q_ref[:, col * DK:(col + 1) * DK].astype(F32) * (scale * LOG2E)
            tail = jnp.broadcast_to(qt_ref[col:col + 1, :], (Q, NSA_TAIL))
            qa_s[g, h * Q:(h + 1) * Q, 0:LANES] = jnp.concatenate([qh, tail], axis=1).astype(BF16)

        cend = CMP_STRIDE * lax.broadcasted_iota(jnp.int32, (ncmp, 1), 0) + (CMP_L - 1)
        bias_c = jnp.where(cend <= t_row, 0.0, -NSA_BIG)
        s = _dot_nt(kca_s[g], qa_s[g, :, 0:LANES]) + per_head(bias_c)
        e = jnp.exp2(s - jnp.max(s, axis=0, keepdims=True))
        inv = per_head(jnp.where(t_row >= CMP_L - 1, 1.0, 0.0)) / jnp.sum(e, axis=0, keepdims=True)
        p = e * inv
        psum_t = p[:, 0:Q]
        for h in range(1, HPG):
            psum_t = psum_t + p[:, h * Q:(h + 1) * Q]
        o_cmp_t.append(_dot(vct_s[g], p.astype(BF16)))

        imp_t = jnp.dot(ovt_ref[...], psum_t, preferred_element_type=F32, precision=HIGHEST)
        jcol = lax.broadcasted_iota(jnp.int32, (nslc, 1), 0)
        cur = t_row // SLC_L
        forced = (jcol == 0) | (jcol == cur) | (jcol == cur - 1)
        imp_t = jnp.where(forced, FORCE_SCORE, imp_t)
        imp_t = jnp.where(jcol <= cur, imp_t, -1.0)
        nrow = nslc // SUBLANES
        rows = [imp_t[r * SUBLANES:(r + 1) * SUBLANES] for r in range(nrow)]
        ranks = [jnp.zeros((SUBLANES, Q), F32) for _ in range(nrow)]
        jloc = lax.broadcasted_iota(jnp.int32, (SUBLANES, Q), 0)
        for j2 in range(nslc):
            other = jnp.broadcast_to(imp_t[j2:j2 + 1, :], (SUBLANES, Q))
            for r in range(nrow):
                if (r + 1) * SUBLANES - 1 <= j2:
                    ahead = jnp.where(other > rows[r], 1.0, 0.0)
                elif r * SUBLANES > j2:
                    ahead = jnp.where(other >= rows[r], 1.0, 0.0)
                else:
                    ahead = jnp.where(jloc + r * SUBLANES > j2, jnp.where(other >= rows[r], 1.0, 0.0),
                                      jnp.where(other > rows[r], 1.0, 0.0))
                ranks[r] = ranks[r] + ahead
        ntop = float(min(SLC_TOP, nslc))
        bias_t = jnp.concatenate([jnp.where(rk < ntop, 0.0, -NSA_BIG) for rk in ranks]
                                 + [jnp.zeros((LANES - nslc, Q), F32)], axis=0)
        sel_bias = bias_t.T.astype(BF16)
        for h in range(HPG):
            qa_s[g, h * Q:(h + 1) * Q, LANES:2 * LANES] = sel_bias

    m_s[...] = jnp.full(m_s.shape, -NSA_BIG, F32)
    acc_s[...] = jnp.zeros(acc_s.shape, F32)

    def sel_tile(kt, causal):
        k0 = pl.multiple_of(kt * TK, TK)
        for g in range(G):
            s = _dot_nt(ksa_s[g, pl.ds(k0, TK), :], qa_s[g])
            if causal:
                pos = k0 + lax.broadcasted_iota(jnp.int32, (TK, 1), 0)
                s = s + per_head(jnp.where(pos <= t_row, 0.0, -NSA_BIG))
            m_old = m_s[g]
            m_new = jnp.maximum(m_old, jnp.max(s, axis=0, keepdims=True))
            p = jnp.exp2(s - m_new).astype(BF16)
            acc_s[g] = jnp.exp2(m_old - m_new) * acc_s[g] + _dot(vst_s[g, :, pl.ds(k0, TK)], p)
            m_s[g] = m_new

    last = q0 // TK

    def full_tile(kt, carry):
        sel_tile(kt, False)
        return carry

    lax.fori_loop(0, last, full_tile, 0)
    sel_tile(last, True)

    w0 = pl.multiple_of(jnp.maximum(q0 - WIN, 0), Q)
    wpos = w0 + lax.broadcasted_iota(jnp.int32, (W, 1), 0)
    dist_w = t_row - wpos
    bias_w = per_head(jnp.where((dist_w >= 0) & (dist_w < WIN), 0.0, -NSA_BIG))
    for g in range(G):
        s = _dot_nt(kwa_s[g, pl.ds(w0, W), :], qa_s[g, :, 0:LANES]) + bias_w
        e = jnp.exp2(s - jnp.max(s, axis=0, keepdims=True))
        pv_w = _dot(vwt_s[g, :, pl.ds(w0, W)], e.astype(BF16))
        acc = acc_s[g]
        o_sel_t = acc[0:DV] / acc[DV:DV + 1]
        o_win_t = pv_w[0:DV] / pv_w[DV:DV + 1]
        outs = []
        for h in range(HPG):
            col = g * HPG + h
            cs = slice(h * Q, (h + 1) * Q)
            outs.append(gates_t[col:col + 1] * o_cmp_t[g][0:DV, cs]
                        + gates_t[NSA_HEADS + col:NSA_HEADS + col + 1] * o_sel_t[:, cs]
                        + gates_t[2 * NSA_HEADS + col:2 * NSA_HEADS + col + 1] * o_win_t[:, cs])
        for h in range(0, HPG, 2):
            col = g * HPG + h
            o_ref[:, col * DV:(col + 2) * DV] = jnp.concatenate([outs[h], outs[h + 1]], axis=0).T.astype(o_ref.dtype)


def _bf16_split3(x):
    x = np.asarray(x, np.float32)
    rnd = lambda a: a.astype(BF16).astype(np.float32)
    hi = rnd(x)
    mid = rnd(x - hi)
    lo = rnd(x - hi - mid)
    return hi, mid, lo


def _pos_tail(pos):
    pos = np.asarray(pos)
    hi, lo = (pos // SLC_L) * SLC_L, pos % SLC_L
    tail = np.zeros((pos.shape[0], NSA_TAIL), np.float32)
    for c in range(3):
        tail[:, 2 * c] = hi
        tail[:, 2 * c + 1] = lo
    return tail


def nsa_attend(proj, k_cmp, v_cmp, B, S):
    Q, G, HPG = NSA_Q, NSA_GROUPS, NSA_HPG
    ncmp, nslc = S // CMP_STRIDE, S // SLC_L
    assert S >= WIN + Q and S % NSA_TK == 0 and nslc <= LANES and nslc % SUBLANES == 0
    n = np.arange(ncmp)[:, None]
    j = np.arange(nslc)[None, :]
    overlap = ((n * CMP_STRIDE < (j + 1) * SLC_L) & (n * CMP_STRIDE + CMP_L > j * SLC_L)
               & (n < (S - CMP_L) // CMP_STRIDE + 1)).astype(np.float32)
    slopes = np.asarray([_alibi_slope(h) for h in range(NSA_HEADS)], np.float32) * np.float32(LOG2E)
    qtail = np.zeros((NSA_HEADS, NSA_TAIL), np.float32)
    for c, part in enumerate(_bf16_split3(slopes)):
        qtail[:, 2 * c] = part
        qtail[:, 2 * c + 1] = part
    ktail = _pos_tail(np.arange(S))
    ctail = _pos_tail(np.arange(ncmp) * CMP_STRIDE + CMP_L - 1)
    vtail = np.zeros((S, NSA_TAIL), np.float32)
    vtail[:, 0] = 1.0
    kind = (np.arange(S)[:, None] // SLC_L == np.arange(LANES)[None, :]).astype(np.float32)
    qw = NSA_HEADS * NSA_DK
    kv0 = qw // LANES
    seqblk = lambda c: pl.BlockSpec((None, S, LANES), lambda b, i: (b, 0, c))
    cmpblk = pl.BlockSpec((None, G, ncmp, NSA_DK), lambda b, i: (b, 0, 0, 0))
    const = lambda a: pl.BlockSpec(a.shape, lambda b, i: (0, 0))
    consts = [jnp.asarray(overlap.T, F32), jnp.asarray(qtail, F32), jnp.asarray(ktail, BF16),
              jnp.asarray(ctail, BF16), jnp.asarray(vtail, BF16), jnp.asarray(kind, BF16)]
    return pl.pallas_call(
        functools.partial(_nsa_kernel, seq=S),
        grid=(B, S // Q),
        in_specs=[
            pl.BlockSpec((None, Q, qw), lambda b, i: (b, i, 0)),
            pl.BlockSpec((None, Q, LANES), lambda b, i: (b, i, kv0 + 6)),
            cmpblk, cmpblk,
            seqblk(kv0 + 2), seqblk(kv0 + 3), seqblk(kv0 + 4), seqblk(kv0 + 5),
        ] + [const(a) for a in consts],
        out_specs=pl.BlockSpec((None, Q, NSA_HEADS * NSA_DV), lambda b, i: (b, i, 0)),
        out_shape=jax.ShapeDtypeStruct((B, S, NSA_HEADS * NSA_DV), BF16),
        scratch_shapes=[
            pltpu.VMEM((G, HPG * Q, 2 * LANES), BF16),
            pltpu.VMEM((G, ncmp, LANES), BF16),
            pltpu.VMEM((G, LANES, ncmp), BF16),
            pltpu.VMEM((G, S, 2 * LANES), BF16),
            pltpu.VMEM((G, LANES, S), BF16),
            pltpu.VMEM((G, S, LANES), BF16),
            pltpu.VMEM((G, LANES, S), BF16),
            pltpu.VMEM((G, 1, HPG * Q), F32),
            pltpu.VMEM((G, LANES, HPG * Q), F32),
        ],
        compiler_params=_cparams("parallel", "arbitrary"),
        name="nsa_attend",
    )(proj, proj, k_cmp, v_cmp, proj, proj, proj, proj, *consts)


def _pad_cols(w, n):
    return jnp.pad(w, ((0, 0), (0, n - w.shape[1])))


def kernel(x, norm_mix_pre, norm_mix_post, norm_mlp_pre, norm_mlp_post, mlp_w_up, mlp_w_down, ret_w_in, ret_gn, ret_w_out, gla_w_in, gla_w_gate_up, gla_b_gate, gla_gn, gla_w_out, lru_w_in, lru_conv_w, lru_conv_b, lru_w_a, lru_b_a, lru_w_x, lru_b_x, lru_lambda, lru_w_out, nsa_w_in, nsa_pe_k, nsa_w1_k, nsa_w2_k, nsa_pe_v, nsa_w1_v, nsa_w2_v, nsa_w_out):
    B, S, D = x.shape
    T = B * S
    depth = norm_mix_pre.shape[0]
    x2 = x.reshape(T, D)
    for i in range(depth):
        m, j = i % 4, i // 4
        if m == 0:
            proj = norm_proj(x2, norm_mix_pre[i], ret_w_in[j].astype(BF16), tn=512)
            mix = retention_mix(proj.reshape(B, S, -1), ret_gn[j], B, S)
            w_out = ret_w_out[j]
        elif m == 1:
            nmain = gla_w_in.shape[2] - GLA_RANK
            w_low = _pad_cols(gla_w_in[j][:, nmain:], LANES).astype(BF16)
            proj, gl = norm_proj(x2, norm_mix_pre[i], gla_w_in[j][:, :nmain].astype(BF16), tn=512, w_extra=w_low)
            w_gate_pad = jnp.pad(gla_w_gate_up[j], ((0, LANES - GLA_RANK), (0, 0)))
            mix = gla_mix(proj.reshape(B, S, -1), gl.reshape(B, S, -1), w_gate_pad, gla_b_gate[j], gla_gn[j], B, S)
            w_out = gla_w_out[j]
        elif m == 2:
            proj = norm_proj(x2, norm_mix_pre[i], lru_w_in[j].astype(BF16), tn=512)
            mix = lru_mix(proj.reshape(B, S, -1), lru_conv_w[j], lru_conv_b[j], lru_w_a[j], lru_b_a[j],
                          lru_w_x[j], lru_b_x[j], lru_lambda[j], B, S)
            w_out = lru_w_out[j]
        else:
            npad = -(-nsa_w_in.shape[2] // (5 * LANES)) * (5 * LANES)
            proj = norm_proj(x2, norm_mix_pre[i], _pad_cols(nsa_w_in[j], npad).astype(BF16), tn=5 * LANES)
            proj = proj.reshape(B, S, -1)
            k_cmp, v_cmp = nsa_compress(proj, nsa_pe_k[j], nsa_w1_k[j], nsa_w2_k[j],
                                        nsa_pe_v[j], nsa_w1_v[j], nsa_w2_v[j], B, S)
            mix = nsa_attend(proj, k_cmp, v_cmp, B, S)
            w_out = nsa_w_out[j]
        x2 = proj_norm_res(mix.reshape(T, -1), w_out.astype(BF16), norm_mix_post[i], x2)
        x2 = mlp_block(x2, norm_mlp_pre[i], mlp_w_up[i].astype(BF16), mlp_w_down[i].astype(BF16),
                       norm_mlp_post[i])
    return x2.reshape(B, S, D)
```

```python
import functools
import math

import numpy as np
import jax
import jax.numpy as jnp
from jax import lax
from jax.experimental import pallas as pl
from jax.experimental.pallas import tpu as pltpu

F32 = jnp.float32
BF16 = jnp.bfloat16
HIGHEST = lax.Precision.HIGHEST

D_MODEL = 1024
D_FF = 4 * D_MODEL
NORM_EPS = 1e-6
NEG_INF = -1e30

RET_HEADS, RET_DK, RET_DV = 4, 256, 512
RET_CHUNK = 256

GLA_HEADS, GLA_DK, GLA_DV = 4, 128, 256
GLA_RANK = 16
GLA_TAU = 16.0
GLA_CHUNK = 128

LRU_WIDTH = D_MODEL
LRU_BLOCKS = 8
LRU_BS = LRU_WIDTH // LRU_BLOCKS
CONV_W = 4
LRU_C = 8.0
LRU_TCHUNK = 256

NSA_HEADS, NSA_GROUPS, NSA_HPG = 16, 2, 8
NSA_DK = 64
NSA_DV = 64
CMP_L, CMP_STRIDE = 32, 16
CMP_HID = 2 * NSA_DK
SLC_L, SLC_TOP = 64, 16
WIN = 512
FORCE_SCORE = 1e6
NSA_Q = 256
NSA_TK = 512
NSA_RB = 256
NSA_TAIL = 64
NSA_VROWS = NSA_DV + 16
NSA_BIG = 1e30
LOG2E = math.log2(math.e)

LANES = 128
SUBLANES = 8
VMEM_LIMIT = 56 * 1024 * 1024


def _cparams(*sem):
    return pltpu.CompilerParams(dimension_semantics=sem, vmem_limit_bytes=VMEM_LIMIT)


def _rms(x, g):
    return x * lax.rsqrt(jnp.mean(x * x, axis=-1, keepdims=True) + NORM_EPS) * g


def _dot(a, b):
    return jnp.dot(a, b, preferred_element_type=F32)


def _dot_nt(a, b):
    return lax.dot_general(a, b, (((1,), (1,)), ((), ())), preferred_element_type=F32)


def _dot_tn(a, b):
    return lax.dot_general(a, b, (((0,), (0,)), ((), ())), preferred_element_type=F32)


def _gelu_tanh(x):
    return 0.5 * x * (1.0 + jnp.tanh(math.sqrt(2.0 / math.pi) * (x + 0.044715 * (x * x * x))))


def _sigmoid(x):
    return 1.0 / (1.0 + jnp.exp(-x))


PROJ_TN = 1024


def _norm_proj_kernel(x_ref, g_ref, w_ref, o_ref, h_ref):
    @pl.when(pl.program_id(1) == 0)
    def _():
        h_ref[...] = _rms(x_ref[...], g_ref[...]).astype(BF16)

    o_ref[...] = _dot(h_ref[...], w_ref[...]).astype(o_ref.dtype)


def _norm_proj_extra_kernel(x_ref, g_ref, w_ref, we_ref, o_ref, oe_ref, h_ref):
    j = pl.program_id(1)

    @pl.when(j == 0)
    def _():
        h_ref[...] = _rms(x_ref[...], g_ref[...]).astype(BF16)

    @pl.when(j < pl.num_programs(1) - 1)
    def _():
        o_ref[...] = _dot(h_ref[...], w_ref[...]).astype(o_ref.dtype)

    @pl.when(j == pl.num_programs(1) - 1)
    def _():
        oe_ref[...] = _dot(h_ref[...], we_ref[...])


def norm_proj(x2d, g, w_bf16, tn, w_extra=None, tm=1024):
    T, D = x2d.shape
    N = w_bf16.shape[1]
    assert T % tm == 0 and N % tn == 0
    nj = N // tn
    if w_extra is None:
        return pl.pallas_call(
            _norm_proj_kernel,
            grid=(T // tm, nj),
            in_specs=[
                pl.BlockSpec((tm, D), lambda i, j: (i, 0)),
                pl.BlockSpec((1, D), lambda i, j: (0, 0)),
                pl.BlockSpec((D, tn), lambda i, j: (0, j)),
            ],
            out_specs=pl.BlockSpec((tm, tn), lambda i, j: (i, j)),
            out_shape=jax.ShapeDtypeStruct((T, N), BF16),
            scratch_shapes=[pltpu.VMEM((tm, D), BF16)],
            compiler_params=_cparams("parallel", "arbitrary"),
            name="norm_proj",
        )(x2d, g.reshape(1, D), w_bf16)
    ne = w_extra.shape[1]
    last = nj - 1
    return pl.pallas_call(
        _norm_proj_extra_kernel,
        grid=(T // tm, nj + 1),
        in_specs=[
            pl.BlockSpec((tm, D), lambda i, j: (i, 0)),
            pl.BlockSpec((1, D), lambda i, j: (0, 0)),
            pl.BlockSpec((D, tn), lambda i, j: (0, jnp.minimum(j, last))),
            pl.BlockSpec((D, ne), lambda i, j: (0, 0)),
        ],
        out_specs=[pl.BlockSpec((tm, tn), lambda i, j: (i, jnp.minimum(j, last))),
                   pl.BlockSpec((tm, ne), lambda i, j: (i, 0))],
        out_shape=[jax.ShapeDtypeStruct((T, N), BF16), jax.ShapeDtypeStruct((T, ne), F32)],
        scratch_shapes=[pltpu.VMEM((tm, D), BF16)],
        compiler_params=_cparams("parallel", "arbitrary"),
        name="norm_proj_extra",
    )(x2d, g.reshape(1, D), w_bf16, w_extra)


MLP_FCHUNK = 512


def _tail_kernel(a_ref, wo_ref, gp_ref, x_ref, g1_ref, wu_ref, wd_ref, g2_ref, o_ref):
    x = x_ref[...] + _rms(_dot(a_ref[...], wo_ref[...]), gp_ref[...])
    h = _rms(x, g1_ref[...]).astype(BF16)
    acc = jnp.zeros(x.shape, F32)
    for c in range(D_FF // MLP_FCHUNK):
        sl = slice(c * MLP_FCHUNK, (c + 1) * MLP_FCHUNK)
        u = jnp.maximum(_dot(h, wu_ref[:, sl]), 0.0)
        acc = acc + _dot((u * u).astype(BF16), wd_ref[sl, :])
    o_ref[...] = x + _rms(acc, g2_ref[...])


def sublayer_tail(a2d, wo_bf16, g_post, x2d, g1, wu_bf16, wd_bf16, g2, tm=512):
    T, D = x2d.shape
    K = a2d.shape[1]
    F = wu_bf16.shape[1]
    resident = lambda shape: pl.BlockSpec(shape, lambda i: (0, 0), pipeline_mode=pl.Buffered(1))
    return pl.pallas_call(
        _tail_kernel,
        grid=(T // tm,),
        in_specs=[
            pl.BlockSpec((tm, K), lambda i: (i, 0)),
            resident((K, D)),
            resident((1, D)),
            pl.BlockSpec((tm, D), lambda i: (i, 0)),
            resident((1, D)),
            resident((D, F)),
            resident((F, D)),
            resident((1, D)),
        ],
        out_specs=pl.BlockSpec((tm, D), lambda i: (i, 0)),
        out_shape=jax.ShapeDtypeStruct((T, D), F32),
        compiler_params=_cparams("parallel"),
        name="sublayer_tail",
    )(a2d, wo_bf16, g_post.reshape(1, D), x2d, g1.reshape(1, D), wu_bf16, wd_bf16, g2.reshape(1, D))


def _ret_kernel(q_ref, k_ref, v_ref, g_ref, gn_ref, di_ref, dq_ref, dk_ref, dc_ref, o_ref, st_ref):
    H, DK, DV = RET_HEADS, RET_DK, RET_DV

    @pl.when(pl.program_id(1) == 0)
    def _():
        st_ref[...] = jnp.zeros_like(st_ref)

    qb = [q_ref[:, h * DK:(h + 1) * DK] for h in range(H)]
    kb = [k_ref[:, h * DK:(h + 1) * DK] * (DK ** -0.5) for h in range(H)]
    vb = [v_ref[:, h * DV:(h + 1) * DV] for h in range(H)]
    st = [st_ref[h] for h in range(H)]
    scores = [_dot_nt(qb[h], kb[h]) for h in range(H)]
    inter = [_dot((qb[h].astype(F32) * dq_ref[h]).astype(BF16), st[h].astype(BF16)) for h in range(H)]
    for h in range(H):
        st_ref[h] = st[h] * dc_ref[h] + _dot_tn((kb[h].astype(F32) * dk_ref[h]).astype(BF16), vb[h])
    for h in range(H):
        o = _dot((scores[h] * di_ref[h]).astype(BF16), vb[h]) + inter[h]
        o = o - jnp.mean(o, axis=-1, keepdims=True)
        hs = slice(h * DV, (h + 1) * DV)
        gate = g_ref[:, hs].astype(F32)
        o_ref[:, hs] = (_rms(o, gn_ref[:, hs]) * (gate * _sigmoid(gate))).astype(o_ref.dtype)


def retention_mix(proj, gn, B, S):
    H, DK, DV, C = RET_HEADS, RET_DK, RET_DV, RET_CHUNK
    assert (DK ** -0.5) == 2.0 ** round(math.log2(DK ** -0.5))
    log_gamma = jnp.log1p(-jnp.exp2(-5.0 - jnp.arange(H, dtype=F32)))
    pos = jnp.arange(C, dtype=F32)
    diff = pos[:, None] - pos[None, :]
    d_intra = jnp.where(diff >= 0, jnp.exp(log_gamma[:, None, None] * jnp.maximum(diff, 0.0)), 0.0)
    d_q = jnp.exp(log_gamma[:, None] * (pos + 1.0))[:, :, None]
    d_k = jnp.exp(log_gamma[:, None] * (C - 1.0 - pos))[:, :, None]
    d_c = jnp.exp(log_gamma * C)[:, None, None]
    qk_w, v_w = H * DK, H * DV
    whole = lambda a: pl.BlockSpec(a.shape, lambda b, c: (0,) * a.ndim)
    return pl.pallas_call(
        _ret_kernel,
        grid=(B, S // C),
        in_specs=[
            pl.BlockSpec((None, C, qk_w), lambda b, c: (b, c, 0)),
            pl.BlockSpec((None, C, qk_w), lambda b, c: (b, c, 1)),
            pl.BlockSpec((None, C, v_w), lambda b, c: (b, c, (2 * qk_w) // v_w)),
            pl.BlockSpec((None, C, v_w), lambda b, c: (b, c, (2 * qk_w) // v_w + 1)),
            pl.BlockSpec((1, v_w), lambda b, c: (0, 0)),
            whole(d_intra), whole(d_q), whole(d_k), whole(d_c),
        ],
        out_specs=pl.BlockSpec((None, C, v_w), lambda b, c: (b, c, 0)),
        out_shape=jax.ShapeDtypeStruct((B, S, v_w), BF16),
        scratch_shapes=[pltpu.VMEM((H, DK, DV), F32)],
        compiler_params=_cparams("parallel", "arbitrary"),
        name="retention",
    )(proj, proj, proj, proj, gn.reshape(1, v_w), d_intra, d_q, d_k, d_c)


def _gla_level_sizes(C):
    sizes, s = [], 2
    while s <= C:
        sizes.append(s)
        s *= 2
    return sizes


def _gla_constants(C):
    t = np.arange(C)[:, None]
    r = np.arange(C)[None, :]
    mats = [(r <= t)]
    mats.append(r > t)
    masks = [np.eye(C, dtype=bool)]
    for bs in _gla_level_sizes(C):
        half = bs // 2
        mid = (t // bs) * bs + half - 1
        second = (t % bs) >= half
        m = np.where(second, (r > mid) & (r <= t), (r > t) & (r <= mid))
        mats.append(m)
        same = (t // bs) == (r // bs)
        masks.append(same & second & ((r % bs) < half))
    return (np.concatenate(mats, 0).astype(np.float32), np.stack(masks, 0).astype(np.float32))


def _gla_kernel(q_ref, k_ref, v_ref, r_ref, gl_ref, wg_ref, bg_ref, gn_ref, mall_ref, mask_ref, o_ref, st_ref):
    C, DK = GLA_CHUNK, GLA_DK

    H, DV = GLA_HEADS, GLA_DV
    W = H * DK

    @pl.when(pl.program_id(1) == 0)
    def _():
        st_ref[...] = jnp.zeros_like(st_ref)

    z = jnp.dot(gl_ref[...], wg_ref[...], preferred_element_type=F32, precision=HIGHEST) + bg_ref[...]
    la = (jnp.minimum(z, 0.0) - jnp.log1p(jnp.exp(-jnp.abs(z)))) * (1.0 / GLA_TAU)
    la_hi = la.astype(BF16)
    la_lo = (la - la_hi.astype(F32)).astype(BF16)
    sums = _dot(mall_ref[...], jnp.concatenate([la_hi, la_lo], axis=1))
    decay = jnp.exp(sums[:, :W] + sums[:, W:])
    e_b = decay[0:C]
    e_rest = decay[C:2 * C]
    e_last = decay[C - 1:C]

    q = q_ref[...].astype(F32) * (DK ** -0.5)
    k = k_ref[...].astype(F32)
    nlev = len(_gla_level_sizes(C))
    qe = [q.astype(BF16)] + [(q * decay[(2 + l) * C:(3 + l) * C]).astype(BF16) for l in range(nlev)]
    ke = [k_ref[...]] + [(k * decay[(2 + l) * C:(3 + l) * C]).astype(BF16) for l in range(nlev)]
    q_in = (q * e_b).astype(BF16)
    k_out = (k * e_rest).astype(BF16)
    for h in range(H):
        ks = slice(h * DK, (h + 1) * DK)
        vs = slice(h * DV, (h + 1) * DV)
        vb = v_ref[:, vs]
        a = _dot_nt(qe[0][:, ks], ke[0][:, ks]) * mask_ref[0]
        for l in range(nlev):
            a = a + _dot_nt(qe[1 + l][:, ks], ke[1 + l][:, ks]) * mask_ref[1 + l]
        st = st_ref[h]
        o = _dot(a.astype(BF16), vb) + _dot_nt(q_in[:, ks], st.astype(BF16))
        st_ref[h] = st * e_last[:, ks] + _dot_tn(vb, k_out[:, ks])
        gate = r_ref[:, vs].astype(F32)
        o_ref[:, vs] = (_rms(o, gn_ref[:, vs]) * (gate * _sigmoid(gate))).astype(o_ref.dtype)


def gla_mix(proj, gl, w_gate_pad, b_gate, gn, B, S):
    H, DK, DV, C = GLA_HEADS, GLA_DK, GLA_DV, GLA_CHUNK
    mall, masks = _gla_constants(C)
    qk_w, v_w = H * DK, H * DV
    whole = lambda shape: pl.BlockSpec(shape, lambda b, c: (0,) * len(shape))
    return pl.pallas_call(
        _gla_kernel,
        grid=(B, S // C),
        in_specs=[
            pl.BlockSpec((None, C, qk_w), lambda b, c: (b, c, 0)),
            pl.BlockSpec((None, C, qk_w), lambda b, c: (b, c, 1)),
            pl.BlockSpec((None, C, v_w), lambda b, c: (b, c, (2 * qk_w) // v_w)),
            pl.BlockSpec((None, C, v_w), lambda b, c: (b, c, (2 * qk_w) // v_w + 1)),
            pl.BlockSpec((None, C, LANES), lambda b, c: (b, c, 0)),
            whole((LANES, qk_w)), whole((1, qk_w)), whole((1, v_w)), whole(mall.shape), whole(masks.shape),
        ],
        out_specs=pl.BlockSpec((None, C, v_w), lambda b, c: (b, c, 0)),
        out_shape=jax.ShapeDtypeStruct((B, S, v_w), BF16),
        scratch_shapes=[pltpu.VMEM((H, DV, DK), F32)],
        compiler_params=_cparams("parallel", "arbitrary"),
        name="gla",
    )(proj, proj, proj, proj, gl, w_gate_pad, b_gate.reshape(1, H * DK), gn.reshape(1, H * DV),
      jnp.asarray(mall, BF16), jnp.asarray(masks, F32))


def _lru_kernel(x_ref, y_ref, cw_ref, cb_ref, wa_ref, ba_ref, wx_ref, bx_ref, lam_ref, o_ref,
                xbuf, a_s, u_s, hcar):
    TC, W = LRU_TCHUNK, LRU_WIDTH

    @pl.when(pl.program_id(1) == 0)
    def _():
        xbuf[0:SUBLANES, :] = jnp.zeros((SUBLANES, W), F32)
        hcar[...] = jnp.zeros_like(hcar)

    xbuf[SUBLANES:SUBLANES + TC, :] = x_ref[...].astype(F32)
    xc = cb_ref[...]
    for tap in range(CONV_W):
        off = SUBLANES - (CONV_W - 1) + tap
        xc = xc + xbuf[off:off + TC, :] * cw_ref[tap:tap + 1, :]
    xbuf[0:SUBLANES, :] = xbuf[TC:TC + SUBLANES, :]

    xcb = xc.astype(BF16)
    for n in range(LRU_BLOCKS):
        sl = slice(n * LRU_BS, (n + 1) * LRU_BS)
        a_s[:, sl] = _dot(xcb[:, sl], wa_ref[n])
        u_s[:, sl] = _dot(xcb[:, sl], wx_ref[n])
    gate_r = _sigmoid(a_s[...] + ba_ref[...])
    gate_i = _sigmoid(u_s[...] + bx_ref[...])
    nl = -lam_ref[...]
    softplus = jnp.maximum(nl, 0.0) + jnp.log1p(jnp.exp(-jnp.abs(nl)))
    a = jnp.exp(-LRU_C * gate_r * softplus)
    a_s[...] = a
    u_s[...] = jnp.sqrt(1.0 - a * a) * gate_i * xc

    row = lax.broadcasted_iota(jnp.int32, (SUBLANES, W), 0)

    def body(r, h):
        sl = pl.ds(pl.multiple_of(r * SUBLANES, SUBLANES), SUBLANES)
        a8 = a_s[sl, :]
        u8 = u_s[sl, :]
        d = 1
        while d < SUBLANES:
            keep = row >= d
            a_sh = jnp.where(keep, pltpu.roll(a8, d, 0), 1.0)
            u_sh = jnp.where(keep, pltpu.roll(u8, d, 0), 0.0)
            u8 = a8 * u_sh + u8
            a8 = a8 * a_sh
            d *= 2
        h8 = a8 * h + u8
        u_s[sl, :] = h8
        return jnp.broadcast_to(h8[SUBLANES - 1:SUBLANES, :], (SUBLANES, W))

    hcar[...] = lax.fori_loop(0, TC // SUBLANES, body, hcar[...])
    o_ref[...] = (u_s[...] * _gelu_tanh(y_ref[...].astype(F32))).astype(o_ref.dtype)


def lru_mix(proj, conv_w, conv_b, w_a, b_a, w_x, b_x, lam, B, S):
    TC, W = LRU_TCHUNK, LRU_WIDTH
    row = lambda a: a.reshape(1, W)
    vec = pl.BlockSpec((1, W), lambda b, t: (0, 0))
    blk = pl.BlockSpec((LRU_BLOCKS, LRU_BS, LRU_BS), lambda b, t: (0, 0, 0))
    return pl.pallas_call(
        _lru_kernel,
        grid=(B, S // TC),
        in_specs=[
            pl.BlockSpec((None, TC, W), lambda b, t: (b, t, 0)),
            pl.BlockSpec((None, TC, W), lambda b, t: (b, t, 1)),
            pl.BlockSpec((CONV_W, W), lambda b, t: (0, 0)),
            vec, blk, vec, blk, vec, vec,
        ],
        out_specs=pl.BlockSpec((None, TC, W), lambda b, t: (b, t, 0)),
        out_shape=jax.ShapeDtypeStruct((B, S, W), BF16),
        scratch_shapes=[
            pltpu.VMEM((TC + SUBLANES, W), F32),
            pltpu.VMEM((TC, W), F32),
            pltpu.VMEM((TC, W), F32),
            pltpu.VMEM((SUBLANES, W), F32),
        ],
        compiler_params=_cparams("parallel", "arbitrary"),
        name="rglru",
    )(proj, proj, conv_w, row(conv_b), w_a.astype(BF16), row(b_a), w_x.astype(BF16), row(b_x), row(lam))


def _nsa_cmp_kernel(kc_ref, vc_ref, pek_ref, w1k_ref, w2k_ref, pev_ref, w1v_ref, w2v_ref, ko_ref, vo_ref,
                    buf, *, nblk):
    def compress(t_ref, pe_ref, w1_ref, w2_ref, o_ref, d):
        buf[...] = t_ref[...].astype(F32)
        ya = [jnp.zeros((nblk, CMP_HID), F32) for _ in range(NSA_GROUPS)]
        yb = [jnp.zeros((nblk, CMP_HID), F32) for _ in range(NSA_GROUPS)]
        for l in range(CMP_STRIDE):
            both = buf[pl.ds(l, nblk, stride=CMP_STRIDE), :]
            for g in range(NSA_GROUPS):
                rows = both[:, g * d:(g + 1) * d]
                lo = (rows + pe_ref[l:l + 1, :]).astype(BF16)
                hi = (rows + pe_ref[CMP_STRIDE + l:CMP_STRIDE + l + 1, :]).astype(BF16)
                ya[g] = ya[g] + _dot(lo, w1_ref[l * d:(l + 1) * d, :])
                yb[g] = yb[g] + _dot(hi, w1_ref[(CMP_STRIDE + l) * d:(CMP_STRIDE + l + 1) * d, :])
        for g in range(NSA_GROUPS):
            shifted = jnp.concatenate([yb[g][1:], jnp.zeros((1, CMP_HID), F32)], axis=0)
            hid = _gelu_tanh(ya[g] + shifted).astype(BF16)
            o_ref[g] = _dot(hid, w2_ref[...])

    compress(kc_ref, pek_ref, w1k_ref, w2k_ref, ko_ref, NSA_DK)
    compress(vc_ref, pev_ref, w1v_ref, w2v_ref, vo_ref, NSA_DV)


def nsa_compress(proj, pe_k, w1_k, w2_k, pe_v, w1_v, w2_v, B, S):
    nblk = S // CMP_STRIDE
    G = NSA_GROUPS
    kc_blk = (NSA_HEADS * NSA_DK) // LANES
    full2 = lambda a: pl.BlockSpec(a.shape, lambda b: (0, 0))
    w1k, w2k, w1v, w2v = (a.astype(BF16) for a in (w1_k, w2_k, w1_v, w2_v))
    out_sd = jax.ShapeDtypeStruct((B, G, nblk, NSA_DK), F32)
    return pl.pallas_call(
        functools.partial(_nsa_cmp_kernel, nblk=nblk),
        grid=(B,),
        in_specs=[
            pl.BlockSpec((None, S, LANES), lambda b: (b, 0, kc_blk)),
            pl.BlockSpec((None, S, LANES), lambda b: (b, 0, kc_blk + 1)),
            full2(pe_k), full2(w1k), full2(w2k), full2(pe_v), full2(w1v), full2(w2v),
        ],
        out_specs=[pl.BlockSpec((None, G, nblk, NSA_DK), lambda b: (b, 0, 0, 0))] * 2,
        out_shape=[out_sd, out_sd],
        scratch_shapes=[pltpu.VMEM((S, LANES), F32)],
        compiler_params=_cparams("parallel"),
        name="nsa_compress",
    )(proj, proj, pe_k, w1k, w2k, pe_v, w1v, w2v)


def _alibi_slope(head):
    return float(2.0 ** (-8.0 * (head + 1.0) / NSA_HEADS))


def _nsa_kernel(q_ref, gate_ref, kc_ref, vc_ref, ks_ref, vs_ref, kw_ref, vw_ref,
                ovt_ref, qt_ref, kt_ref, ct_ref, vt_ref, kind_ref, o_ref,
                qa_s, kca_s, vct_s, ksa_s, vst_s, kwa_s, vwt_s, *, seq):
    Q, TK, HPG, DK, DV, G = NSA_Q, NSA_TK, NSA_HPG, NSA_DK, NSA_DV, NSA_GROUPS
    R = HPG * Q
    W = WIN + Q
    ncmp = seq // CMP_STRIDE
    nslc = seq // SLC_L
    scale = DK ** -0.5
    qi = pl.program_id(1)
    q0 = qi * Q
    t_row = q0 + lax.broadcasted_iota(jnp.int32, (1, Q), 1)
    gates_t = _sigmoid(gate_ref[...].astype(F32)).T
    per_head = lambda a: jnp.concatenate([a] * HPG, axis=1)

    @pl.when(qi == 0)
    def _():
        for g in range(G):
            ks = slice(g * DK, (g + 1) * DK)
            vs = slice(g * DV, (g + 1) * DV)
            ksa_s[g, :, 0:LANES] = jnp.concatenate([ks_ref[:, ks], kt_ref[...]], axis=1)
            ksa_s[g, :, LANES:2 * LANES] = kind_ref[...]
            kwa_s[g] = jnp.concatenate([kw_ref[:, ks], kt_ref[...]], axis=1)
            tr = lambda a: a.astype(F32).T[0:NSA_VROWS].astype(BF16)
            vst_s[g] = tr(jnp.concatenate([vs_ref[:, vs], vt_ref[...]], axis=1))
            vwt_s[g] = tr(jnp.concatenate([vw_ref[:, vs], vt_ref[...]], axis=1))
            kca_s[g] = jnp.concatenate([kc_ref[g].astype(BF16), ct_ref[...]], axis=1)
            vct_s[g] = tr(jnp.concatenate([vc_ref[g], jnp.zeros((ncmp, NSA_TAIL), F32)], axis=1))

    o_cmp_t = []
    for g in range(G):
        for h in range(HPG):
            col = g * HPG + h
            qh = q_ref[:, col * DK:(col + 1) * DK].astype(F32) * (scale * LOG2E)
            tail = jnp.broadcast_to(qt_ref[col:col + 1, :], (Q, NSA_TAIL))
            qa_s[g, h * Q:(h + 1) * Q, 0:LANES] = jnp.concatenate([qh, tail], axis=1).astype(BF16)

        cend = CMP_STRIDE * lax.broadcasted_iota(jnp.int32, (ncmp, 1), 0) + (CMP_L - 1)
        bias_c = jnp.where(cend <= t_row, 0.0, -NSA_BIG)
        s = _dot_nt(kca_s[g], qa_s[g, :, 0:LANES]) + per_head(bias_c)
        e = jnp.exp2(s - jnp.max(s, axis=0, keepdims=True))
        inv = per_head(jnp.where(t_row >= CMP_L - 1, 1.0, 0.0)) / jnp.sum(e, axis=0, keepdims=True)
        p = e * inv
        psum_t = p[:, 0:Q]
        for h in range(1, HPG):
            psum_t = psum_t + p[:, h * Q:(h + 1) * Q]
        o_cmp_t.append(_dot(vct_s[g], p.astype(BF16)))

        imp_t = jnp.dot(ovt_ref[...], psum_t, preferred_element_type=F32, precision=HIGHEST)
        jcol = lax.broadcasted_iota(jnp.int32, (nslc, 1), 0)
        cur = t_row // SLC_L
        forced = (jcol == 0) | (jcol == cur) | (jcol == cur - 1)
        imp_t = jnp.where(forced, FORCE_SCORE, imp_t)
        imp_t = jnp.where(jcol <= cur, imp_t, -1.0)
        nrow = nslc // SUBLANES
        rows = [imp_t[r * SUBLANES:(r + 1) * SUBLANES] for r in range(nrow)]
        ranks = [jnp.zeros((SUBLANES, Q), F32) for _ in range(nrow)]
        jloc = lax.broadcasted_iota(jnp.int32, (SUBLANES, Q), 0)
        for j2 in range(nslc):
            other = jnp.broadcast_to(imp_t[j2:j2 + 1, :], (SUBLANES, Q))
            for r in range(nrow):
                if (r + 1) * SUBLANES - 1 <= j2:
                    ahead = jnp.where(other > rows[r], 1.0, 0.0)
                elif r * SUBLANES > j2:
                    ahead = jnp.where(other >= rows[r], 1.0, 0.0)
                else:
                    ahead = jnp.where(jloc + r * SUBLANES > j2, jnp.where(other >= rows[r], 1.0, 0.0),
                                      jnp.where(other > rows[r], 1.0, 0.0))
                ranks[r] = ranks[r] + ahead
        ntop = float(min(SLC_TOP, nslc))
        bias_t = jnp.concatenate([jnp.where(rk < ntop, 0.0, -NSA_BIG) for rk in ranks]
                                 + [jnp.zeros((LANES - nslc, Q), F32)], axis=0)
        sel_bias = bias_t.T.astype(BF16)
        for h in range(HPG):
            qa_s[g, h * Q:(h + 1) * Q, LANES:2 * LANES] = sel_bias

    state0 = tuple((jnp.full((1, R), -NSA_BIG, F32), jnp.zeros((NSA_VROWS, R), F32)) for _ in range(G))

    def sel_tile(kt, state, causal):
        k0 = pl.multiple_of(kt * TK, TK)
        scores = [_dot_nt(ksa_s[g, pl.ds(k0, TK), :], qa_s[g]) for g in range(G)]
        if causal:
            pos = k0 + lax.broadcasted_iota(jnp.int32, (TK, 1), 0)
            bias = per_head(jnp.where(pos <= t_row, 0.0, -NSA_BIG))
            scores = [s + bias for s in scores]
        out = []
        for g in range(G):
            m_old, acc = state[g]
            m_new = jnp.maximum(m_old, jnp.max(scores[g], axis=0, keepdims=True))
            p = jnp.exp2(scores[g] - m_new).astype(BF16)
            acc = jnp.exp2(m_old - m_new) * acc + _dot(vst_s[g, :, pl.ds(k0, TK)], p)
            out.append((m_new, acc))
        return tuple(out)

    last = q0 // TK

    state = lax.fori_loop(0, last, lambda kt, st: sel_tile(kt, st, False), state0)
    state = sel_tile(last, state, True)

    w0 = pl.multiple_of(jnp.maximum(q0 - WIN, 0), Q)
    wpos = w0 + lax.broadcasted_iota(jnp.int32, (W, 1), 0)
    dist_w = t_row - wpos
    bias_w = per_head(jnp.where((dist_w >= 0) & (dist_w < WIN), 0.0, -NSA_BIG))
    scores = [_dot_nt(kwa_s[g, pl.ds(w0, W), :], qa_s[g, :, 0:LANES]) + bias_w for g in range(G)]
    for g in range(G):
        s = scores[g]
        e = jnp.exp2(s - jnp.max(s, axis=0, keepdims=True))
        pv_w = _dot(vwt_s[g, :, pl.ds(w0, W)], e.astype(BF16))
        acc = state[g][1]
        o_sel_t = acc[0:DV] / acc[DV:DV + 1]
        o_win_t = pv_w[0:DV] / pv_w[DV:DV + 1]
        outs = []
        for h in range(HPG):
            col = g * HPG + h
            cs = slice(h * Q, (h + 1) * Q)
            outs.append(gates_t[col:col + 1] * o_cmp_t[g][0:DV, cs]
                        + gates_t[NSA_HEADS + col:NSA_HEADS + col + 1] * o_sel_t[:, cs]
                        + gates_t[2 * NSA_HEADS + col:2 * NSA_HEADS + col + 1] * o_win_t[:, cs])
        for h in range(0, HPG, 2):
            col = g * HPG + h
            o_ref[:, col * DV:(col + 2) * DV] = jnp.concatenate([outs[h], outs[h + 1]], axis=0).T.astype(o_ref.dtype)


def _bf16_split3(x):
    x = np.asarray(x, np.float32)
    rnd = lambda a: a.astype(BF16).astype(np.float32)
    hi = rnd(x)
    mid = rnd(x - hi)
    lo = rnd(x - hi - mid)
    return hi, mid, lo


def _pos_tail(pos):
    pos = np.asarray(pos)
    hi, lo = (pos // SLC_L) * SLC_L, pos % SLC_L
    tail = np.zeros((pos.shape[0], NSA_TAIL), np.float32)
    for c in range(3):
        tail[:, 2 * c] = hi
        tail[:, 2 * c + 1] = lo
    return tail


def nsa_attend(proj, k_cmp, v_cmp, B, S):
    Q, G, HPG = NSA_Q, NSA_GROUPS, NSA_HPG
    ncmp, nslc = S // CMP_STRIDE, S // SLC_L
    assert S >= WIN + Q and S % NSA_TK == 0 and nslc <= LANES and nslc % SUBLANES == 0
    n = np.arange(ncmp)[:, None]
    j = np.arange(nslc)[None, :]
    overlap = ((n * CMP_STRIDE < (j + 1) * SLC_L) & (n * CMP_STRIDE + CMP_L > j * SLC_L)
               & (n < (S - CMP_L) // CMP_STRIDE + 1)).astype(np.float32)
    slopes = np.asarray([_alibi_slope(h) for h in range(NSA_HEADS)], np.float32) * np.float32(LOG2E)
    qtail = np.zeros((NSA_HEADS, NSA_TAIL), np.float32)
    for c, part in enumerate(_bf16_split3(slopes)):
        qtail[:, 2 * c] = part
        qtail[:, 2 * c + 1] = part
    ktail = _pos_tail(np.arange(S))
    ctail = _pos_tail(np.arange(ncmp) * CMP_STRIDE + CMP_L - 1)
    vtail = np.zeros((S, NSA_TAIL), np.float32)
    vtail[:, 0] = 1.0
    kind = (np.arange(S)[:, None] // SLC_L == np.arange(LANES)[None, :]).astype(np.float32)
    qw = NSA_HEADS * NSA_DK
    kv0 = qw // LANES
    seqblk = lambda c: pl.BlockSpec((None, S, LANES), lambda b, i: (b, 0, c))
    cmpblk = pl.BlockSpec((None, G, ncmp, NSA_DK), lambda b, i: (b, 0, 0, 0))
    const = lambda a: pl.BlockSpec(a.shape, lambda b, i: (0, 0))
    consts = [jnp.asarray(overlap.T, F32), jnp.asarray(qtail, F32), jnp.asarray(ktail, BF16),
              jnp.asarray(ctail, BF16), jnp.asarray(vtail, BF16), jnp.asarray(kind, BF16)]
    return pl.pallas_call(
        functools.partial(_nsa_kernel, seq=S),
        grid=(B, S // Q),
        in_specs=[
            pl.BlockSpec((None, Q, qw), lambda b, i: (b, i, 0)),
            pl.BlockSpec((None, Q, LANES), lambda b, i: (b, i, kv0 + 6)),
            cmpblk, cmpblk,
            seqblk(kv0 + 2), seqblk(kv0 + 3), seqblk(kv0 + 4), seqblk(kv0 + 5),
        ] + [const(a) for a in consts],
        out_specs=pl.BlockSpec((None, Q, NSA_HEADS * NSA_DV), lambda b, i: (b, i, 0)),
        out_shape=jax.ShapeDtypeStruct((B, S, NSA_HEADS * NSA_DV), BF16),
        scratch_shapes=[
            pltpu.VMEM((G, HPG * Q, 2 * LANES), BF16),
            pltpu.VMEM((G, ncmp, LANES), BF16),
            pltpu.VMEM((G, NSA_VROWS, ncmp), BF16),
            pltpu.VMEM((G, S, 2 * LANES), BF16),
            pltpu.VMEM((G, NSA_VROWS, S), BF16),
            pltpu.VMEM((G, S, LANES), BF16),
            pltpu.VMEM((G, NSA_VROWS, S), BF16),
        ],
        compiler_params=_cparams("parallel", "arbitrary"),
        name="nsa_attend",
    )(proj, proj, k_cmp, v_cmp, proj, proj, proj, proj, *consts)


def _pad_cols(w, n):
    return jnp.pad(w, ((0, 0), (0, n - w.shape[1])))


def kernel(x, norm_mix_pre, norm_mix_post, norm_mlp_pre, norm_mlp_post, mlp_w_up, mlp_w_down, ret_w_in, ret_gn, ret_w_out, gla_w_in, gla_w_gate_up, gla_b_gate, gla_gn, gla_w_out, lru_w_in, lru_conv_w, lru_conv_b, lru_w_a, lru_b_a, lru_w_x, lru_b_x, lru_lambda, lru_w_out, nsa_w_in, nsa_pe_k, nsa_w1_k, nsa_w2_k, nsa_pe_v, nsa_w1_v, nsa_w2_v, nsa_w_out):
    B, S, D = x.shape
    T = B * S
    depth = norm_mix_pre.shape[0]
    x2 = x.reshape(T, D)
    for i in range(depth):
        m, j = i % 4, i // 4
        if m == 0:
            proj = norm_proj(x2, norm_mix_pre[i], ret_w_in[j].astype(BF16), tn=PROJ_TN)
            mix = retention_mix(proj.reshape(B, S, -1), ret_gn[j], B, S)
            w_out = ret_w_out[j]
        elif m == 1:
            nmain = gla_w_in.shape[2] - GLA_RANK
            w_low = _pad_cols(gla_w_in[j][:, nmain:], LANES).astype(BF16)
            proj, gl = norm_proj(x2, norm_mix_pre[i], gla_w_in[j][:, :nmain].astype(BF16), tn=PROJ_TN, w_extra=w_low)
            w_gate_pad = jnp.pad(gla_w_gate_up[j], ((0, LANES - GLA_RANK), (0, 0)))
            mix = gla_mix(proj.reshape(B, S, -1), gl.reshape(B, S, -1), w_gate_pad, gla_b_gate[j], gla_gn[j], B, S)
            w_out = gla_w_out[j]
        elif m == 2:
            proj = norm_proj(x2, norm_mix_pre[i], lru_w_in[j].astype(BF16), tn=PROJ_TN)
            mix = lru_mix(proj.reshape(B, S, -1), lru_conv_w[j], lru_conv_b[j], lru_w_a[j], lru_b_a[j],
                          lru_w_x[j], lru_b_x[j], lru_lambda[j], B, S)
            w_out = lru_w_out[j]
        else:
            npad = -(-nsa_w_in.shape[2] // PROJ_TN) * PROJ_TN
            proj = norm_proj(x2, norm_mix_pre[i], _pad_cols(nsa_w_in[j], npad).astype(BF16), tn=PROJ_TN)
            proj = proj.reshape(B, S, -1)
            k_cmp, v_cmp = nsa_compress(proj, nsa_pe_k[j], nsa_w1_k[j], nsa_w2_k[j],
                                        nsa_pe_v[j], nsa_w1_v[j], nsa_w2_v[j], B, S)
            mix = nsa_attend(proj, k_cmp, v_cmp, B, S)
            w_out = nsa_w_out[j]
        x2 = sublayer_tail(mix.reshape(T, -1), w_out.astype(BF16), norm_mix_post[i], x2, norm_mlp_pre[i],
                           mlp_w_up[i].astype(BF16), mlp_w_down[i].astype(BF16), norm_mlp_post[i])
    return x2.reshape(B, S, D)
```

```python
import functools
import math

import numpy as np
import jax
import jax.numpy as jnp
from jax import lax
from jax.experimental import pallas as pl
from jax.experimental.pallas import tpu as pltpu

F32 = jnp.float32
BF16 = jnp.bfloat16
HIGHEST = lax.Precision.HIGHEST

D_MODEL = 1024
D_FF = 4 * D_MODEL
NORM_EPS = 1e-6
NEG_INF = -1e30

RET_HEADS, RET_DK, RET_DV = 4, 256, 512
RET_CHUNK = 256

GLA_HEADS, GLA_DK, GLA_DV = 4, 128, 256
GLA_RANK = 16
GLA_TAU = 16.0
GLA_CHUNK = 128

LRU_WIDTH = D_MODEL
LRU_BLOCKS = 8
LRU_BS = LRU_WIDTH // LRU_BLOCKS
CONV_W = 4
LRU_C = 8.0
LRU_TCHUNK = 256

NSA_HEADS, NSA_GROUPS, NSA_HPG = 16, 2, 8
NSA_DK = 64
NSA_DV = 64
CMP_L, CMP_STRIDE = 32, 16
CMP_HID = 2 * NSA_DK
SLC_L, SLC_TOP = 64, 16
WIN = 512
FORCE_SCORE = 1e6
NSA_Q = 256
NSA_TK = 512
NSA_RB = 256
NSA_TAIL = 64
NSA_VROWS = NSA_DV + 16
NSA_BIG = 1e30
LOG2E = math.log2(math.e)

LANES = 128
SUBLANES = 8
VMEM_LIMIT = 56 * 1024 * 1024


def _cparams(*sem):
    return pltpu.CompilerParams(dimension_semantics=sem, vmem_limit_bytes=VMEM_LIMIT)


def _rms(x, g):
    return x * lax.rsqrt(jnp.mean(x * x, axis=-1, keepdims=True) + NORM_EPS) * g


def _dot(a, b):
    return jnp.dot(a, b, preferred_element_type=F32)


def _dot_nt(a, b):
    return lax.dot_general(a, b, (((1,), (1,)), ((), ())), preferred_element_type=F32)


def _dot_tn(a, b):
    return lax.dot_general(a, b, (((0,), (0,)), ((), ())), preferred_element_type=F32)


def _gelu_tanh(x):
    return 0.5 * x * (1.0 + jnp.tanh(math.sqrt(2.0 / math.pi) * (x + 0.044715 * (x * x * x))))


def _sigmoid(x):
    return 0.5 * (jnp.tanh(0.5 * x) + 1.0)


PROJ_TN = 1024


def _norm_proj_kernel(x_ref, g_ref, w_ref, o_ref, h_ref):
    @pl.when(pl.program_id(1) == 0)
    def _():
        h_ref[...] = _rms(x_ref[...], g_ref[...]).astype(BF16)

    o_ref[...] = _dot(h_ref[...], w_ref[...]).astype(o_ref.dtype)


def norm_proj(x2d, g, w_bf16, tn, tm=1024):
    T, D = x2d.shape
    N = w_bf16.shape[1]
    assert T % tm == 0 and N % tn == 0
    return pl.pallas_call(
        _norm_proj_kernel,
        grid=(T // tm, N // tn),
        in_specs=[
            pl.BlockSpec((tm, D), lambda i, j: (i, 0)),
            pl.BlockSpec((1, D), lambda i, j: (0, 0)),
            pl.BlockSpec((D, tn), lambda i, j: (0, j)),
        ],
        out_specs=pl.BlockSpec((tm, tn), lambda i, j: (i, j)),
        out_shape=jax.ShapeDtypeStruct((T, N), BF16),
        scratch_shapes=[pltpu.VMEM((tm, D), BF16)],
        compiler_params=_cparams("parallel", "arbitrary"),
        name="norm_proj",
    )(x2d, g.reshape(1, D), w_bf16)


def _proj_kernel(h_ref, w_ref, o_ref):
    o_ref[...] = _dot(h_ref[...], w_ref[...]).astype(o_ref.dtype)


def _proj_extra_kernel(h_ref, w_ref, we_ref, o_ref, oe_ref):
    j = pl.program_id(1)

    @pl.when(j < pl.num_programs(1) - 1)
    def _():
        o_ref[...] = _dot(h_ref[...], w_ref[...]).astype(o_ref.dtype)

    @pl.when(j == pl.num_programs(1) - 1)
    def _():
        oe_ref[...] = _dot(h_ref[...], we_ref[...])


def in_proj(h2d, w_bf16, tn, w_extra=None, tm=1024):
    T, D = h2d.shape
    N = w_bf16.shape[1]
    assert T % tm == 0 and N % tn == 0
    nj = N // tn
    if w_extra is None:
        return pl.pallas_call(
            _proj_kernel,
            grid=(T // tm, nj),
            in_specs=[pl.BlockSpec((tm, D), lambda i, j: (i, 0)), pl.BlockSpec((D, tn), lambda i, j: (0, j))],
            out_specs=pl.BlockSpec((tm, tn), lambda i, j: (i, j)),
            out_shape=jax.ShapeDtypeStruct((T, N), BF16),
            compiler_params=_cparams("parallel", "parallel"),
            name="in_proj",
        )(h2d, w_bf16)
    ne = w_extra.shape[1]
    last = nj - 1
    return pl.pallas_call(
        _proj_extra_kernel,
        grid=(T // tm, nj + 1),
        in_specs=[
            pl.BlockSpec((tm, D), lambda i, j: (i, 0)),
            pl.BlockSpec((D, tn), lambda i, j: (0, jnp.minimum(j, last))),
            pl.BlockSpec((D, ne), lambda i, j: (0, 0)),
        ],
        out_specs=[pl.BlockSpec((tm, tn), lambda i, j: (i, jnp.minimum(j, last))),
                   pl.BlockSpec((tm, ne), lambda i, j: (i, 0))],
        out_shape=[jax.ShapeDtypeStruct((T, N), BF16), jax.ShapeDtypeStruct((T, ne), F32)],
        compiler_params=_cparams("parallel", "arbitrary"),
        name="in_proj_extra",
    )(h2d, w_bf16, w_extra)


MLP_FCHUNK = 512


def _tail_kernel(a_ref, wo_ref, gp_ref, x_ref, g1_ref, wu_ref, wd_ref, g2_ref, *rest):
    o_ref = rest[-2] if len(rest) == 3 else rest[0]
    x = x_ref[...] + _rms(_dot(a_ref[...], wo_ref[...]), gp_ref[...])
    h = _rms(x, g1_ref[...]).astype(BF16)
    acc = jnp.zeros(x.shape, F32)
    for c in range(D_FF // MLP_FCHUNK):
        sl = slice(c * MLP_FCHUNK, (c + 1) * MLP_FCHUNK)
        u = jnp.maximum(_dot(h, wu_ref[:, sl]), 0.0)
        acc = acc + _dot((u * u).astype(BF16), wd_ref[sl, :])
    out = x + _rms(acc, g2_ref[...])
    o_ref[...] = out
    if len(rest) == 3:
        gn_ref, _, hn_ref = rest
        hn_ref[...] = _rms(out, gn_ref[...]).astype(hn_ref.dtype)


def sublayer_tail(a2d, wo_bf16, g_post, x2d, g1, wu_bf16, wd_bf16, g2, g_next=None, tm=512):
    T, D = x2d.shape
    K = a2d.shape[1]
    F = wu_bf16.shape[1]
    resident = lambda shape: pl.BlockSpec(shape, lambda i: (0, 0), pipeline_mode=pl.Buffered(1))
    tile = pl.BlockSpec((tm, D), lambda i: (i, 0))
    row = lambda a: a.reshape(1, D)
    in_specs = [pl.BlockSpec((tm, K), lambda i: (i, 0)), resident((K, D)), resident((1, D)), tile,
                resident((1, D)), resident((D, F)), resident((F, D)), resident((1, D))]
    args = [a2d, wo_bf16, row(g_post), x2d, row(g1), wu_bf16, wd_bf16, row(g2)]
    out_specs, out_shape = tile, jax.ShapeDtypeStruct((T, D), F32)
    if g_next is not None:
        in_specs.append(resident((1, D)))
        args.append(row(g_next))
        out_specs, out_shape = [tile, tile], [out_shape, jax.ShapeDtypeStruct((T, D), BF16)]
    return pl.pallas_call(
        _tail_kernel,
        grid=(T // tm,),
        in_specs=in_specs,
        out_specs=out_specs,
        out_shape=out_shape,
        compiler_params=_cparams("parallel"),
        name="sublayer_tail",
    )(*args)


def _ret_kernel(q_ref, k_ref, v_ref, g_ref, gn_ref, di_ref, dq_ref, dk_ref, dc_ref, o_ref, st_ref):
    H, DK, DV = RET_HEADS, RET_DK, RET_DV

    @pl.when(pl.program_id(1) == 0)
    def _():
        st_ref[...] = jnp.zeros_like(st_ref)

    qb = [q_ref[:, h * DK:(h + 1) * DK] for h in range(H)]
    kb = [k_ref[:, h * DK:(h + 1) * DK] * (DK ** -0.5) for h in range(H)]
    vb = [v_ref[:, h * DV:(h + 1) * DV] for h in range(H)]
    st = [st_ref[h] for h in range(H)]
    scores = [_dot_nt(qb[h], kb[h]) for h in range(H)]
    inter = [_dot((qb[h].astype(F32) * dq_ref[h]).astype(BF16), st[h].astype(BF16)) for h in range(H)]
    for h in range(H):
        st_ref[h] = st[h] * dc_ref[h] + _dot_tn((kb[h].astype(F32) * dk_ref[h]).astype(BF16), vb[h])
    for h in range(H):
        o = _dot((scores[h] * di_ref[h]).astype(BF16), vb[h]) + inter[h]
        o = o - jnp.mean(o, axis=-1, keepdims=True)
        hs = slice(h * DV, (h + 1) * DV)
        gate = g_ref[:, hs].astype(F32)
        o_ref[:, hs] = (_rms(o, gn_ref[:, hs]) * (gate * _sigmoid(gate))).astype(o_ref.dtype)


def retention_mix(proj, gn, B, S):
    H, DK, DV, C = RET_HEADS, RET_DK, RET_DV, RET_CHUNK
    assert (DK ** -0.5) == 2.0 ** round(math.log2(DK ** -0.5))
    log_gamma = jnp.log1p(-jnp.exp2(-5.0 - jnp.arange(H, dtype=F32)))
    pos = jnp.arange(C, dtype=F32)
    diff = pos[:, None] - pos[None, :]
    d_intra = jnp.where(diff >= 0, jnp.exp(log_gamma[:, None, None] * jnp.maximum(diff, 0.0)), 0.0)
    d_q = jnp.exp(log_gamma[:, None] * (pos + 1.0))[:, :, None]
    d_k = jnp.exp(log_gamma[:, None] * (C - 1.0 - pos))[:, :, None]
    d_c = jnp.exp(log_gamma * C)[:, None, None]
    qk_w, v_w = H * DK, H * DV
    whole = lambda a: pl.BlockSpec(a.shape, lambda b, c: (0,) * a.ndim)
    return pl.pallas_call(
        _ret_kernel,
        grid=(B, S // C),
        in_specs=[
            pl.BlockSpec((None, C, qk_w), lambda b, c: (b, c, 0)),
            pl.BlockSpec((None, C, qk_w), lambda b, c: (b, c, 1)),
            pl.BlockSpec((None, C, v_w), lambda b, c: (b, c, (2 * qk_w) // v_w)),
            pl.BlockSpec((None, C, v_w), lambda b, c: (b, c, (2 * qk_w) // v_w + 1)),
            pl.BlockSpec((1, v_w), lambda b, c: (0, 0)),
            whole(d_intra), whole(d_q), whole(d_k), whole(d_c),
        ],
        out_specs=pl.BlockSpec((None, C, v_w), lambda b, c: (b, c, 0)),
        out_shape=jax.ShapeDtypeStruct((B, S, v_w), BF16),
        scratch_shapes=[pltpu.VMEM((H, DK, DV), F32)],
        compiler_params=_cparams("parallel", "arbitrary"),
        name="retention",
    )(proj, proj, proj, proj, gn.reshape(1, v_w), d_intra, d_q, d_k, d_c)


def _gla_level_sizes(C):
    sizes, s = [], 2
    while s <= C:
        sizes.append(s)
        s *= 2
    return sizes


def _gla_constants(C):
    t = np.arange(C)[:, None]
    r = np.arange(C)[None, :]
    mats = [(r <= t)]
    mats.append(r > t)
    masks = [np.eye(C, dtype=bool)]
    for bs in _gla_level_sizes(C):
        half = bs // 2
        mid = (t // bs) * bs + half - 1
        second = (t % bs) >= half
        m = np.where(second, (r > mid) & (r <= t), (r > t) & (r <= mid))
        mats.append(m)
        same = (t // bs) == (r // bs)
        masks.append(same & second & ((r % bs) < half))
    return (np.concatenate(mats, 0).astype(np.float32), np.stack(masks, 0).astype(np.float32))


def _gla_kernel(q_ref, k_ref, v_ref, r_ref, gl_ref, wg_ref, bg_ref, gn_ref, mall_ref, mask_ref, o_ref, st_ref):
    C, DK = GLA_CHUNK, GLA_DK

    H, DV = GLA_HEADS, GLA_DV
    W = H * DK

    @pl.when(pl.program_id(1) == 0)
    def _():
        st_ref[...] = jnp.zeros_like(st_ref)

    z = jnp.dot(gl_ref[...], wg_ref[...], preferred_element_type=F32, precision=HIGHEST) + bg_ref[...]
    la = (jnp.minimum(z, 0.0) - jnp.log1p(jnp.exp(-jnp.abs(z)))) * (1.0 / GLA_TAU)
    la_hi = la.astype(BF16)
    la_lo = (la - la_hi.astype(F32)).astype(BF16)
    sums = _dot(mall_ref[...], jnp.concatenate([la_hi, la_lo], axis=1))
    decay = jnp.exp(sums[:, :W] + sums[:, W:])
    e_b = decay[0:C]
    e_rest = decay[C:2 * C]
    e_last = decay[C - 1:C]

    q = q_ref[...].astype(F32) * (DK ** -0.5)
    k = k_ref[...].astype(F32)
    nlev = len(_gla_level_sizes(C))
    qe = [q.astype(BF16)] + [(q * decay[(2 + l) * C:(3 + l) * C]).astype(BF16) for l in range(nlev)]
    ke = [k_ref[...]] + [(k * decay[(2 + l) * C:(3 + l) * C]).astype(BF16) for l in range(nlev)]
    q_in = (q * e_b).astype(BF16)
    k_out = (k * e_rest).astype(BF16)
    for h in range(H):
        ks = slice(h * DK, (h + 1) * DK)
        vs = slice(h * DV, (h + 1) * DV)
        vb = v_ref[:, vs]
        a = _dot_nt(qe[0][:, ks], ke[0][:, ks]) * mask_ref[0]
        for l in range(nlev):
            a = a + _dot_nt(qe[1 + l][:, ks], ke[1 + l][:, ks]) * mask_ref[1 + l]
        st = st_ref[h]
        o = _dot(a.astype(BF16), vb) + _dot_nt(q_in[:, ks], st.astype(BF16))
        st_ref[h] = st * e_last[:, ks] + _dot_tn(vb, k_out[:, ks])
        gate = r_ref[:, vs].astype(F32)
        o_ref[:, vs] = (_rms(o, gn_ref[:, vs]) * (gate * _sigmoid(gate))).astype(o_ref.dtype)


def gla_mix(proj, gl, w_gate_pad, b_gate, gn, B, S):
    H, DK, DV, C = GLA_HEADS, GLA_DK, GLA_DV, GLA_CHUNK
    mall, masks = _gla_constants(C)
    qk_w, v_w = H * DK, H * DV
    whole = lambda shape: pl.BlockSpec(shape, lambda b, c: (0,) * len(shape))
    return pl.pallas_call(
        _gla_kernel,
        grid=(B, S // C),
        in_specs=[
            pl.BlockSpec((None, C, qk_w), lambda b, c: (b, c, 0)),
            pl.BlockSpec((None, C, qk_w), lambda b, c: (b, c, 1)),
            pl.BlockSpec((None, C, v_w), lambda b, c: (b, c, (2 * qk_w) // v_w)),
            pl.BlockSpec((None, C, v_w), lambda b, c: (b, c, (2 * qk_w) // v_w + 1)),
            pl.BlockSpec((None, C, LANES), lambda b, c: (b, c, 0)),
            whole((LANES, qk_w)), whole((1, qk_w)), whole((1, v_w)), whole(mall.shape), whole(masks.shape),
        ],
        out_specs=pl.BlockSpec((None, C, v_w), lambda b, c: (b, c, 0)),
        out_shape=jax.ShapeDtypeStruct((B, S, v_w), BF16),
        scratch_shapes=[pltpu.VMEM((H, DV, DK), F32)],
        compiler_params=_cparams("parallel", "arbitrary"),
        name="gla",
    )(proj, proj, proj, proj, gl, w_gate_pad, b_gate.reshape(1, H * DK), gn.reshape(1, H * DV),
      jnp.asarray(mall, BF16), jnp.asarray(masks, F32))


def _lru_kernel(x_ref, y_ref, cw_ref, cb_ref, wa_ref, ba_ref, wx_ref, bx_ref, lam_ref, o_ref,
                xbuf, a_s, u_s, hcar):
    TC, W = LRU_TCHUNK, LRU_WIDTH

    @pl.when(pl.program_id(1) == 0)
    def _():
        xbuf[0:SUBLANES, :] = jnp.zeros((SUBLANES, W), F32)
        hcar[...] = jnp.zeros_like(hcar)

    xbuf[SUBLANES:SUBLANES + TC, :] = x_ref[...].astype(F32)
    xc = cb_ref[...]
    for tap in range(CONV_W):
        off = SUBLANES - (CONV_W - 1) + tap
        xc = xc + xbuf[off:off + TC, :] * cw_ref[tap:tap + 1, :]
    xbuf[0:SUBLANES, :] = xbuf[TC:TC + SUBLANES, :]

    xcb = xc.astype(BF16)
    for n in range(LRU_BLOCKS):
        sl = slice(n * LRU_BS, (n + 1) * LRU_BS)
        a_s[:, sl] = _dot(xcb[:, sl], wa_ref[n])
        u_s[:, sl] = _dot(xcb[:, sl], wx_ref[n])
    gate_r = _sigmoid(a_s[...] + ba_ref[...])
    gate_i = _sigmoid(u_s[...] + bx_ref[...])
    nl = -lam_ref[...]
    softplus = jnp.maximum(nl, 0.0) + jnp.log1p(jnp.exp(-jnp.abs(nl)))
    a = jnp.exp(-LRU_C * gate_r * softplus)
    a_s[...] = a
    u_s[...] = jnp.sqrt(1.0 - a * a) * gate_i * xc

    row = lax.broadcasted_iota(jnp.int32, (SUBLANES, W), 0)

    def body(r, h):
        sl = pl.ds(pl.multiple_of(r * SUBLANES, SUBLANES), SUBLANES)
        a8 = a_s[sl, :]
        u8 = u_s[sl, :]
        d = 1
        while d < SUBLANES:
            keep = row >= d
            a_sh = jnp.where(keep, pltpu.roll(a8, d, 0), 1.0)
            u_sh = jnp.where(keep, pltpu.roll(u8, d, 0), 0.0)
            u8 = a8 * u_sh + u8
            a8 = a8 * a_sh
            d *= 2
        h8 = a8 * h + u8
        u_s[sl, :] = h8
        return jnp.broadcast_to(h8[SUBLANES - 1:SUBLANES, :], (SUBLANES, W))

    hcar[...] = lax.fori_loop(0, TC // SUBLANES, body, hcar[...])
    o_ref[...] = (u_s[...] * _gelu_tanh(y_ref[...].astype(F32))).astype(o_ref.dtype)


def lru_mix(proj, conv_w, conv_b, w_a, b_a, w_x, b_x, lam, B, S):
    TC, W = LRU_TCHUNK, LRU_WIDTH
    row = lambda a: a.reshape(1, W)
    vec = pl.BlockSpec((1, W), lambda b, t: (0, 0))
    blk = pl.BlockSpec((LRU_BLOCKS, LRU_BS, LRU_BS), lambda b, t: (0, 0, 0))
    return pl.pallas_call(
        _lru_kernel,
        grid=(B, S // TC),
        in_specs=[
            pl.BlockSpec((None, TC, W), lambda b, t: (b, t, 0)),
            pl.BlockSpec((None, TC, W), lambda b, t: (b, t, 1)),
            pl.BlockSpec((CONV_W, W), lambda b, t: (0, 0)),
            vec, blk, vec, blk, vec, vec,
        ],
        out_specs=pl.BlockSpec((None, TC, W), lambda b, t: (b, t, 0)),
        out_shape=jax.ShapeDtypeStruct((B, S, W), BF16),
        scratch_shapes=[
            pltpu.VMEM((TC + SUBLANES, W), F32),
            pltpu.VMEM((TC, W), F32),
            pltpu.VMEM((TC, W), F32),
            pltpu.VMEM((SUBLANES, W), F32),
        ],
        compiler_params=_cparams("parallel", "arbitrary"),
        name="rglru",
    )(proj, proj, conv_w, row(conv_b), w_a.astype(BF16), row(b_a), w_x.astype(BF16), row(b_x), row(lam))


def _nsa_cmp_kernel(kc_ref, vc_ref, pek_ref, w1k_ref, w2k_ref, pev_ref, w1v_ref, w2v_ref, ko_ref, vo_ref,
                    buf, *, nblk):
    def compress(t_ref, pe_ref, w1_ref, w2_ref, o_ref, d):
        buf[...] = t_ref[...].astype(F32)
        ya = [jnp.zeros((nblk, CMP_HID), F32) for _ in range(NSA_GROUPS)]
        yb = [jnp.zeros((nblk, CMP_HID), F32) for _ in range(NSA_GROUPS)]
        for l in range(CMP_STRIDE):
            both = buf[pl.ds(l, nblk, stride=CMP_STRIDE), :]
            for g in range(NSA_GROUPS):
                rows = both[:, g * d:(g + 1) * d]
                lo = (rows + pe_ref[l:l + 1, :]).astype(BF16)
                hi = (rows + pe_ref[CMP_STRIDE + l:CMP_STRIDE + l + 1, :]).astype(BF16)
                ya[g] = ya[g] + _dot(lo, w1_ref[l * d:(l + 1) * d, :])
                yb[g] = yb[g] + _dot(hi, w1_ref[(CMP_STRIDE + l) * d:(CMP_STRIDE + l + 1) * d, :])
        for g in range(NSA_GROUPS):
            shifted = jnp.concatenate([yb[g][1:], jnp.zeros((1, CMP_HID), F32)], axis=0)
            hid = _gelu_tanh(ya[g] + shifted).astype(BF16)
            o_ref[g] = _dot(hid, w2_ref[...])

    compress(kc_ref, pek_ref, w1k_ref, w2k_ref, ko_ref, NSA_DK)
    compress(vc_ref, pev_ref, w1v_ref, w2v_ref, vo_ref, NSA_DV)


def nsa_compress(proj, pe_k, w1_k, w2_k, pe_v, w1_v, w2_v, B, S):
    nblk = S // CMP_STRIDE
    G = NSA_GROUPS
    kc_blk = (NSA_HEADS * NSA_DK) // LANES
    full2 = lambda a: pl.BlockSpec(a.shape, lambda b: (0, 0))
    w1k, w2k, w1v, w2v = (a.astype(BF16) for a in (w1_k, w2_k, w1_v, w2_v))
    out_sd = jax.ShapeDtypeStruct((B, G, nblk, NSA_DK), F32)
    return pl.pallas_call(
        functools.partial(_nsa_cmp_kernel, nblk=nblk),
        grid=(B,),
        in_specs=[
            pl.BlockSpec((None, S, LANES), lambda b: (b, 0, kc_blk)),
            pl.BlockSpec((None, S, LANES), lambda b: (b, 0, kc_blk + 1)),
            full2(pe_k), full2(w1k), full2(w2k), full2(pe_v), full2(w1v), full2(w2v),
        ],
        out_specs=[pl.BlockSpec((None, G, nblk, NSA_DK), lambda b: (b, 0, 0, 0))] * 2,
        out_shape=[out_sd, out_sd],
        scratch_shapes=[pltpu.VMEM((S, LANES), F32)],
        compiler_params=_cparams("parallel"),
        name="nsa_compress",
    )(proj, proj, pe_k, w1k, w2k, pe_v, w1v, w2v)


def _alibi_slope(head):
    return float(2.0 ** (-8.0 * (head + 1.0) / NSA_HEADS))


def _nsa_kernel(q_ref, gate_ref, kc_ref, vc_ref, ks_ref, vs_ref, kw_ref, vw_ref,
                ovt_ref, qt_ref, kt_ref, ct_ref, vt_ref, kind_ref, o_ref,
                qa_s, kca_s, vct_s, ksa_s, vst_s, kwa_s, vwt_s, part_s, *, seq):
    Q, TK, HPG, DK, DV, G = NSA_Q, NSA_TK, NSA_HPG, NSA_DK, NSA_DV, NSA_GROUPS
    R = HPG * Q
    W = WIN + Q
    ncmp = seq // CMP_STRIDE
    nslc = seq // SLC_L
    scale = DK ** -0.5
    qi = pl.program_id(1)
    q0 = qi * Q
    t_row = q0 + lax.broadcasted_iota(jnp.int32, (1, Q), 1)
    gates_t = _sigmoid(gate_ref[...].astype(F32)).T
    per_head = lambda a: jnp.concatenate([a] * HPG, axis=1)

    @pl.when(qi == 0)
    def _():
        for g in range(G):
            ks = slice(g * DK, (g + 1) * DK)
            vs = slice(g * DV, (g + 1) * DV)
            ksa_s[g, :, 0:LANES] = jnp.concatenate([ks_ref[:, ks], kt_ref[...]], axis=1)
            ksa_s[g, :, LANES:2 * LANES] = kind_ref[...]
            kwa_s[g] = jnp.concatenate([kw_ref[:, ks], kt_ref[...]], axis=1)
            tr = lambda a: a.astype(F32).T[0:NSA_VROWS].astype(BF16)
            vst_s[g] = tr(jnp.concatenate([vs_ref[:, vs], vt_ref[...]], axis=1))
            vwt_s[g] = tr(jnp.concatenate([vw_ref[:, vs], vt_ref[...]], axis=1))
            kca_s[g] = jnp.concatenate([kc_ref[g].astype(BF16), ct_ref[...]], axis=1)
            vct_s[g] = tr(jnp.concatenate([vc_ref[g], jnp.zeros((ncmp, NSA_TAIL), F32)], axis=1))

    for g in range(G):
        for h in range(HPG):
            col = g * HPG + h
            qh = q_ref[:, col * DK:(col + 1) * DK].astype(F32) * (scale * LOG2E)
            tail = jnp.broadcast_to(qt_ref[col:col + 1, :], (Q, NSA_TAIL))
            qa_s[g, h * Q:(h + 1) * Q, 0:LANES] = jnp.concatenate([qh, tail], axis=1).astype(BF16)

    cend = CMP_STRIDE * lax.broadcasted_iota(jnp.int32, (ncmp, 1), 0) + (CMP_L - 1)
    bias_c = per_head(jnp.where(cend <= t_row, 0.0, -NSA_BIG))
    w0 = pl.multiple_of(jnp.maximum(q0 - WIN, 0), Q)
    wpos = w0 + lax.broadcasted_iota(jnp.int32, (W, 1), 0)
    dist_w = t_row - wpos
    bias_w = per_head(jnp.where((dist_w >= 0) & (dist_w < WIN), 0.0, -NSA_BIG))
    scores_c = [_dot_nt(kca_s[g], qa_s[g, :, 0:LANES]) + bias_c for g in range(G)]
    scores_w = [_dot_nt(kwa_s[g, pl.ds(w0, W), :], qa_s[g, :, 0:LANES]) + bias_w for g in range(G)]

    psum_t = []
    for g in range(G):
        s = scores_c[g]
        e = jnp.exp2(s - jnp.max(s, axis=0, keepdims=True))
        inv = per_head(jnp.where(t_row >= CMP_L - 1, 1.0, 0.0)) / jnp.sum(e, axis=0, keepdims=True)
        p = e * inv
        ps = p[:, 0:Q]
        for h in range(1, HPG):
            ps = ps + p[:, h * Q:(h + 1) * Q]
        psum_t.append(ps)
        o_cmp_t = _dot(vct_s[g], p.astype(BF16))
        for h in range(HPG):
            col = g * HPG + h
            cs = slice(h * Q, (h + 1) * Q)
            part_s[g, :, cs] = gates_t[col:col + 1] * o_cmp_t[0:DV, cs]

    for g in range(G):
        s = scores_w[g]
        e = jnp.exp2(s - jnp.max(s, axis=0, keepdims=True))
        pv_w = _dot(vwt_s[g, :, pl.ds(w0, W)], e.astype(BF16))
        o_win_t = pv_w[0:DV] / pv_w[DV:DV + 1]
        for h in range(HPG):
            col = g * HPG + h
            cs = slice(h * Q, (h + 1) * Q)
            part_s[g, :, cs] += gates_t[2 * NSA_HEADS + col:2 * NSA_HEADS + col + 1] * o_win_t[:, cs]

    for g in range(G):
        imp_t = jnp.dot(ovt_ref[...], psum_t[g], preferred_element_type=F32, precision=HIGHEST)
        jcol = lax.broadcasted_iota(jnp.int32, (nslc, 1), 0)
        cur = t_row // SLC_L
        forced = (jcol == 0) | (jcol == cur) | (jcol == cur - 1)
        imp_t = jnp.where(forced, FORCE_SCORE, imp_t)
        imp_t = jnp.where(jcol <= cur, imp_t, -1.0)
        nrow = nslc // SUBLANES
        rows = [imp_t[r * SUBLANES:(r + 1) * SUBLANES] for r in range(nrow)]
        ranks = [jnp.zeros((SUBLANES, Q), F32) for _ in range(nrow)]
        jloc = lax.broadcasted_iota(jnp.int32, (SUBLANES, Q), 0)
        for j2 in range(nslc):
            other = jnp.broadcast_to(imp_t[j2:j2 + 1, :], (SUBLANES, Q))
            for r in range(nrow):
                if (r + 1) * SUBLANES - 1 <= j2:
                    ahead = jnp.where(other > rows[r], 1.0, 0.0)
                elif r * SUBLANES > j2:
                    ahead = jnp.where(other >= rows[r], 1.0, 0.0)
                else:
                    ahead = jnp.where(jloc + r * SUBLANES > j2, jnp.where(other >= rows[r], 1.0, 0.0),
                                      jnp.where(other > rows[r], 1.0, 0.0))
                ranks[r] = ranks[r] + ahead
        ntop = float(min(SLC_TOP, nslc))
        bias_t = jnp.concatenate([jnp.where(rk < ntop, 0.0, -NSA_BIG) for rk in ranks]
                                 + [jnp.zeros((LANES - nslc, Q), F32)], axis=0)
        sel_bias = bias_t.T.astype(BF16)
        for h in range(HPG):
            qa_s[g, h * Q:(h + 1) * Q, LANES:2 * LANES] = sel_bias

    state0 = tuple((jnp.full((1, R), -NSA_BIG, F32), jnp.zeros((NSA_VROWS, R), F32)) for _ in range(G))

    def sel_tile(kt, state, causal):
        k0 = pl.multiple_of(kt * TK, TK)
        scores = [_dot_nt(ksa_s[g, pl.ds(k0, TK), :], qa_s[g]) for g in range(G)]
        if causal:
            pos = k0 + lax.broadcasted_iota(jnp.int32, (TK, 1), 0)
            bias = per_head(jnp.where(pos <= t_row, 0.0, -NSA_BIG))
            scores = [s + bias for s in scores]
        out = []
        for g in range(G):
            m_old, acc = state[g]
            m_new = jnp.maximum(m_old, jnp.max(scores[g], axis=0, keepdims=True))
            p = jnp.exp2(scores[g] - m_new).astype(BF16)
            acc = jnp.exp2(m_old - m_new) * acc + _dot(vst_s[g, :, pl.ds(k0, TK)], p)
            out.append((m_new, acc))
        return tuple(out)

    last = q0 // TK

    state = lax.fori_loop(0, last, lambda kt, st: sel_tile(kt, st, False), state0)
    state = sel_tile(last, state, True)

    for g in range(G):
        acc = state[g][1]
        o_sel_t = acc[0:DV] / acc[DV:DV + 1]
        outs = []
        for h in range(HPG):
            col = g * HPG + h
            cs = slice(h * Q, (h + 1) * Q)
            outs.append(part_s[g, :, cs] + gates_t[NSA_HEADS + col:NSA_HEADS + col + 1] * o_sel_t[:, cs])
        for h in range(0, HPG, 2):
            col = g * HPG + h
            o_ref[:, col * DV:(col + 2) * DV] = jnp.concatenate([outs[h], outs[h + 1]], axis=0).T.astype(o_ref.dtype)


def _bf16_split3(x):
    x = np.asarray(x, np.float32)
    rnd = lambda a: a.astype(BF16).astype(np.float32)
    hi = rnd(x)
    mid = rnd(x - hi)
    lo = rnd(x - hi - mid)
    return hi, mid, lo


def _pos_tail(pos):
    pos = np.asarray(pos)
    hi, lo = (pos // SLC_L) * SLC_L, pos % SLC_L
    tail = np.zeros((pos.shape[0], NSA_TAIL), np.float32)
    for c in range(3):
        tail[:, 2 * c] = hi
        tail[:, 2 * c + 1] = lo
    return tail


def nsa_attend(proj, k_cmp, v_cmp, B, S):
    Q, G, HPG = NSA_Q, NSA_GROUPS, NSA_HPG
    ncmp, nslc = S // CMP_STRIDE, S // SLC_L
    assert S >= WIN + Q and S % NSA_TK == 0 and nslc <= LANES and nslc % SUBLANES == 0
    n = np.arange(ncmp)[:, None]
    j = np.arange(nslc)[None, :]
    overlap = ((n * CMP_STRIDE < (j + 1) * SLC_L) & (n * CMP_STRIDE + CMP_L > j * SLC_L)
               & (n < (S - CMP_L) // CMP_STRIDE + 1)).astype(np.float32)
    slopes = np.asarray([_alibi_slope(h) for h in range(NSA_HEADS)], np.float32) * np.float32(LOG2E)
    qtail = np.zeros((NSA_HEADS, NSA_TAIL), np.float32)
    for c, part in enumerate(_bf16_split3(slopes)):
        qtail[:, 2 * c] = part
        qtail[:, 2 * c + 1] = part
    ktail = _pos_tail(np.arange(S))
    ctail = _pos_tail(np.arange(ncmp) * CMP_STRIDE + CMP_L - 1)
    vtail = np.zeros((S, NSA_TAIL), np.float32)
    vtail[:, 0] = 1.0
    kind = (np.arange(S)[:, None] // SLC_L == np.arange(LANES)[None, :]).astype(np.float32)
    qw = NSA_HEADS * NSA_DK
    kv0 = qw // LANES
    seqblk = lambda c: pl.BlockSpec((None, S, LANES), lambda b, i: (b, 0, c))
    cmpblk = pl.BlockSpec((None, G, ncmp, NSA_DK), lambda b, i: (b, 0, 0, 0))
    const = lambda a: pl.BlockSpec(a.shape, lambda b, i: (0, 0))
    consts = [jnp.asarray(overlap.T, F32), jnp.asarray(qtail, F32), jnp.asarray(ktail, BF16),
              jnp.asarray(ctail, BF16), jnp.asarray(vtail, BF16), jnp.asarray(kind, BF16)]
    return pl.pallas_call(
        functools.partial(_nsa_kernel, seq=S),
        grid=(B, S // Q),
        in_specs=[
            pl.BlockSpec((None, Q, qw), lambda b, i: (b, i, 0)),
            pl.BlockSpec((None, Q, LANES), lambda b, i: (b, i, kv0 + 6)),
            cmpblk, cmpblk,
            seqblk(kv0 + 2), seqblk(kv0 + 3), seqblk(kv0 + 4), seqblk(kv0 + 5),
        ] + [const(a) for a in consts],
        out_specs=pl.BlockSpec((None, Q, NSA_HEADS * NSA_DV), lambda b, i: (b, i, 0)),
        out_shape=jax.ShapeDtypeStruct((B, S, NSA_HEADS * NSA_DV), BF16),
        scratch_shapes=[
            pltpu.VMEM((G, HPG * Q, 2 * LANES), BF16),
            pltpu.VMEM((G, ncmp, LANES), BF16),
            pltpu.VMEM((G, NSA_VROWS, ncmp), BF16),
            pltpu.VMEM((G, S, 2 * LANES), BF16),
            pltpu.VMEM((G, NSA_VROWS, S), BF16),
            pltpu.VMEM((G, S, LANES), BF16),
            pltpu.VMEM((G, NSA_VROWS, S), BF16),
            pltpu.VMEM((G, NSA_DV, HPG * Q), F32),
        ],
        compiler_params=_cparams("parallel", "arbitrary"),
        name="nsa_attend",
    )(proj, proj, k_cmp, v_cmp, proj, proj, proj, proj, *consts)


def _pad_cols(w, n):
    return jnp.pad(w, ((0, 0), (0, n - w.shape[1])))


def kernel(x, norm_mix_pre, norm_mix_post, norm_mlp_pre, norm_mlp_post, mlp_w_up, mlp_w_down, ret_w_in, ret_gn, ret_w_out, gla_w_in, gla_w_gate_up, gla_b_gate, gla_gn, gla_w_out, lru_w_in, lru_conv_w, lru_conv_b, lru_w_a, lru_b_a, lru_w_x, lru_b_x, lru_lambda, lru_w_out, nsa_w_in, nsa_pe_k, nsa_w1_k, nsa_w2_k, nsa_pe_v, nsa_w1_v, nsa_w2_v, nsa_w_out):
    B, S, D = x.shape
    T = B * S
    depth = norm_mix_pre.shape[0]
    x2 = x.reshape(T, D)
    h2 = None
    for i in range(depth):
        m, j = i % 4, i // 4

        def project(w, w_extra=None):
            if h2 is not None:
                return in_proj(h2, w.astype(BF16), PROJ_TN, w_extra)
            assert w_extra is None
            return norm_proj(x2, norm_mix_pre[i], w.astype(BF16), PROJ_TN)

        if m == 0:
            proj = project(ret_w_in[j])
            mix = retention_mix(proj.reshape(B, S, -1), ret_gn[j], B, S)
            w_out = ret_w_out[j]
        elif m == 1:
            nmain = gla_w_in.shape[2] - GLA_RANK
            w_low = _pad_cols(gla_w_in[j][:, nmain:], LANES).astype(BF16)
            proj, gl = project(gla_w_in[j][:, :nmain], w_low)
            w_gate_pad = jnp.pad(gla_w_gate_up[j], ((0, LANES - GLA_RANK), (0, 0)))
            mix = gla_mix(proj.reshape(B, S, -1), gl.reshape(B, S, -1), w_gate_pad, gla_b_gate[j], gla_gn[j], B, S)
            w_out = gla_w_out[j]
        elif m == 2:
            proj = project(lru_w_in[j])
            mix = lru_mix(proj.reshape(B, S, -1), lru_conv_w[j], lru_conv_b[j], lru_w_a[j], lru_b_a[j],
                          lru_w_x[j], lru_b_x[j], lru_lambda[j], B, S)
            w_out = lru_w_out[j]
        else:
            npad = -(-nsa_w_in.shape[2] // PROJ_TN) * PROJ_TN
            proj = project(_pad_cols(nsa_w_in[j], npad)).reshape(B, S, -1)
            k_cmp, v_cmp = nsa_compress(proj, nsa_pe_k[j], nsa_w1_k[j], nsa_w2_k[j],
                                        nsa_pe_v[j], nsa_w1_v[j], nsa_w2_v[j], B, S)
            mix = nsa_attend(proj, k_cmp, v_cmp, B, S)
            w_out = nsa_w_out[j]
        g_next = norm_mix_pre[i + 1] if i + 1 < depth else None
        res = sublayer_tail(mix.reshape(T, -1), w_out.astype(BF16), norm_mix_post[i], x2, norm_mlp_pre[i],
                            mlp_w_up[i].astype(BF16), mlp_w_down[i].astype(BF16), norm_mlp_post[i], g_next)
        x2, h2 = res if g_next is not None else (res, None)
    return x2.reshape(B, S, D)
```

```python
import functools
import math

import numpy as np
import jax
import jax.numpy as jnp
from jax import lax
from jax.experimental import pallas as pl
from jax.experimental.pallas import tpu as pltpu

F32 = jnp.float32
BF16 = jnp.bfloat16
HIGHEST = lax.Precision.HIGHEST

D_MODEL = 1024
D_FF = 4 * D_MODEL
NORM_EPS = 1e-6
NEG_INF = -1e30

RET_HEADS, RET_DK, RET_DV = 4, 256, 512
RET_CHUNK = 256

GLA_HEADS, GLA_DK, GLA_DV = 4, 128, 256
GLA_RANK = 16
GLA_TAU = 16.0
GLA_CHUNK = 128

LRU_WIDTH = D_MODEL
LRU_BLOCKS = 8
LRU_BS = LRU_WIDTH // LRU_BLOCKS
CONV_W = 4
LRU_C = 8.0
LRU_TCHUNK = 128
LRU_HIST = 128
LRU_KEEP = 16
LRU_SLAB = LRU_TCHUNK + 8

NSA_HEADS, NSA_GROUPS, NSA_HPG = 16, 2, 8
NSA_DK = 64
NSA_DV = 64
CMP_L, CMP_STRIDE = 32, 16
CMP_HID = 2 * NSA_DK
SLC_L, SLC_TOP = 64, 16
WIN = 512
FORCE_SCORE = 1e6
NSA_Q = 256
NSA_TK = 512
NSA_TAIL = 64
NSA_VROWS = NSA_DV + 16
NSA_BIG = 1e30
LOG2E = math.log2(math.e)

LANES = 128
SUBLANES = 8
VMEM_LIMIT = 56 * 1024 * 1024


def _cparams(*sem):
    return pltpu.CompilerParams(dimension_semantics=sem, vmem_limit_bytes=VMEM_LIMIT)


def _rms(x, g):
    return x * lax.rsqrt(jnp.mean(x * x, axis=-1, keepdims=True) + NORM_EPS) * g


def _dot(a, b):
    return jnp.dot(a, b, preferred_element_type=F32)


def _dot_nt(a, b):
    return lax.dot_general(a, b, (((1,), (1,)), ((), ())), preferred_element_type=F32)


def _dot_tn(a, b):
    return lax.dot_general(a, b, (((0,), (0,)), ((), ())), preferred_element_type=F32)


def _gelu_tanh(x):
    return 0.5 * x * (1.0 + jnp.tanh(math.sqrt(2.0 / math.pi) * (x + 0.044715 * (x * x * x))))


def _sigmoid(x):
    return 0.5 * (jnp.tanh(0.5 * x) + 1.0)


PROJ_TN = 1024


def _norm_proj_kernel(x_ref, g_ref, w_ref, o_ref, h_ref):
    @pl.when(pl.program_id(1) == 0)
    def _():
        h_ref[...] = _rms(x_ref[...], g_ref[...]).astype(BF16)

    o_ref[...] = _dot(h_ref[...], w_ref[...]).astype(o_ref.dtype)


def norm_proj(x2d, g, w_bf16, tn, tm=1024):
    T, D = x2d.shape
    N = w_bf16.shape[1]
    assert T % tm == 0 and N % tn == 0
    return pl.pallas_call(
        _norm_proj_kernel,
        grid=(T // tm, N // tn),
        in_specs=[
            pl.BlockSpec((tm, D), lambda i, j: (i, 0)),
            pl.BlockSpec((1, D), lambda i, j: (0, 0)),
            pl.BlockSpec((D, tn), lambda i, j: (0, j)),
        ],
        out_specs=pl.BlockSpec((tm, tn), lambda i, j: (i, j)),
        out_shape=jax.ShapeDtypeStruct((T, N), BF16),
        scratch_shapes=[pltpu.VMEM((tm, D), BF16)],
        compiler_params=_cparams("parallel", "arbitrary"),
        name="norm_proj",
    )(x2d, g.reshape(1, D), w_bf16)


def _proj_kernel(h_ref, w_ref, o_ref):
    o_ref[...] = _dot(h_ref[...], w_ref[...]).astype(o_ref.dtype)


def _proj_extra_kernel(h_ref, w_ref, we_ref, o_ref, oe_ref):
    j = pl.program_id(1)

    @pl.when(j < pl.num_programs(1) - 1)
    def _():
        o_ref[...] = _dot(h_ref[...], w_ref[...]).astype(o_ref.dtype)

    @pl.when(j == pl.num_programs(1) - 1)
    def _():
        oe_ref[...] = _dot(h_ref[...], we_ref[...])


def in_proj(h2d, w_bf16, tn, w_extra=None, tm=1024):
    T, D = h2d.shape
    N = w_bf16.shape[1]
    assert T % tm == 0 and N % tn == 0
    nj = N // tn
    if w_extra is None:
        return pl.pallas_call(
            _proj_kernel,
            grid=(T // tm, nj),
            in_specs=[pl.BlockSpec((tm, D), lambda i, j: (i, 0)), pl.BlockSpec((D, tn), lambda i, j: (0, j))],
            out_specs=pl.BlockSpec((tm, tn), lambda i, j: (i, j)),
            out_shape=jax.ShapeDtypeStruct((T, N), BF16),
            compiler_params=_cparams("parallel", "parallel"),
            name="in_proj",
        )(h2d, w_bf16)
    ne = w_extra.shape[1]
    last = nj - 1
    return pl.pallas_call(
        _proj_extra_kernel,
        grid=(T // tm, nj + 1),
        in_specs=[
            pl.BlockSpec((tm, D), lambda i, j: (i, 0)),
            pl.BlockSpec((D, tn), lambda i, j: (0, jnp.minimum(j, last))),
            pl.BlockSpec((D, ne), lambda i, j: (0, 0)),
        ],
        out_specs=[pl.BlockSpec((tm, tn), lambda i, j: (i, jnp.minimum(j, last))),
                   pl.BlockSpec((tm, ne), lambda i, j: (i, 0))],
        out_shape=[jax.ShapeDtypeStruct((T, N), BF16), jax.ShapeDtypeStruct((T, ne), F32)],
        compiler_params=_cparams("parallel", "arbitrary"),
        name="in_proj_extra",
    )(h2d, w_bf16, w_extra)


MLP_FCHUNK = 512


def _tail_kernel(a_ref, wo_ref, gp_ref, x_ref, g1_ref, wu_ref, wd_ref, g2_ref, *rest):
    o_ref = rest[-2] if len(rest) == 3 else rest[0]
    x = x_ref[...] + _rms(_dot(a_ref[...], wo_ref[...]), gp_ref[...])
    h = _rms(x, g1_ref[...]).astype(BF16)
    acc = jnp.zeros(x.shape, F32)
    for c in range(D_FF // MLP_FCHUNK):
        sl = slice(c * MLP_FCHUNK, (c + 1) * MLP_FCHUNK)
        u = jnp.maximum(_dot(h, wu_ref[:, sl]), 0.0)
        acc = acc + _dot((u * u).astype(BF16), wd_ref[sl, :])
    out = x + _rms(acc, g2_ref[...])
    o_ref[...] = out
    if len(rest) == 3:
        gn_ref, _, hn_ref = rest
        hn_ref[...] = _rms(out, gn_ref[...]).astype(hn_ref.dtype)


def sublayer_tail(a2d, wo_bf16, g_post, x2d, g1, wu_bf16, wd_bf16, g2, g_next=None, tm=512):
    T, D = x2d.shape
    K = a2d.shape[1]
    F = wu_bf16.shape[1]
    resident = lambda shape: pl.BlockSpec(shape, lambda i: (0, 0), pipeline_mode=pl.Buffered(1))
    tile = pl.BlockSpec((tm, D), lambda i: (i, 0))
    row = lambda a: a.reshape(1, D)
    in_specs = [pl.BlockSpec((tm, K), lambda i: (i, 0)), resident((K, D)), resident((1, D)), tile,
                resident((1, D)), resident((D, F)), resident((F, D)), resident((1, D))]
    args = [a2d, wo_bf16, row(g_post), x2d, row(g1), wu_bf16, wd_bf16, row(g2)]
    out_specs, out_shape = tile, jax.ShapeDtypeStruct((T, D), F32)
    if g_next is not None:
        in_specs.append(resident((1, D)))
        args.append(row(g_next))
        out_specs, out_shape = [tile, tile], [out_shape, jax.ShapeDtypeStruct((T, D), BF16)]
    return pl.pallas_call(
        _tail_kernel,
        grid=(T // tm,),
        in_specs=in_specs,
        out_specs=out_specs,
        out_shape=out_shape,
        compiler_params=_cparams("parallel"),
        name="sublayer_tail",
    )(*args)


def _ret_kernel(q_ref, k_ref, v_ref, g_ref, gn_ref, di_ref, dq_ref, dk_ref, dc_ref, o_ref, st_ref):
    H, DK, DV = RET_HEADS, RET_DK, RET_DV

    @pl.when(pl.program_id(1) == 0)
    def _():
        st_ref[...] = jnp.zeros_like(st_ref)

    qb = [q_ref[:, h * DK:(h + 1) * DK] for h in range(H)]
    kb = [k_ref[:, h * DK:(h + 1) * DK] * (DK ** -0.5) for h in range(H)]
    vb = [v_ref[:, h * DV:(h + 1) * DV] for h in range(H)]
    st = [st_ref[h] for h in range(H)]
    scores = [_dot_nt(qb[h], kb[h]) for h in range(H)]
    inter = [_dot((qb[h].astype(F32) * dq_ref[h]).astype(BF16), st[h].astype(BF16)) for h in range(H)]
    for h in range(H):
        st_ref[h] = st[h] * dc_ref[h] + _dot_tn((kb[h].astype(F32) * dk_ref[h]).astype(BF16), vb[h])
    for h in range(H):
        o = _dot((scores[h] * di_ref[h]).astype(BF16), vb[h]) + inter[h]
        o = o - jnp.mean(o, axis=-1, keepdims=True)
        hs = slice(h * DV, (h + 1) * DV)
        gate = g_ref[:, hs].astype(F32)
        o_ref[:, hs] = (_rms(o, gn_ref[:, hs]) * (gate * _sigmoid(gate))).astype(o_ref.dtype)


def retention_mix(proj, gn, B, S):
    H, DK, DV, C = RET_HEADS, RET_DK, RET_DV, RET_CHUNK
    assert (DK ** -0.5) == 2.0 ** round(math.log2(DK ** -0.5))
    log_gamma = jnp.log1p(-jnp.exp2(-5.0 - jnp.arange(H, dtype=F32)))
    pos = jnp.arange(C, dtype=F32)
    diff = pos[:, None] - pos[None, :]
    d_intra = jnp.where(diff >= 0, jnp.exp(log_gamma[:, None, None] * jnp.maximum(diff, 0.0)), 0.0)
    d_q = jnp.exp(log_gamma[:, None] * (pos + 1.0))[:, :, None]
    d_k = jnp.exp(log_gamma[:, None] * (C - 1.0 - pos))[:, :, None]
    d_c = jnp.exp(log_gamma * C)[:, None, None]
    qk_w, v_w = H * DK, H * DV
    whole = lambda a: pl.BlockSpec(a.shape, lambda b, c: (0,) * a.ndim)
    return pl.pallas_call(
        _ret_kernel,
        grid=(B, S // C),
        in_specs=[
            pl.BlockSpec((None, C, qk_w), lambda b, c: (b, c, 0)),
            pl.BlockSpec((None, C, qk_w), lambda b, c: (b, c, 1)),
            pl.BlockSpec((None, C, v_w), lambda b, c: (b, c, (2 * qk_w) // v_w)),
            pl.BlockSpec((None, C, v_w), lambda b, c: (b, c, (2 * qk_w) // v_w + 1)),
            pl.BlockSpec((1, v_w), lambda b, c: (0, 0)),
            whole(d_intra), whole(d_q), whole(d_k), whole(d_c),
        ],
        out_specs=pl.BlockSpec((None, C, v_w), lambda b, c: (b, c, 0)),
        out_shape=jax.ShapeDtypeStruct((B, S, v_w), BF16),
        scratch_shapes=[pltpu.VMEM((H, DK, DV), F32)],
        compiler_params=_cparams("parallel", "arbitrary"),
        name="retention",
    )(proj, proj, proj, proj, gn.reshape(1, v_w), d_intra, d_q, d_k, d_c)


def _gla_level_sizes(C):
    sizes, s = [], 2
    while s <= C:
        sizes.append(s)
        s *= 2
    return sizes


def _gla_constants(C):
    t = np.arange(C)[:, None]
    r = np.arange(C)[None, :]
    mats = [(r <= t)]
    mats.append(r > t)
    masks = [np.eye(C, dtype=bool)]
    for bs in _gla_level_sizes(C):
        half = bs // 2
        mid = (t // bs) * bs + half - 1
        second = (t % bs) >= half
        m = np.where(second, (r > mid) & (r <= t), (r > t) & (r <= mid))
        mats.append(m)
        same = (t // bs) == (r // bs)
        masks.append(same & second & ((r % bs) < half))
    return (np.concatenate(mats, 0).astype(np.float32), np.stack(masks, 0).astype(np.float32))


def _gla_kernel(q_ref, k_ref, v_ref, r_ref, gl_ref, wg_ref, bg_ref, gn_ref, mall_ref, mask_ref, o_ref, st_ref):
    C, DK = GLA_CHUNK, GLA_DK

    H, DV = GLA_HEADS, GLA_DV
    W = H * DK

    @pl.when(pl.program_id(1) == 0)
    def _():
        st_ref[...] = jnp.zeros_like(st_ref)

    z = jnp.dot(gl_ref[...], wg_ref[...], preferred_element_type=F32, precision=HIGHEST) + bg_ref[...]
    la = (jnp.minimum(z, 0.0) - jnp.log1p(jnp.exp(-jnp.abs(z)))) * (1.0 / GLA_TAU)
    la_hi = la.astype(BF16)
    la_lo = (la - la_hi.astype(F32)).astype(BF16)
    sums = _dot(mall_ref[...], jnp.concatenate([la_hi, la_lo], axis=1))
    decay = jnp.exp(sums[:, :W] + sums[:, W:])
    e_b = decay[0:C]
    e_rest = decay[C:2 * C]
    e_last = decay[C - 1:C]

    q = q_ref[...].astype(F32) * (DK ** -0.5)
    k = k_ref[...].astype(F32)
    nlev = len(_gla_level_sizes(C))
    qe = [q.astype(BF16)] + [(q * decay[(2 + l) * C:(3 + l) * C]).astype(BF16) for l in range(nlev)]
    ke = [k_ref[...]] + [(k * decay[(2 + l) * C:(3 + l) * C]).astype(BF16) for l in range(nlev)]
    q_in = (q * e_b).astype(BF16)
    k_out = (k * e_rest).astype(BF16)
    for h in range(H):
        ks = slice(h * DK, (h + 1) * DK)
        vs = slice(h * DV, (h + 1) * DV)
        vb = v_ref[:, vs]
        a = _dot_nt(qe[0][:, ks], ke[0][:, ks]) * mask_ref[0]
        for l in range(nlev):
            a = a + _dot_nt(qe[1 + l][:, ks], ke[1 + l][:, ks]) * mask_ref[1 + l]
        st = st_ref[h]
        o = _dot(a.astype(BF16), vb) + _dot_nt(q_in[:, ks], st.astype(BF16))
        st_ref[h] = st * e_last[:, ks] + _dot_tn(vb, k_out[:, ks])
        gate = r_ref[:, vs].astype(F32)
        o_ref[:, vs] = (_rms(o, gn_ref[:, vs]) * (gate * _sigmoid(gate))).astype(o_ref.dtype)


def gla_mix(proj, gl, w_gate_pad, b_gate, gn, B, S):
    H, DK, DV, C = GLA_HEADS, GLA_DK, GLA_DV, GLA_CHUNK
    mall, masks = _gla_constants(C)
    qk_w, v_w = H * DK, H * DV
    whole = lambda shape: pl.BlockSpec(shape, lambda b, c: (0,) * len(shape))
    return pl.pallas_call(
        _gla_kernel,
        grid=(B, S // C),
        in_specs=[
            pl.BlockSpec((None, C, qk_w), lambda b, c: (b, c, 0)),
            pl.BlockSpec((None, C, qk_w), lambda b, c: (b, c, 1)),
            pl.BlockSpec((None, C, v_w), lambda b, c: (b, c, (2 * qk_w) // v_w)),
            pl.BlockSpec((None, C, v_w), lambda b, c: (b, c, (2 * qk_w) // v_w + 1)),
            pl.BlockSpec((None, C, LANES), lambda b, c: (b, c, 0)),
            whole((LANES, qk_w)), whole((1, qk_w)), whole((1, v_w)), whole(mall.shape), whole(masks.shape),
        ],
        out_specs=pl.BlockSpec((None, C, v_w), lambda b, c: (b, c, 0)),
        out_shape=jax.ShapeDtypeStruct((B, S, v_w), BF16),
        scratch_shapes=[pltpu.VMEM((H, DV, DK), F32)],
        compiler_params=_cparams("parallel", "arbitrary"),
        name="gla",
    )(proj, proj, proj, proj, gl, w_gate_pad, b_gate.reshape(1, H * DK), gn.reshape(1, H * DV),
      jnp.asarray(mall, BF16), jnp.asarray(masks, F32))


def _lru_kernel(x_ref, y_ref, sh_ref, cw_ref, cb_ref, wa_ref, ba_ref, wx_ref, bx_ref, lam_ref, o_ref,
                xbuf, xc_s, a_s, u_s, hcar):
    B, TC, W = x_ref.shape
    HIST, SLAB = LRU_HIST, LRU_SLAB

    @pl.when(pl.program_id(0) == 0)
    def _():
        xbuf[...] = jnp.zeros_like(xbuf)
        hcar[...] = jnp.zeros_like(hcar)

    for b in range(B):
        xbuf[b, HIST:HIST + TC, :] = x_ref[b]
        xe = xbuf[b]
        xc = cb_ref[...] + x_ref[b].astype(F32) * cw_ref[CONV_W - 1:CONV_W, :]
        for tap in range(CONV_W - 1):
            xc = xc + _dot(sh_ref[tap], xe) * cw_ref[tap:tap + 1, :]
        xc_s[b * TC:(b + 1) * TC, :] = xc
        xbuf[b, HIST - LRU_KEEP:HIST, :] = xbuf[b, HIST + TC - LRU_KEEP:HIST + TC, :]

    xc = xc_s[...]
    xcb = xc.astype(BF16)
    for n in range(LRU_BLOCKS):
        sl = slice(n * LRU_BS, (n + 1) * LRU_BS)
        xc_s[:, sl] = _dot(xcb[:, sl], wa_ref[n])
    gate_r = _sigmoid(xc_s[...] + ba_ref[...])
    for n in range(LRU_BLOCKS):
        sl = slice(n * LRU_BS, (n + 1) * LRU_BS)
        xc_s[:, sl] = _dot(xcb[:, sl], wx_ref[n])
    gate_i = _sigmoid(xc_s[...] + bx_ref[...])
    nl = -lam_ref[...]
    softplus = jnp.maximum(nl, 0.0) + jnp.log1p(jnp.exp(-jnp.abs(nl)))
    a = jnp.exp(-LRU_C * gate_r * softplus)
    u = jnp.sqrt(1.0 - a * a) * gate_i * xc
    NL = W // LANES
    for j in range(NL):
        for b in range(B):
            a_s[j, b * SLAB:b * SLAB + TC, :] = a[b * TC:(b + 1) * TC, j * LANES:(j + 1) * LANES]
            u_s[j, b * SLAB:b * SLAB + TC, :] = u[b * TC:(b + 1) * TC, j * LANES:(j + 1) * LANES]

    def body(t, hs):
        rows = pl.ds(t, B, stride=SLAB)
        out = []
        for j in range(NL):
            h = a_s[j, rows, :] * hs[j] + u_s[j, rows, :]
            u_s[j, rows, :] = h
            out.append(h)
        return tuple(out)

    h0 = tuple(hcar[:, j * LANES:(j + 1) * LANES] for j in range(NL))
    hT = lax.fori_loop(0, TC, body, h0, unroll=8)
    hcar[...] = jnp.concatenate(hT, axis=1)
    for b in range(B):
        hs = jnp.concatenate([u_s[j, b * SLAB:b * SLAB + TC, :] for j in range(NL)], axis=1)
        o_ref[b] = (hs * _gelu_tanh(y_ref[b].astype(F32))).astype(o_ref.dtype)


def lru_mix(proj, conv_w, conv_b, w_a, b_a, w_x, b_x, lam, B, S):
    TC, W, HIST, SLAB = LRU_TCHUNK, LRU_WIDTH, LRU_HIST, LRU_SLAB
    assert LRU_KEEP >= CONV_W - 1 and S % TC == 0
    t_idx = np.arange(TC)[:, None]
    c_idx = np.arange(HIST + TC)[None, :]
    shift = np.stack([(c_idx == HIST + t_idx - (CONV_W - 1 - tap)) for tap in range(CONV_W - 1)]).astype(np.float32)
    row = lambda a: a.reshape(1, W)
    vec = pl.BlockSpec((1, W), lambda t: (0, 0))
    blk = pl.BlockSpec((LRU_BLOCKS, LRU_BS, LRU_BS), lambda t: (0, 0, 0))
    return pl.pallas_call(
        _lru_kernel,
        grid=(S // TC,),
        in_specs=[
            pl.BlockSpec((B, TC, W), lambda t: (0, t, 0)),
            pl.BlockSpec((B, TC, W), lambda t: (0, t, 1)),
            pl.BlockSpec(shift.shape, lambda t: (0, 0, 0)),
            pl.BlockSpec((CONV_W, W), lambda t: (0, 0)),
            vec, blk, vec, blk, vec, vec,
        ],
        out_specs=pl.BlockSpec((B, TC, W), lambda t: (0, t, 0)),
        out_shape=jax.ShapeDtypeStruct((B, S, W), BF16),
        scratch_shapes=[
            pltpu.VMEM((B, HIST + TC, W), BF16),
            pltpu.VMEM((B * TC, W), F32),
            pltpu.VMEM((W // LANES, B * SLAB, LANES), F32),
            pltpu.VMEM((W // LANES, B * SLAB, LANES), F32),
            pltpu.VMEM((B, W), F32),
        ],
        compiler_params=_cparams("arbitrary"),
        name="rglru",
    )(proj, proj, jnp.asarray(shift, BF16), conv_w, row(conv_b), w_a.astype(BF16), row(b_a),
      w_x.astype(BF16), row(b_x), row(lam))


def _nsa_cmp_kernel(kc_ref, vc_ref, pek_ref, w1k_ref, w2k_ref, pev_ref, w1v_ref, w2v_ref, ko_ref, vo_ref,
                    buf, *, nblk):
    def compress(t_ref, pe_ref, w1_ref, w2_ref, o_ref, d):
        buf[...] = t_ref[...].astype(F32)
        ya = [jnp.zeros((nblk, CMP_HID), F32) for _ in range(NSA_GROUPS)]
        yb = [jnp.zeros((nblk, CMP_HID), F32) for _ in range(NSA_GROUPS)]
        for l in range(CMP_STRIDE):
            both = buf[pl.ds(l, nblk, stride=CMP_STRIDE), :]
            for g in range(NSA_GROUPS):
                rows = both[:, g * d:(g + 1) * d]
                lo = (rows + pe_ref[l:l + 1, :]).astype(BF16)
                hi = (rows + pe_ref[CMP_STRIDE + l:CMP_STRIDE + l + 1, :]).astype(BF16)
                ya[g] = ya[g] + _dot(lo, w1_ref[l * d:(l + 1) * d, :])
                yb[g] = yb[g] + _dot(hi, w1_ref[(CMP_STRIDE + l) * d:(CMP_STRIDE + l + 1) * d, :])
        for g in range(NSA_GROUPS):
            shifted = jnp.concatenate([yb[g][1:], jnp.zeros((1, CMP_HID), F32)], axis=0)
            hid = _gelu_tanh(ya[g] + shifted).astype(BF16)
            o_ref[g] = _dot(hid, w2_ref[...])

    compress(kc_ref, pek_ref, w1k_ref, w2k_ref, ko_ref, NSA_DK)
    compress(vc_ref, pev_ref, w1v_ref, w2v_ref, vo_ref, NSA_DV)


def nsa_compress(proj, pe_k, w1_k, w2_k, pe_v, w1_v, w2_v, B, S):
    nblk = S // CMP_STRIDE
    G = NSA_GROUPS
    kc_blk = (NSA_HEADS * NSA_DK) // LANES
    full2 = lambda a: pl.BlockSpec(a.shape, lambda b: (0, 0))
    w1k, w2k, w1v, w2v = (a.astype(BF16) for a in (w1_k, w2_k, w1_v, w2_v))
    out_sd = jax.ShapeDtypeStruct((B, G, nblk, NSA_DK), F32)
    return pl.pallas_call(
        functools.partial(_nsa_cmp_kernel, nblk=nblk),
        grid=(B,),
        in_specs=[
            pl.BlockSpec((None, S, LANES), lambda b: (b, 0, kc_blk)),
            pl.BlockSpec((None, S, LANES), lambda b: (b, 0, kc_blk + 1)),
            full2(pe_k), full2(w1k), full2(w2k), full2(pe_v), full2(w1v), full2(w2v),
        ],
        out_specs=[pl.BlockSpec((None, G, nblk, NSA_DK), lambda b: (b, 0, 0, 0))] * 2,
        out_shape=[out_sd, out_sd],
        scratch_shapes=[pltpu.VMEM((S, LANES), F32)],
        compiler_params=_cparams("parallel"),
        name="nsa_compress",
    )(proj, proj, pe_k, w1k, w2k, pe_v, w1v, w2v)


def _alibi_slope(head):
    return float(2.0 ** (-8.0 * (head + 1.0) / NSA_HEADS))


def _nsa_kernel(q_ref, gate_ref, kc_ref, vc_ref, ks_ref, vs_ref, kw_ref, vw_ref,
                ovt_ref, qt_ref, kt_ref, ct_ref, vt_ref, kind_ref, o_ref,
                qa_s, kca_s, vct_s, ksa_s, vst_s, kwa_s, vwt_s, part_s, *, seq):
    Q, TK, HPG, DK, DV, G = NSA_Q, NSA_TK, NSA_HPG, NSA_DK, NSA_DV, NSA_GROUPS
    R = HPG * Q
    W = WIN + Q
    ncmp = seq // CMP_STRIDE
    nslc = seq // SLC_L
    scale = DK ** -0.5
    qi = pl.program_id(1)
    q0 = qi * Q
    t_row = q0 + lax.broadcasted_iota(jnp.int32, (1, Q), 1)
    gates_t = _sigmoid(gate_ref[...].astype(F32)).T
    per_head = lambda a: jnp.concatenate([a] * HPG, axis=1)

    @pl.when(qi == 0)
    def _():
        for g in range(G):
            ks = slice(g * DK, (g + 1) * DK)
            vs = slice(g * DV, (g + 1) * DV)
            ksa_s[g, :, 0:LANES] = jnp.concatenate([ks_ref[:, ks], kt_ref[...]], axis=1)
            ksa_s[g, :, LANES:2 * LANES] = kind_ref[...]
            kwa_s[g] = jnp.concatenate([kw_ref[:, ks], kt_ref[...]], axis=1)
            tr = lambda a: a.astype(F32).T[0:NSA_VROWS].astype(BF16)
            vst_s[g] = tr(jnp.concatenate([vs_ref[:, vs], vt_ref[...]], axis=1))
            vwt_s[g] = tr(jnp.concatenate([vw_ref[:, vs], vt_ref[...]], axis=1))
            kca_s[g] = jnp.concatenate([kc_ref[g].astype(BF16), ct_ref[...]], axis=1)
            vct_s[g] = tr(jnp.concatenate([vc_ref[g], jnp.zeros((ncmp, NSA_TAIL), F32)], axis=1))

    for g in range(G):
        for h in range(HPG):
            col = g * HPG + h
            qh = q_ref[:, col * DK:(col + 1) * DK].astype(F32) * (scale * LOG2E)
            tail = jnp.broadcast_to(qt_ref[col:col + 1, :], (Q, NSA_TAIL))
            qa_s[g, h * Q:(h + 1) * Q, 0:LANES] = jnp.concatenate([qh, tail], axis=1).astype(BF16)

    cend = CMP_STRIDE * lax.broadcasted_iota(jnp.int32, (ncmp, 1), 0) + (CMP_L - 1)
    bias_c = per_head(jnp.where(cend <= t_row, 0.0, -NSA_BIG))
    scores_c = [_dot_nt(kca_s[g], qa_s[g, :, 0:LANES]) + bias_c for g in range(G)]
    psum_t = []
    for g in range(G):
        s = scores_c[g]
        e = jnp.exp2(s - jnp.max(s, axis=0, keepdims=True))
        inv = per_head(jnp.where(t_row >= CMP_L - 1, 1.0, 0.0)) / jnp.sum(e, axis=0, keepdims=True)
        p = e * inv
        ps = p[:, 0:Q]
        for h in range(1, HPG):
            ps = ps + p[:, h * Q:(h + 1) * Q]
        psum_t.append(ps)
        o_cmp_t = _dot(vct_s[g], p.astype(BF16))
        for h in range(HPG):
            col = g * HPG + h
            cs = slice(h * Q, (h + 1) * Q)
            part_s[g, :, cs] = gates_t[col:col + 1] * o_cmp_t[0:DV, cs]

    for g in range(G):
        imp_t = jnp.dot(ovt_ref[...], psum_t[g], preferred_element_type=F32, precision=HIGHEST)
        jcol = lax.broadcasted_iota(jnp.int32, (nslc, 1), 0)
        cur = t_row // SLC_L
        forced = (jcol == 0) | (jcol == cur) | (jcol == cur - 1)
        imp_t = jnp.where(forced, FORCE_SCORE, imp_t)
        imp_t = jnp.where(jcol <= cur, imp_t, -1.0)
        nrow = nslc // SUBLANES
        rows = [imp_t[r * SUBLANES:(r + 1) * SUBLANES] for r in range(nrow)]
        ranks = [jnp.zeros((SUBLANES, Q), F32) for _ in range(nrow)]
        jloc = lax.broadcasted_iota(jnp.int32, (SUBLANES, Q), 0)
        for j2 in range(nslc):
            other = jnp.broadcast_to(imp_t[j2:j2 + 1, :], (SUBLANES, Q))
            for r in range(nrow):
                if (r + 1) * SUBLANES - 1 <= j2:
                    ahead = jnp.where(other > rows[r], 1.0, 0.0)
                elif r * SUBLANES > j2:
                    ahead = jnp.where(other >= rows[r], 1.0, 0.0)
                else:
                    ahead = jnp.where(jloc + r * SUBLANES > j2, jnp.where(other >= rows[r], 1.0, 0.0),
                                      jnp.where(other > rows[r], 1.0, 0.0))
                ranks[r] = ranks[r] + ahead
        ntop = float(min(SLC_TOP, nslc))
        bias_t = jnp.concatenate([jnp.where(rk < ntop, 0.0, -NSA_BIG) for rk in ranks]
                                 + [jnp.zeros((LANES - nslc, Q), F32)], axis=0)
        sel_bias = bias_t.T.astype(BF16)
        for h in range(HPG):
            qa_s[g, h * Q:(h + 1) * Q, LANES:2 * LANES] = sel_bias

    state0 = tuple((jnp.full((1, R), -NSA_BIG, F32), jnp.zeros((NSA_VROWS, R), F32)) for _ in range(G))

    def sel_tile(k0, nk, state, causal):
        scores = [_dot_nt(ksa_s[g, pl.ds(k0, nk), :], qa_s[g]) for g in range(G)]
        if causal:
            pos = k0 + lax.broadcasted_iota(jnp.int32, (nk, 1), 0)
            bias = per_head(jnp.where(pos <= t_row, 0.0, -NSA_BIG))
            scores = [s + bias for s in scores]
        out = []
        for g in range(G):
            m_old, acc = state[g]
            m_new = jnp.maximum(m_old, jnp.max(scores[g], axis=0, keepdims=True))
            p = jnp.exp2(scores[g] - m_new).astype(BF16)
            acc = jnp.exp2(m_old - m_new) * acc + _dot(vst_s[g, :, pl.ds(k0, nk)], p)
            out.append((m_new, acc))
        return tuple(out)

    state = lax.fori_loop(0, q0 // TK, lambda kt, st: sel_tile(pl.multiple_of(kt * TK, TK), TK, st, False), state0)
    state = lax.cond(q0 % TK != 0, lambda st: sel_tile(pl.multiple_of(q0 - Q, Q), Q, st, False), lambda st: st,
                     state)
    state = sel_tile(pl.multiple_of(q0, Q), Q, state, True)

    w0 = pl.multiple_of(jnp.maximum(q0 - WIN, 0), Q)
    wpos = w0 + lax.broadcasted_iota(jnp.int32, (W, 1), 0)
    dist_w = t_row - wpos
    bias_w = per_head(jnp.where((dist_w >= 0) & (dist_w < WIN), 0.0, -NSA_BIG))
    scores_w = [_dot_nt(kwa_s[g, pl.ds(w0, W), :], qa_s[g, :, 0:LANES]) + bias_w for g in range(G)]
    for g in range(G):
        s = scores_w[g]
        e = jnp.exp2(s - jnp.max(s, axis=0, keepdims=True))
        pv_w = _dot(vwt_s[g, :, pl.ds(w0, W)], e.astype(BF16))
        o_win_t = pv_w[0:DV] / pv_w[DV:DV + 1]
        for h in range(HPG):
            col = g * HPG + h
            cs = slice(h * Q, (h + 1) * Q)
            part_s[g, :, cs] += gates_t[2 * NSA_HEADS + col:2 * NSA_HEADS + col + 1] * o_win_t[:, cs]

    for g in range(G):
        acc = state[g][1]
        o_sel_t = acc[0:DV] / acc[DV:DV + 1]
        outs = []
        for h in range(HPG):
            col = g * HPG + h
            cs = slice(h * Q, (h + 1) * Q)
            outs.append(part_s[g, :, cs] + gates_t[NSA_HEADS + col:NSA_HEADS + col + 1] * o_sel_t[:, cs])
        for h in range(0, HPG, 2):
            col = g * HPG + h
            o_ref[:, col * DV:(col + 2) * DV] = jnp.concatenate([outs[h], outs[h + 1]], axis=0).T.astype(o_ref.dtype)


def _bf16_split3(x):
    x = np.asarray(x, np.float32)
    rnd = lambda a: a.astype(BF16).astype(np.float32)
    hi = rnd(x)
    mid = rnd(x - hi)
    lo = rnd(x - hi - mid)
    return hi, mid, lo


def _pos_tail(pos):
    pos = np.asarray(pos)
    hi, lo = (pos // SLC_L) * SLC_L, pos % SLC_L
    tail = np.zeros((pos.shape[0], NSA_TAIL), np.float32)
    for c in range(3):
        tail[:, 2 * c] = hi
        tail[:, 2 * c + 1] = lo
    return tail


def nsa_attend(proj, k_cmp, v_cmp, B, S):
    Q, G, HPG = NSA_Q, NSA_GROUPS, NSA_HPG
    ncmp, nslc = S // CMP_STRIDE, S // SLC_L
    assert S >= WIN + Q and S % NSA_TK == 0 and nslc <= LANES and nslc % SUBLANES == 0 and NSA_TK == 2 * Q
    n = np.arange(ncmp)[:, None]
    j = np.arange(nslc)[None, :]
    overlap = ((n * CMP_STRIDE < (j + 1) * SLC_L) & (n * CMP_STRIDE + CMP_L > j * SLC_L)
               & (n < (S - CMP_L) // CMP_STRIDE + 1)).astype(np.float32)
    slopes = np.asarray([_alibi_slope(h) for h in range(NSA_HEADS)], np.float32) * np.float32(LOG2E)
    qtail = np.zeros((NSA_HEADS, NSA_TAIL), np.float32)
    for c, part in enumerate(_bf16_split3(slopes)):
        qtail[:, 2 * c] = part
        qtail[:, 2 * c + 1] = part
    ktail = _pos_tail(np.arange(S))
    ctail = _pos_tail(np.arange(ncmp) * CMP_STRIDE + CMP_L - 1)
    vtail = np.zeros((S, NSA_TAIL), np.float32)
    vtail[:, 0] = 1.0
    kind = (np.arange(S)[:, None] // SLC_L == np.arange(LANES)[None, :]).astype(np.float32)
    qw = NSA_HEADS * NSA_DK
    kv0 = qw // LANES
    seqblk = lambda c: pl.BlockSpec((None, S, LANES), lambda b, i: (b, 0, c))
    cmpblk = pl.BlockSpec((None, G, ncmp, NSA_DK), lambda b, i: (b, 0, 0, 0))
    const = lambda a: pl.BlockSpec(a.shape, lambda b, i: (0, 0))
    consts = [jnp.asarray(overlap.T, F32), jnp.asarray(qtail, F32), jnp.asarray(ktail, BF16),
              jnp.asarray(ctail, BF16), jnp.asarray(vtail, BF16), jnp.asarray(kind, BF16)]
    return pl.pallas_call(
        functools.partial(_nsa_kernel, seq=S),
        grid=(B, S // Q),
        in_specs=[
            pl.BlockSpec((None, Q, qw), lambda b, i: (b, i, 0)),
            pl.BlockSpec((None, Q, LANES), lambda b, i: (b, i, kv0 + 6)),
            cmpblk, cmpblk,
            seqblk(kv0 + 2), seqblk(kv0 + 3), seqblk(kv0 + 4), seqblk(kv0 + 5),
        ] + [const(a) for a in consts],
        out_specs=pl.BlockSpec((None, Q, NSA_HEADS * NSA_DV), lambda b, i: (b, i, 0)),
        out_shape=jax.ShapeDtypeStruct((B, S, NSA_HEADS * NSA_DV), BF16),
        scratch_shapes=[
            pltpu.VMEM((G, HPG * Q, 2 * LANES), BF16),
            pltpu.VMEM((G, ncmp, LANES), BF16),
            pltpu.VMEM((G, NSA_VROWS, ncmp), BF16),
            pltpu.VMEM((G, S, 2 * LANES), BF16),
            pltpu.VMEM((G, NSA_VROWS, S), BF16),
            pltpu.VMEM((G, S, LANES), BF16),
            pltpu.VMEM((G, NSA_VROWS, S), BF16),
            pltpu.VMEM((G, NSA_DV, HPG * Q), F32),
        ],
        compiler_params=_cparams("parallel", "arbitrary"),
        name="nsa_attend",
    )(proj, proj, k_cmp, v_cmp, proj, proj, proj, proj, *consts)


def _pad_cols(w, n):
    return jnp.pad(w, ((0, 0), (0, n - w.shape[1])))


def kernel(x, norm_mix_pre, norm_mix_post, norm_mlp_pre, norm_mlp_post, mlp_w_up, mlp_w_down, ret_w_in, ret_gn, ret_w_out, gla_w_in, gla_w_gate_up, gla_b_gate, gla_gn, gla_w_out, lru_w_in, lru_conv_w, lru_conv_b, lru_w_a, lru_b_a, lru_w_x, lru_b_x, lru_lambda, lru_w_out, nsa_w_in, nsa_pe_k, nsa_w1_k, nsa_w2_k, nsa_pe_v, nsa_w1_v, nsa_w2_v, nsa_w_out):
    B, S, D = x.shape
    T = B * S
    depth = norm_mix_pre.shape[0]
    x2 = x.reshape(T, D)
    h2 = None
    for i in range(depth):
        m, j = i % 4, i // 4

        def project(w, w_extra=None):
            if h2 is not None:
                return in_proj(h2, w.astype(BF16), PROJ_TN, w_extra)
            assert w_extra is None
            return norm_proj(x2, norm_mix_pre[i], w.astype(BF16), PROJ_TN)

        if m == 0:
            proj = project(ret_w_in[j])
            mix = retention_mix(proj.reshape(B, S, -1), ret_gn[j], B, S)
            w_out = ret_w_out[j]
        elif m == 1:
            nmain = gla_w_in.shape[2] - GLA_RANK
            w_low = _pad_cols(gla_w_in[j][:, nmain:], LANES).astype(BF16)
            proj, gl = project(gla_w_in[j][:, :nmain], w_low)
            w_gate_pad = jnp.pad(gla_w_gate_up[j], ((0, LANES - GLA_RANK), (0, 0)))
            mix = gla_mix(proj.reshape(B, S, -1), gl.reshape(B, S, -1), w_gate_pad, gla_b_gate[j], gla_gn[j], B, S)
            w_out = gla_w_out[j]
        elif m == 2:
            proj = project(lru_w_in[j])
            mix = lru_mix(proj.reshape(B, S, -1), lru_conv_w[j], lru_conv_b[j], lru_w_a[j], lru_b_a[j],
                          lru_w_x[j], lru_b_x[j], lru_lambda[j], B, S)
            w_out = lru_w_out[j]
        else:
            npad = -(-nsa_w_in.shape[2] // PROJ_TN) * PROJ_TN
            proj = project(_pad_cols(nsa_w_in[j], npad)).reshape(B, S, -1)
            k_cmp, v_cmp = nsa_compress(proj, nsa_pe_k[j], nsa_w1_k[j], nsa_w2_k[j],
                                        nsa_pe_v[j], nsa_w1_v[j], nsa_w2_v[j], B, S)
            mix = nsa_attend(proj, k_cmp, v_cmp, B, S)
            w_out = nsa_w_out[j]
        g_next = norm_mix_pre[i + 1] if i + 1 < depth else None
        res = sublayer_tail(mix.reshape(T, -1), w_out.astype(BF16), norm_mix_post[i], x2, norm_mlp_pre[i],
                            mlp_w_up[i].astype(BF16), mlp_w_down[i].astype(BF16), norm_mlp_post[i], g_next)
        x2, h2 = res if g_next is not None else (res, None)
    return x2.reshape(B, S, D)
```

```python
import functools
import math

import numpy as np
import jax
import jax.numpy as jnp
from jax import lax
from jax.experimental import pallas as pl
from jax.experimental.pallas import tpu as pltpu

F32 = jnp.float32
BF16 = jnp.bfloat16
HIGHEST = lax.Precision.HIGHEST

D_MODEL = 1024
D_FF = 4 * D_MODEL
NORM_EPS = 1e-6
NEG_INF = -1e30

RET_HEADS, RET_DK, RET_DV = 4, 256, 512
RET_CHUNK = 256

GLA_HEADS, GLA_DK, GLA_DV = 4, 128, 256
GLA_RANK = 16
GLA_TAU = 16.0
GLA_CHUNK = 128

LRU_WIDTH = D_MODEL
LRU_BLOCKS = 8
LRU_BS = LRU_WIDTH // LRU_BLOCKS
CONV_W = 4
LRU_C = 8.0
LRU_TCHUNK = 128
LRU_HIST = 128
LRU_KEEP = 16
LRU_SLAB = LRU_TCHUNK + 8

NSA_HEADS, NSA_GROUPS, NSA_HPG = 16, 2, 8
NSA_DK = 64
NSA_DV = 64
CMP_L, CMP_STRIDE = 32, 16
CMP_HID = 2 * NSA_DK
SLC_L, SLC_TOP = 64, 16
WIN = 512
FORCE_SCORE = 1e6
NSA_Q = 256
NSA_TK = 512
NSA_TAIL = 64
NSA_VROWS = NSA_DV + 16
NSA_BIG = 1e30
LOG2E = math.log2(math.e)

LANES = 128
SUBLANES = 8
VMEM_LIMIT = 56 * 1024 * 1024


def _cparams(*sem):
    return pltpu.CompilerParams(dimension_semantics=sem, vmem_limit_bytes=VMEM_LIMIT)


def _rms(x, g):
    return x * lax.rsqrt(jnp.mean(x * x, axis=-1, keepdims=True) + NORM_EPS) * g


def _dot(a, b):
    return jnp.dot(a, b, preferred_element_type=F32)


def _dot_nt(a, b):
    return lax.dot_general(a, b, (((1,), (1,)), ((), ())), preferred_element_type=F32)


def _dot_tn(a, b):
    return lax.dot_general(a, b, (((0,), (0,)), ((), ())), preferred_element_type=F32)


def _gelu_tanh(x):
    return 0.5 * x * (1.0 + jnp.tanh(math.sqrt(2.0 / math.pi) * (x + 0.044715 * (x * x * x))))


def _sigmoid(x):
    return 0.5 * (jnp.tanh(0.5 * x) + 1.0)


PROJ_TN_MAX = 3072


def _proj_tn(n):
    return max(t for t in range(LANES, min(n, PROJ_TN_MAX) + 1, LANES) if n % t == 0)


def _norm_proj_kernel(x_ref, g_ref, w_ref, o_ref, h_ref):
    @pl.when(pl.program_id(1) == 0)
    def _():
        h_ref[...] = _rms(x_ref[...], g_ref[...]).astype(BF16)

    o_ref[...] = _dot(h_ref[...], w_ref[...]).astype(o_ref.dtype)


def norm_proj(x2d, g, w_bf16, tn, tm=1024):
    T, D = x2d.shape
    N = w_bf16.shape[1]
    assert T % tm == 0 and N % tn == 0
    return pl.pallas_call(
        _norm_proj_kernel,
        grid=(T // tm, N // tn),
        in_specs=[
            pl.BlockSpec((tm, D), lambda i, j: (i, 0)),
            pl.BlockSpec((1, D), lambda i, j: (0, 0)),
            pl.BlockSpec((D, tn), lambda i, j: (0, j)),
        ],
        out_specs=pl.BlockSpec((tm, tn), lambda i, j: (i, j)),
        out_shape=jax.ShapeDtypeStruct((T, N), BF16),
        scratch_shapes=[pltpu.VMEM((tm, D), BF16)],
        compiler_params=_cparams("parallel", "arbitrary"),
        name="norm_proj",
    )(x2d, g.reshape(1, D), w_bf16)


def _proj_kernel(h_ref, w_ref, o_ref):
    o_ref[...] = _dot(h_ref[...], w_ref[...]).astype(o_ref.dtype)


def _proj_extra_kernel(h_ref, w_ref, we_ref, o_ref, oe_ref):
    j = pl.program_id(1)

    @pl.when(j < pl.num_programs(1) - 1)
    def _():
        o_ref[...] = _dot(h_ref[...], w_ref[...]).astype(o_ref.dtype)

    @pl.when(j == pl.num_programs(1) - 1)
    def _():
        oe_ref[...] = _dot(h_ref[...], we_ref[...])


def in_proj(h2d, w_bf16, tn, w_extra=None, tm=1024):
    T, D = h2d.shape
    N = w_bf16.shape[1]
    assert T % tm == 0 and N % tn == 0
    nj = N // tn
    if w_extra is None:
        return pl.pallas_call(
            _proj_kernel,
            grid=(T // tm, nj),
            in_specs=[pl.BlockSpec((tm, D), lambda i, j: (i, 0)), pl.BlockSpec((D, tn), lambda i, j: (0, j))],
            out_specs=pl.BlockSpec((tm, tn), lambda i, j: (i, j)),
            out_shape=jax.ShapeDtypeStruct((T, N), BF16),
            compiler_params=_cparams("parallel", "parallel"),
            name="in_proj",
        )(h2d, w_bf16)
    ne = w_extra.shape[1]
    last = nj - 1
    return pl.pallas_call(
        _proj_extra_kernel,
        grid=(T // tm, nj + 1),
        in_specs=[
            pl.BlockSpec((tm, D), lambda i, j: (i, 0)),
            pl.BlockSpec((D, tn), lambda i, j: (0, jnp.minimum(j, last))),
            pl.BlockSpec((D, ne), lambda i, j: (0, 0)),
        ],
        out_specs=[pl.BlockSpec((tm, tn), lambda i, j: (i, jnp.minimum(j, last))),
                   pl.BlockSpec((tm, ne), lambda i, j: (i, 0))],
        out_shape=[jax.ShapeDtypeStruct((T, N), BF16), jax.ShapeDtypeStruct((T, ne), F32)],
        compiler_params=_cparams("parallel", "arbitrary"),
        name="in_proj_extra",
    )(h2d, w_bf16, w_extra)


MLP_FCHUNK = 512


def _tail_kernel(a_ref, wo_ref, gp_ref, x_ref, g1_ref, wu_ref, wd_ref, g2_ref, *rest):
    o_ref = rest[-2] if len(rest) == 3 else rest[0]
    x = x_ref[...] + _rms(_dot(a_ref[...], wo_ref[...]), gp_ref[...])
    h = _rms(x, g1_ref[...]).astype(BF16)
    acc = jnp.zeros(x.shape, F32)
    for c in range(D_FF // MLP_FCHUNK):
        sl = slice(c * MLP_FCHUNK, (c + 1) * MLP_FCHUNK)
        u = jnp.maximum(_dot(h, wu_ref[:, sl]), 0.0)
        acc = acc + _dot((u * u).astype(BF16), wd_ref[sl, :])
    out = x + _rms(acc, g2_ref[...])
    o_ref[...] = out
    if len(rest) == 3:
        gn_ref, _, hn_ref = rest
        hn_ref[...] = _rms(out, gn_ref[...]).astype(hn_ref.dtype)


def sublayer_tail(a2d, wo_bf16, g_post, x2d, g1, wu_all, wd_all, layer, g2, g_next=None, tm=512):
    T, D = x2d.shape
    K = a2d.shape[1]
    F = wu_all.shape[2]
    resident = lambda shape: pl.BlockSpec(shape, lambda i: (0, 0), pipeline_mode=pl.Buffered(1))
    of_layer = lambda shape: pl.BlockSpec((None,) + shape, lambda i: (layer, 0, 0), pipeline_mode=pl.Buffered(1))
    tile = pl.BlockSpec((tm, D), lambda i: (i, 0))
    row = lambda a: a.reshape(1, D)
    in_specs = [pl.BlockSpec((tm, K), lambda i: (i, 0)), resident((K, D)), resident((1, D)), tile,
                resident((1, D)), of_layer((D, F)), of_layer((F, D)), resident((1, D))]
    args = [a2d, wo_bf16, row(g_post), x2d, row(g1), wu_all, wd_all, row(g2)]
    out_specs, out_shape = tile, jax.ShapeDtypeStruct((T, D), F32)
    if g_next is not None:
        in_specs.append(resident((1, D)))
        args.append(row(g_next))
        out_specs, out_shape = [tile, tile], [out_shape, jax.ShapeDtypeStruct((T, D), BF16)]
    return pl.pallas_call(
        _tail_kernel,
        grid=(T // tm,),
        in_specs=in_specs,
        out_specs=out_specs,
        out_shape=out_shape,
        compiler_params=_cparams("parallel"),
        name="sublayer_tail",
    )(*args)


def _ret_kernel(q_ref, k_ref, v_ref, g_ref, gn_ref, di_ref, dq_ref, dk_ref, dc_ref, o_ref, st_ref):
    H, DK, DV = RET_HEADS, RET_DK, RET_DV

    @pl.when(pl.program_id(1) == 0)
    def _():
        st_ref[...] = jnp.zeros_like(st_ref)

    qb = [q_ref[:, h * DK:(h + 1) * DK] for h in range(H)]
    kb = [k_ref[:, h * DK:(h + 1) * DK] * (DK ** -0.5) for h in range(H)]
    vb = [v_ref[:, h * DV:(h + 1) * DV] for h in range(H)]
    st = [st_ref[h] for h in range(H)]
    scores = [_dot_nt(qb[h], kb[h]) for h in range(H)]
    inter = [_dot((qb[h].astype(F32) * dq_ref[h]).astype(BF16), st[h].astype(BF16)) for h in range(H)]
    for h in range(H):
        st_ref[h] = st[h] * dc_ref[h] + _dot_tn((kb[h].astype(F32) * dk_ref[h]).astype(BF16), vb[h])
    for h in range(H):
        o = _dot((scores[h] * di_ref[h]).astype(BF16), vb[h]) + inter[h]
        o = o - jnp.mean(o, axis=-1, keepdims=True)
        hs = slice(h * DV, (h + 1) * DV)
        gate = g_ref[:, hs].astype(F32)
        o_ref[:, hs] = (_rms(o, gn_ref[:, hs]) * (gate * _sigmoid(gate))).astype(o_ref.dtype)


def retention_mix(proj, gn, B, S):
    H, DK, DV, C = RET_HEADS, RET_DK, RET_DV, RET_CHUNK
    assert (DK ** -0.5) == 2.0 ** round(math.log2(DK ** -0.5))
    log_gamma = jnp.log1p(-jnp.exp2(-5.0 - jnp.arange(H, dtype=F32)))
    pos = jnp.arange(C, dtype=F32)
    diff = pos[:, None] - pos[None, :]
    d_intra = jnp.where(diff >= 0, jnp.exp(log_gamma[:, None, None] * jnp.maximum(diff, 0.0)), 0.0)
    d_q = jnp.exp(log_gamma[:, None] * (pos + 1.0))[:, :, None]
    d_k = jnp.exp(log_gamma[:, None] * (C - 1.0 - pos))[:, :, None]
    d_c = jnp.exp(log_gamma * C)[:, None, None]
    qk_w, v_w = H * DK, H * DV
    whole = lambda a: pl.BlockSpec(a.shape, lambda b, c: (0,) * a.ndim)
    return pl.pallas_call(
        _ret_kernel,
        grid=(B, S // C),
        in_specs=[
            pl.BlockSpec((None, C, qk_w), lambda b, c: (b, c, 0)),
            pl.BlockSpec((None, C, qk_w), lambda b, c: (b, c, 1)),
            pl.BlockSpec((None, C, v_w), lambda b, c: (b, c, (2 * qk_w) // v_w)),
            pl.BlockSpec((None, C, v_w), lambda b, c: (b, c, (2 * qk_w) // v_w + 1)),
            pl.BlockSpec((1, v_w), lambda b, c: (0, 0)),
            whole(d_intra), whole(d_q), whole(d_k), whole(d_c),
        ],
        out_specs=pl.BlockSpec((None, C, v_w), lambda b, c: (b, c, 0)),
        out_shape=jax.ShapeDtypeStruct((B, S, v_w), BF16),
        scratch_shapes=[pltpu.VMEM((H, DK, DV), F32)],
        compiler_params=_cparams("parallel", "arbitrary"),
        name="retention",
    )(proj, proj, proj, proj, gn.reshape(1, v_w), d_intra, d_q, d_k, d_c)


def _gla_level_sizes(C):
    sizes, s = [], 2
    while s <= C:
        sizes.append(s)
        s *= 2
    return sizes


def _gla_constants(C):
    t = np.arange(C)[:, None]
    r = np.arange(C)[None, :]
    mats = [(r <= t)]
    mats.append(r > t)
    masks = [np.eye(C, dtype=bool)]
    for bs in _gla_level_sizes(C):
        half = bs // 2
        mid = (t // bs) * bs + half - 1
        second = (t % bs) >= half
        m = np.where(second, (r > mid) & (r <= t), (r > t) & (r <= mid))
        mats.append(m)
        same = (t // bs) == (r // bs)
        masks.append(same & second & ((r % bs) < half))
    return (np.concatenate(mats, 0).astype(np.float32), np.stack(masks, 0).astype(np.float32))


def _gla_kernel(q_ref, k_ref, v_ref, r_ref, gl_ref, wg_ref, bg_ref, gn_ref, mall_ref, mask_ref, o_ref, st_ref):
    C, DK = GLA_CHUNK, GLA_DK

    H, DV = GLA_HEADS, GLA_DV
    W = H * DK

    @pl.when(pl.program_id(1) == 0)
    def _():
        st_ref[...] = jnp.zeros_like(st_ref)

    z = jnp.dot(gl_ref[...], wg_ref[...], preferred_element_type=F32, precision=HIGHEST) + bg_ref[...]
    la = (jnp.minimum(z, 0.0) - jnp.log1p(jnp.exp(-jnp.abs(z)))) * (1.0 / GLA_TAU)
    la_hi = la.astype(BF16)
    la_lo = (la - la_hi.astype(F32)).astype(BF16)
    cum = _dot(mall_ref[0:2 * C, :], jnp.concatenate([la_hi, la_lo], axis=1))
    decay_cum = jnp.exp(cum[:, :W] + cum[:, W:])
    decay = jnp.exp(_dot(mall_ref[2 * C:, :], la_hi))
    e_b = decay_cum[0:C]
    e_rest = decay_cum[C:2 * C]
    e_last = decay_cum[C - 1:C]

    q = q_ref[...].astype(F32) * (DK ** -0.5)
    k = k_ref[...].astype(F32)
    nlev = len(_gla_level_sizes(C))
    qe = [q.astype(BF16)] + [(q * decay[l * C:(l + 1) * C]).astype(BF16) for l in range(nlev)]
    ke = [k_ref[...]] + [(k * decay[l * C:(l + 1) * C]).astype(BF16) for l in range(nlev)]
    q_in = (q * e_b).astype(BF16)
    k_out = (k * e_rest).astype(BF16)
    for h in range(H):
        ks = slice(h * DK, (h + 1) * DK)
        vs = slice(h * DV, (h + 1) * DV)
        vb = v_ref[:, vs]
        a = _dot_nt(qe[0][:, ks], ke[0][:, ks]) * mask_ref[0]
        for l in range(nlev):
            a = a + _dot_nt(qe[1 + l][:, ks], ke[1 + l][:, ks]) * mask_ref[1 + l]
        st = st_ref[h]
        o = _dot(a.astype(BF16), vb) + _dot_nt(q_in[:, ks], st.astype(BF16))
        st_ref[h] = st * e_last[:, ks] + _dot_tn(vb, k_out[:, ks])
        gate = r_ref[:, vs].astype(F32)
        o_ref[:, vs] = (_rms(o, gn_ref[:, vs]) * (gate * _sigmoid(gate))).astype(o_ref.dtype)


def gla_mix(proj, gl, w_gate_pad, b_gate, gn, B, S):
    H, DK, DV, C = GLA_HEADS, GLA_DK, GLA_DV, GLA_CHUNK
    mall, masks = _gla_constants(C)
    qk_w, v_w = H * DK, H * DV
    whole = lambda shape: pl.BlockSpec(shape, lambda b, c: (0,) * len(shape))
    return pl.pallas_call(
        _gla_kernel,
        grid=(B, S // C),
        in_specs=[
            pl.BlockSpec((None, C, qk_w), lambda b, c: (b, c, 0)),
            pl.BlockSpec((None, C, qk_w), lambda b, c: (b, c, 1)),
            pl.BlockSpec((None, C, v_w), lambda b, c: (b, c, (2 * qk_w) // v_w)),
            pl.BlockSpec((None, C, v_w), lambda b, c: (b, c, (2 * qk_w) // v_w + 1)),
            pl.BlockSpec((None, C, LANES), lambda b, c: (b, c, 0)),
            whole((LANES, qk_w)), whole((1, qk_w)), whole((1, v_w)), whole(mall.shape), whole(masks.shape),
        ],
        out_specs=pl.BlockSpec((None, C, v_w), lambda b, c: (b, c, 0)),
        out_shape=jax.ShapeDtypeStruct((B, S, v_w), BF16),
        scratch_shapes=[pltpu.VMEM((H, DV, DK), F32)],
        compiler_params=_cparams("parallel", "arbitrary"),
        name="gla",
    )(proj, proj, proj, proj, gl, w_gate_pad, b_gate.reshape(1, H * DK), gn.reshape(1, H * DV),
      jnp.asarray(mall, BF16), jnp.asarray(masks, F32))


def _lru_kernel(x_ref, y_ref, sh_ref, cw_ref, cb_ref, wa_ref, ba_ref, wx_ref, bx_ref, lam_ref, o_ref,
                xbuf, xc_s, a_s, u_s, hcar):
    B, TC, W = x_ref.shape
    HIST, SLAB = LRU_HIST, LRU_SLAB

    @pl.when(pl.program_id(0) == 0)
    def _():
        xbuf[...] = jnp.zeros_like(xbuf)
        hcar[...] = jnp.zeros_like(hcar)

    for b in range(B):
        xbuf[b, HIST:HIST + TC, :] = x_ref[b]
        xe = xbuf[b]
        xc = cb_ref[...] + x_ref[b].astype(F32) * cw_ref[CONV_W - 1:CONV_W, :]
        for tap in range(CONV_W - 1):
            xc = xc + _dot(sh_ref[tap], xe) * cw_ref[tap:tap + 1, :]
        xc_s[b * TC:(b + 1) * TC, :] = xc
        xbuf[b, HIST - LRU_KEEP:HIST, :] = xbuf[b, HIST + TC - LRU_KEEP:HIST + TC, :]

    xc = xc_s[...]
    xcb = xc.astype(BF16)
    for n in range(LRU_BLOCKS):
        sl = slice(n * LRU_BS, (n + 1) * LRU_BS)
        xc_s[:, sl] = _dot(xcb[:, sl], wa_ref[n])
    gate_r = _sigmoid(xc_s[...] + ba_ref[...])
    for n in range(LRU_BLOCKS):
        sl = slice(n * LRU_BS, (n + 1) * LRU_BS)
        xc_s[:, sl] = _dot(xcb[:, sl], wx_ref[n])
    gate_i = _sigmoid(xc_s[...] + bx_ref[...])
    nl = -lam_ref[...]
    softplus = jnp.maximum(nl, 0.0) + jnp.log1p(jnp.exp(-jnp.abs(nl)))
    a = jnp.exp(-LRU_C * gate_r * softplus)
    u = jnp.sqrt(1.0 - a * a) * gate_i * xc
    NL = W // LANES
    for j in range(NL):
        for b in range(B):
            a_s[j, b * SLAB:b * SLAB + TC, :] = a[b * TC:(b + 1) * TC, j * LANES:(j + 1) * LANES]
            u_s[j, b * SLAB:b * SLAB + TC, :] = u[b * TC:(b + 1) * TC, j * LANES:(j + 1) * LANES]

    def body(t, hs):
        rows = pl.ds(t, B, stride=SLAB)
        out = []
        for j in range(NL):
            h = a_s[j, rows, :] * hs[j] + u_s[j, rows, :]
            u_s[j, rows, :] = h
            out.append(h)
        return tuple(out)

    h0 = tuple(hcar[:, j * LANES:(j + 1) * LANES] for j in range(NL))
    hT = lax.fori_loop(0, TC, body, h0, unroll=8)
    hcar[...] = jnp.concatenate(hT, axis=1)
    for b in range(B):
        hs = jnp.concatenate([u_s[j, b * SLAB:b * SLAB + TC, :] for j in range(NL)], axis=1)
        o_ref[b] = (hs * _gelu_tanh(y_ref[b].astype(F32))).astype(o_ref.dtype)


def lru_mix(proj, conv_w, conv_b, w_a, b_a, w_x, b_x, lam, B, S):
    TC, W, HIST, SLAB = LRU_TCHUNK, LRU_WIDTH, LRU_HIST, LRU_SLAB
    assert LRU_KEEP >= CONV_W - 1 and S % TC == 0
    t_idx = np.arange(TC)[:, None]
    c_idx = np.arange(HIST + TC)[None, :]
    shift = np.stack([(c_idx == HIST + t_idx - (CONV_W - 1 - tap)) for tap in range(CONV_W - 1)]).astype(np.float32)
    row = lambda a: a.reshape(1, W)
    vec = pl.BlockSpec((1, W), lambda t: (0, 0))
    blk = pl.BlockSpec((LRU_BLOCKS, LRU_BS, LRU_BS), lambda t: (0, 0, 0))
    return pl.pallas_call(
        _lru_kernel,
        grid=(S // TC,),
        in_specs=[
            pl.BlockSpec((B, TC, W), lambda t: (0, t, 0)),
            pl.BlockSpec((B, TC, W), lambda t: (0, t, 1)),
            pl.BlockSpec(shift.shape, lambda t: (0, 0, 0)),
            pl.BlockSpec((CONV_W, W), lambda t: (0, 0)),
            vec, blk, vec, blk, vec, vec,
        ],
        out_specs=pl.BlockSpec((B, TC, W), lambda t: (0, t, 0)),
        out_shape=jax.ShapeDtypeStruct((B, S, W), BF16),
        scratch_shapes=[
            pltpu.VMEM((B, HIST + TC, W), BF16),
            pltpu.VMEM((B * TC, W), F32),
            pltpu.VMEM((W // LANES, B * SLAB, LANES), F32),
            pltpu.VMEM((W // LANES, B * SLAB, LANES), F32),
            pltpu.VMEM((B, W), F32),
        ],
        compiler_params=_cparams("arbitrary"),
        name="rglru",
    )(proj, proj, jnp.asarray(shift, BF16), conv_w, row(conv_b), w_a.astype(BF16), row(b_a),
      w_x.astype(BF16), row(b_x), row(lam))


def _nsa_cmp_kernel(kc_ref, vc_ref, pek_ref, w1k_ref, w2k_ref, pev_ref, w1v_ref, w2v_ref, ko_ref, vo_ref,
                    buf, *, nblk):
    def compress(t_ref, pe_ref, w1_ref, w2_ref, o_ref, d):
        buf[...] = t_ref[...].astype(F32)
        ya = [jnp.zeros((nblk, CMP_HID), F32) for _ in range(NSA_GROUPS)]
        yb = [jnp.zeros((nblk, CMP_HID), F32) for _ in range(NSA_GROUPS)]
        for l in range(CMP_STRIDE):
            both = buf[pl.ds(l, nblk, stride=CMP_STRIDE), :]
            for g in range(NSA_GROUPS):
                rows = both[:, g * d:(g + 1) * d]
                lo = (rows + pe_ref[l:l + 1, :]).astype(BF16)
                hi = (rows + pe_ref[CMP_STRIDE + l:CMP_STRIDE + l + 1, :]).astype(BF16)
                ya[g] = ya[g] + _dot(lo, w1_ref[l * d:(l + 1) * d, :])
                yb[g] = yb[g] + _dot(hi, w1_ref[(CMP_STRIDE + l) * d:(CMP_STRIDE + l + 1) * d, :])
        for g in range(NSA_GROUPS):
            shifted = jnp.concatenate([yb[g][1:], jnp.zeros((1, CMP_HID), F32)], axis=0)
            hid = _gelu_tanh(ya[g] + shifted).astype(BF16)
            o_ref[g] = _dot(hid, w2_ref[...])

    compress(kc_ref, pek_ref, w1k_ref, w2k_ref, ko_ref, NSA_DK)
    compress(vc_ref, pev_ref, w1v_ref, w2v_ref, vo_ref, NSA_DV)


def nsa_compress(proj, pe_k, w1_k, w2_k, pe_v, w1_v, w2_v, B, S):
    nblk = S // CMP_STRIDE
    G = NSA_GROUPS
    kc_blk = (NSA_HEADS * NSA_DK) // LANES
    full2 = lambda a: pl.BlockSpec(a.shape, lambda b: (0, 0))
    w1k, w2k, w1v, w2v = (a.astype(BF16) for a in (w1_k, w2_k, w1_v, w2_v))
    out_sd = jax.ShapeDtypeStruct((B, G, nblk, NSA_DK), F32)
    return pl.pallas_call(
        functools.partial(_nsa_cmp_kernel, nblk=nblk),
        grid=(B,),
        in_specs=[
            pl.BlockSpec((None, S, LANES), lambda b: (b, 0, kc_blk)),
            pl.BlockSpec((None, S, LANES), lambda b: (b, 0, kc_blk + 1)),
            full2(pe_k), full2(w1k), full2(w2k), full2(pe_v), full2(w1v), full2(w2v),
        ],
        out_specs=[pl.BlockSpec((None, G, nblk, NSA_DK), lambda b: (b, 0, 0, 0))] * 2,
        out_shape=[out_sd, out_sd],
        scratch_shapes=[pltpu.VMEM((S, LANES), F32)],
        compiler_params=_cparams("parallel"),
        name="nsa_compress",
    )(proj, proj, pe_k, w1k, w2k, pe_v, w1v, w2v)


def _alibi_slope(head):
    return float(2.0 ** (-8.0 * (head + 1.0) / NSA_HEADS))


def _nsa_kernel(q_ref, gate_ref, kc_ref, vc_ref, ks_ref, vs_ref, kw_ref, vw_ref,
                ovt_ref, qt_ref, kt_ref, ct_ref, vt_ref, kind_ref, o_ref,
                qa_s, kca_s, vct_s, ksa_s, vst_s, kwa_s, vwt_s, part_s, *, seq):
    Q, TK, HPG, DK, DV, G = NSA_Q, NSA_TK, NSA_HPG, NSA_DK, NSA_DV, NSA_GROUPS
    R = HPG * Q
    W = WIN + Q
    ncmp = seq // CMP_STRIDE
    nslc = seq // SLC_L
    scale = DK ** -0.5
    qi = pl.program_id(1)
    q0 = qi * Q
    t_row = q0 + lax.broadcasted_iota(jnp.int32, (1, Q), 1)
    gates_t = _sigmoid(gate_ref[...].astype(F32)).T
    per_head = lambda a: jnp.concatenate([a] * HPG, axis=1)

    @pl.when(qi == 0)
    def _():
        for g in range(G):
            ks = slice(g * DK, (g + 1) * DK)
            vs = slice(g * DV, (g + 1) * DV)
            ksa_s[g, :, 0:LANES] = jnp.concatenate([ks_ref[:, ks], kt_ref[...]], axis=1)
            ksa_s[g, :, LANES:2 * LANES] = kind_ref[...]
            kwa_s[g] = jnp.concatenate([kw_ref[:, ks], kt_ref[...]], axis=1)
            tr = lambda a: a.astype(F32).T[0:NSA_VROWS].astype(BF16)
            vst_s[g] = tr(jnp.concatenate([vs_ref[:, vs], vt_ref[...]], axis=1))
            vwt_s[g] = tr(jnp.concatenate([vw_ref[:, vs], vt_ref[...]], axis=1))
            kca_s[g] = jnp.concatenate([kc_ref[g].astype(BF16), ct_ref[...]], axis=1)
            vct_s[g] = tr(jnp.concatenate([vc_ref[g], jnp.zeros((ncmp, NSA_TAIL), F32)], axis=1))

    for g in range(G):
        for h in range(HPG):
            col = g * HPG + h
            qh = q_ref[:, col * DK:(col + 1) * DK].astype(F32) * (scale * LOG2E)
            tail = jnp.broadcast_to(qt_ref[col:col + 1, :], (Q, NSA_TAIL))
            qa_s[g, h * Q:(h + 1) * Q, 0:LANES] = jnp.concatenate([qh, tail], axis=1).astype(BF16)

    cend = CMP_STRIDE * lax.broadcasted_iota(jnp.int32, (ncmp, 1), 0) + (CMP_L - 1)
    bias_c = per_head(jnp.where(cend <= t_row, 0.0, -NSA_BIG))
    scores_c = [_dot_nt(kca_s[g], qa_s[g, :, 0:LANES]) + bias_c for g in range(G)]
    psum_t = []
    for g in range(G):
        s = scores_c[g]
        e = jnp.exp2(s - jnp.max(s, axis=0, keepdims=True))
        inv = per_head(jnp.where(t_row >= CMP_L - 1, 1.0, 0.0)) / jnp.sum(e, axis=0, keepdims=True)
        p = e * inv
        ps = p[:, 0:Q]
        for h in range(1, HPG):
            ps = ps + p[:, h * Q:(h + 1) * Q]
        psum_t.append(ps)
        o_cmp_t = _dot(vct_s[g], p.astype(BF16))
        for h in range(HPG):
            col = g * HPG + h
            cs = slice(h * Q, (h + 1) * Q)
            part_s[g, :, cs] = gates_t[col:col + 1] * o_cmp_t[0:DV, cs]

    for g in range(G):
        imp_t = jnp.dot(ovt_ref[...], psum_t[g], preferred_element_type=F32, precision=HIGHEST)
        jcol = lax.broadcasted_iota(jnp.int32, (nslc, 1), 0)
        cur = t_row // SLC_L
        forced = (jcol == 0) | (jcol == cur) | (jcol == cur - 1)
        imp_t = jnp.where(forced, FORCE_SCORE, imp_t)
        imp_t = jnp.where(jcol <= cur, imp_t, -1.0)
        nrow = nslc // SUBLANES
        rows = [imp_t[r * SUBLANES:(r + 1) * SUBLANES] for r in range(nrow)]
        ranks = [jnp.zeros((SUBLANES, Q), F32) for _ in range(nrow)]
        jloc = lax.broadcasted_iota(jnp.int32, (SUBLANES, Q), 0)
        for j2 in range(nslc):
            other = jnp.broadcast_to(imp_t[j2:j2 + 1, :], (SUBLANES, Q))
            for r in range(nrow):
                if (r + 1) * SUBLANES - 1 <= j2:
                    ahead = jnp.where(other > rows[r], 1.0, 0.0)
                elif r * SUBLANES > j2:
                    ahead = jnp.where(other >= rows[r], 1.0, 0.0)
                else:
                    ahead = jnp.where(jloc + r * SUBLANES > j2, jnp.where(other >= rows[r], 1.0, 0.0),
                                      jnp.where(other > rows[r], 1.0, 0.0))
                ranks[r] = ranks[r] + ahead
        ntop = float(min(SLC_TOP, nslc))
        bias_t = jnp.concatenate([jnp.where(rk < ntop, 0.0, -NSA_BIG) for rk in ranks]
                                 + [jnp.zeros((LANES - nslc, Q), F32)], axis=0)
        sel_bias = bias_t.T.astype(BF16)
        for h in range(HPG):
            qa_s[g, h * Q:(h + 1) * Q, LANES:2 * LANES] = sel_bias

    state0 = tuple((jnp.full((1, R), -NSA_BIG, F32), jnp.zeros((NSA_VROWS, R), F32)) for _ in range(G))

    def sel_tile(k0, nk, state, causal):
        scores = [_dot_nt(ksa_s[g, pl.ds(k0, nk), :], qa_s[g]) for g in range(G)]
        if causal:
            pos = k0 + lax.broadcasted_iota(jnp.int32, (nk, 1), 0)
            bias = per_head(jnp.where(pos <= t_row, 0.0, -NSA_BIG))
            scores = [s + bias for s in scores]
        out = []
        for g in range(G):
            m_old, acc = state[g]
            m_new = jnp.maximum(m_old, jnp.max(scores[g], axis=0, keepdims=True))
            p = jnp.exp2(scores[g] - m_new).astype(BF16)
            acc = jnp.exp2(m_old - m_new) * acc + _dot(vst_s[g, :, pl.ds(k0, nk)], p)
            out.append((m_new, acc))
        return tuple(out)

    state = lax.fori_loop(0, q0 // TK, lambda kt, st: sel_tile(pl.multiple_of(kt * TK, TK), TK, st, False), state0)
    state = lax.cond(q0 % TK != 0, lambda st: sel_tile(pl.multiple_of(q0 - Q, Q), Q, st, False), lambda st: st,
                     state)
    state = sel_tile(pl.multiple_of(q0, Q), Q, state, True)

    w0 = pl.multiple_of(jnp.maximum(q0 - WIN, 0), Q)
    wpos = w0 + lax.broadcasted_iota(jnp.int32, (W, 1), 0)
    dist_w = t_row - wpos
    bias_w = per_head(jnp.where((dist_w >= 0) & (dist_w < WIN), 0.0, -NSA_BIG))
    scores_w = [_dot_nt(kwa_s[g, pl.ds(w0, W), :], qa_s[g, :, 0:LANES]) + bias_w for g in range(G)]
    for g in range(G):
        s = scores_w[g]
        e = jnp.exp2(s - jnp.max(s, axis=0, keepdims=True))
        pv_w = _dot(vwt_s[g, :, pl.ds(w0, W)], e.astype(BF16))
        o_win_t = pv_w[0:DV] / pv_w[DV:DV + 1]
        for h in range(HPG):
            col = g * HPG + h
            cs = slice(h * Q, (h + 1) * Q)
            part_s[g, :, cs] += gates_t[2 * NSA_HEADS + col:2 * NSA_HEADS + col + 1] * o_win_t[:, cs]

    for g in range(G):
        acc = state[g][1]
        o_sel_t = acc[0:DV] / acc[DV:DV + 1]
        outs = []
        for h in range(HPG):
            col = g * HPG + h
            cs = slice(h * Q, (h + 1) * Q)
            outs.append(part_s[g, :, cs] + gates_t[NSA_HEADS + col:NSA_HEADS + col + 1] * o_sel_t[:, cs])
        for h in range(0, HPG, 2):
            col = g * HPG + h
            o_ref[:, col * DV:(col + 2) * DV] = jnp.concatenate([outs[h], outs[h + 1]], axis=0).T.astype(o_ref.dtype)


def _bf16_split3(x):
    x = np.asarray(x, np.float32)
    rnd = lambda a: a.astype(BF16).astype(np.float32)
    hi = rnd(x)
    mid = rnd(x - hi)
    lo = rnd(x - hi - mid)
    return hi, mid, lo


def _pos_tail(pos):
    pos = np.asarray(pos)
    hi, lo = (pos // SLC_L) * SLC_L, pos % SLC_L
    tail = np.zeros((pos.shape[0], NSA_TAIL), np.float32)
    for c in range(3):
        tail[:, 2 * c] = hi
        tail[:, 2 * c + 1] = lo
    return tail


def nsa_attend(proj, k_cmp, v_cmp, B, S):
    Q, G, HPG = NSA_Q, NSA_GROUPS, NSA_HPG
    ncmp, nslc = S // CMP_STRIDE, S // SLC_L
    assert S >= WIN + Q and S % NSA_TK == 0 and nslc <= LANES and nslc % SUBLANES == 0 and NSA_TK == 2 * Q
    n = np.arange(ncmp)[:, None]
    j = np.arange(nslc)[None, :]
    overlap = ((n * CMP_STRIDE < (j + 1) * SLC_L) & (n * CMP_STRIDE + CMP_L > j * SLC_L)
               & (n < (S - CMP_L) // CMP_STRIDE + 1)).astype(np.float32)
    slopes = np.asarray([_alibi_slope(h) for h in range(NSA_HEADS)], np.float32) * np.float32(LOG2E)
    qtail = np.zeros((NSA_HEADS, NSA_TAIL), np.float32)
    for c, part in enumerate(_bf16_split3(slopes)):
        qtail[:, 2 * c] = part
        qtail[:, 2 * c + 1] = part
    ktail = _pos_tail(np.arange(S))
    ctail = _pos_tail(np.arange(ncmp) * CMP_STRIDE + CMP_L - 1)
    vtail = np.zeros((S, NSA_TAIL), np.float32)
    vtail[:, 0] = 1.0
    kind = (np.arange(S)[:, None] // SLC_L == np.arange(LANES)[None, :]).astype(np.float32)
    qw = NSA_HEADS * NSA_DK
    kv0 = qw // LANES
    seqblk = lambda c: pl.BlockSpec((None, S, LANES), lambda b, i: (b, 0, c))
    cmpblk = pl.BlockSpec((None, G, ncmp, NSA_DK), lambda b, i: (b, 0, 0, 0))
    const = lambda a: pl.BlockSpec(a.shape, lambda b, i: (0, 0))
    consts = [jnp.asarray(overlap.T, F32), jnp.asarray(qtail, F32), jnp.asarray(ktail, BF16),
              jnp.asarray(ctail, BF16), jnp.asarray(vtail, BF16), jnp.asarray(kind, BF16)]
    return pl.pallas_call(
        functools.partial(_nsa_kernel, seq=S),
        grid=(B, S // Q),
        in_specs=[
            pl.BlockSpec((None, Q, qw), lambda b, i: (b, i, 0)),
            pl.BlockSpec((None, Q, LANES), lambda b, i: (b, i, kv0 + 6)),
            cmpblk, cmpblk,
            seqblk(kv0 + 2), seqblk(kv0 + 3), seqblk(kv0 + 4), seqblk(kv0 + 5),
        ] + [const(a) for a in consts],
        out_specs=pl.BlockSpec((None, Q, NSA_HEADS * NSA_DV), lambda b, i: (b, i, 0)),
        out_shape=jax.ShapeDtypeStruct((B, S, NSA_HEADS * NSA_DV), BF16),
        scratch_shapes=[
            pltpu.VMEM((G, HPG * Q, 2 * LANES), BF16),
            pltpu.VMEM((G, ncmp, LANES), BF16),
            pltpu.VMEM((G, NSA_VROWS, ncmp), BF16),
            pltpu.VMEM((G, S, 2 * LANES), BF16),
            pltpu.VMEM((G, NSA_VROWS, S), BF16),
            pltpu.VMEM((G, S, LANES), BF16),
            pltpu.VMEM((G, NSA_VROWS, S), BF16),
            pltpu.VMEM((G, NSA_DV, HPG * Q), F32),
        ],
        compiler_params=_cparams("parallel", "arbitrary"),
        name="nsa_attend",
    )(proj, proj, k_cmp, v_cmp, proj, proj, proj, proj, *consts)


def _pad_cols(w, n):
    return jnp.pad(w, ((0, 0), (0, n - w.shape[1])))


def kernel(x, norm_mix_pre, norm_mix_post, norm_mlp_pre, norm_mlp_post, mlp_w_up, mlp_w_down, ret_w_in, ret_gn, ret_w_out, gla_w_in, gla_w_gate_up, gla_b_gate, gla_gn, gla_w_out, lru_w_in, lru_conv_w, lru_conv_b, lru_w_a, lru_b_a, lru_w_x, lru_b_x, lru_lambda, lru_w_out, nsa_w_in, nsa_pe_k, nsa_w1_k, nsa_w2_k, nsa_pe_v, nsa_w1_v, nsa_w2_v, nsa_w_out):
    B, S, D = x.shape
    T = B * S
    depth = norm_mix_pre.shape[0]
    x2 = x.reshape(T, D)
    h2 = None
    wu_all, wd_all = mlp_w_up.astype(BF16), mlp_w_down.astype(BF16)
    for i in range(depth):
        m, j = i % 4, i // 4

        def project(w, w_extra=None):
            tn = _proj_tn(w.shape[1])
            if h2 is not None:
                return in_proj(h2, w.astype(BF16), tn, w_extra)
            assert w_extra is None
            return norm_proj(x2, norm_mix_pre[i], w.astype(BF16), tn)

        if m == 0:
            proj = project(ret_w_in[j])
            mix = retention_mix(proj.reshape(B, S, -1), ret_gn[j], B, S)
            w_out = ret_w_out[j]
        elif m == 1:
            nmain = gla_w_in.shape[2] - GLA_RANK
            w_low = _pad_cols(gla_w_in[j][:, nmain:], LANES).astype(BF16)
            proj, gl = project(gla_w_in[j][:, :nmain], w_low)
            w_gate_pad = jnp.pad(gla_w_gate_up[j], ((0, LANES - GLA_RANK), (0, 0)))
            mix = gla_mix(proj.reshape(B, S, -1), gl.reshape(B, S, -1), w_gate_pad, gla_b_gate[j], gla_gn[j], B, S)
            w_out = gla_w_out[j]
        elif m == 2:
            proj = project(lru_w_in[j])
            mix = lru_mix(proj.reshape(B, S, -1), lru_conv_w[j], lru_conv_b[j], lru_w_a[j], lru_b_a[j],
                          lru_w_x[j], lru_b_x[j], lru_lambda[j], B, S)
            w_out = lru_w_out[j]
        else:
            npad = -(-nsa_w_in.shape[2] // LANES) * LANES
            proj = project(_pad_cols(nsa_w_in[j], npad)).reshape(B, S, -1)
            k_cmp, v_cmp = nsa_compress(proj, nsa_pe_k[j], nsa_w1_k[j], nsa_w2_k[j],
                                        nsa_pe_v[j], nsa_w1_v[j], nsa_w2_v[j], B, S)
            mix = nsa_attend(proj, k_cmp, v_cmp, B, S)
            w_out = nsa_w_out[j]
        g_next = norm_mix_pre[i + 1] if i + 1 < depth else None
        res = sublayer_tail(mix.reshape(T, -1), w_out.astype(BF16), norm_mix_post[i], x2, norm_mlp_pre[i],
                            wu_all, wd_all, i, norm_mlp_post[i], g_next)
        x2, h2 = res if g_next is not None else (res, None)
    return x2.reshape(B, S, D)
```

```python
import functools
import math

import numpy as np
import jax
import jax.numpy as jnp
from jax import lax
from jax.experimental import pallas as pl
from jax.experimental.pallas import tpu as pltpu

F32 = jnp.float32
BF16 = jnp.bfloat16
HIGHEST = lax.Precision.HIGHEST

D_MODEL = 1024
D_FF = 4 * D_MODEL
NORM_EPS = 1e-6
NEG_INF = -1e30

RET_HEADS, RET_DK, RET_DV = 4, 256, 512
RET_CHUNK = 256

GLA_HEADS, GLA_DK, GLA_DV = 4, 128, 256
GLA_RANK = 16
GLA_TAU = 16.0
GLA_CHUNK = 128

LRU_WIDTH = D_MODEL
LRU_BLOCKS = 8
LRU_BS = LRU_WIDTH // LRU_BLOCKS
CONV_W = 4
LRU_C = 8.0
LRU_TCHUNK = 128
LRU_HIST = 128
LRU_KEEP = 16
LRU_SLAB = LRU_TCHUNK + 8

NSA_HEADS, NSA_GROUPS, NSA_HPG = 16, 2, 8
NSA_DK = 64
NSA_DV = 64
CMP_L, CMP_STRIDE = 32, 16
CMP_HID = 2 * NSA_DK
SLC_L, SLC_TOP = 64, 16
WIN = 512
FORCE_SCORE = 1e6
NSA_Q = 256
NSA_TK = 512
NSA_TAIL = 64
NSA_VROWS = NSA_DV + 16
NSA_BIG = 1e30
LOG2E = math.log2(math.e)

LANES = 128
SUBLANES = 8
VMEM_LIMIT = 56 * 1024 * 1024


def _cparams(*sem):
    return pltpu.CompilerParams(dimension_semantics=sem, vmem_limit_bytes=VMEM_LIMIT)


def _rms(x, g):
    return x * lax.rsqrt(jnp.mean(x * x, axis=-1, keepdims=True) + NORM_EPS) * g


def _dot(a, b):
    return jnp.dot(a, b, preferred_element_type=F32)


def _dot_nt(a, b):
    return lax.dot_general(a, b, (((1,), (1,)), ((), ())), preferred_element_type=F32)


def _dot_tn(a, b):
    return lax.dot_general(a, b, (((0,), (0,)), ((), ())), preferred_element_type=F32)


def _gelu_tanh(x):
    return 0.5 * x * (1.0 + jnp.tanh(math.sqrt(2.0 / math.pi) * (x + 0.044715 * (x * x * x))))


def _sigmoid(x):
    return 0.5 * (jnp.tanh(0.5 * x) + 1.0)


PROJ_TN_MAX = 3200


def _proj_tn(n):
    return max(t for t in range(LANES, min(n, PROJ_TN_MAX) + 1, LANES) if n % t == 0)


def _norm_proj_kernel(x_ref, g_ref, w_ref, o_ref, h_ref):
    @pl.when(pl.program_id(1) == 0)
    def _():
        h_ref[...] = _rms(x_ref[...], g_ref[...]).astype(BF16)

    o_ref[...] = _dot(h_ref[...], w_ref[...]).astype(o_ref.dtype)


def norm_proj(x2d, g, w_bf16, tn, tm=1024):
    T, D = x2d.shape
    N = w_bf16.shape[1]
    assert T % tm == 0 and N % tn == 0
    return pl.pallas_call(
        _norm_proj_kernel,
        grid=(T // tm, N // tn),
        in_specs=[
            pl.BlockSpec((tm, D), lambda i, j: (i, 0)),
            pl.BlockSpec((1, D), lambda i, j: (0, 0)),
            pl.BlockSpec((D, tn), lambda i, j: (0, j)),
        ],
        out_specs=pl.BlockSpec((tm, tn), lambda i, j: (i, j)),
        out_shape=jax.ShapeDtypeStruct((T, N), BF16),
        scratch_shapes=[pltpu.VMEM((tm, D), BF16)],
        compiler_params=_cparams("parallel", "arbitrary"),
        name="norm_proj",
    )(x2d, g.reshape(1, D), w_bf16)


def _proj_kernel(h_ref, w_ref, o_ref):
    o_ref[...] = _dot(h_ref[...], w_ref[...]).astype(o_ref.dtype)


def in_proj(h2d, w_bf16, tn, tm=1024):
    T, D = h2d.shape
    N = w_bf16.shape[1]
    assert T % tm == 0 and N % tn == 0
    return pl.pallas_call(
        _proj_kernel,
        grid=(T // tm, N // tn),
        in_specs=[pl.BlockSpec((tm, D), lambda i, j: (i, 0)), pl.BlockSpec((D, tn), lambda i, j: (0, j))],
        out_specs=pl.BlockSpec((tm, tn), lambda i, j: (i, j)),
        out_shape=jax.ShapeDtypeStruct((T, N), BF16),
        compiler_params=_cparams("parallel", "parallel"),
        name="in_proj",
    )(h2d, w_bf16)


MLP_FCHUNK = 512


def _tail_kernel(a_ref, wo_ref, gp_ref, x_ref, g1_ref, wu_ref, wd_ref, g2_ref, *rest):
    o_ref = rest[-2] if len(rest) == 3 else rest[0]
    x = x_ref[...] + _rms(_dot(a_ref[...], wo_ref[...]), gp_ref[...])
    h = _rms(x, g1_ref[...]).astype(BF16)
    acc = jnp.zeros(x.shape, F32)
    for c in range(D_FF // MLP_FCHUNK):
        sl = slice(c * MLP_FCHUNK, (c + 1) * MLP_FCHUNK)
        u = jnp.maximum(_dot(h, wu_ref[:, sl]), 0.0)
        acc = acc + _dot((u * u).astype(BF16), wd_ref[sl, :])
    out = x + _rms(acc, g2_ref[...])
    o_ref[...] = out
    if len(rest) == 3:
        gn_ref, _, hn_ref = rest
        hn_ref[...] = _rms(out, gn_ref[...]).astype(hn_ref.dtype)


def sublayer_tail(a2d, wo_bf16, g_post, x2d, g1, wu_all, wd_all, layer, g2, g_next=None, tm=512):
    T, D = x2d.shape
    K = a2d.shape[1]
    F = wu_all.shape[2]
    resident = lambda shape: pl.BlockSpec(shape, lambda i: (0, 0), pipeline_mode=pl.Buffered(1))
    of_layer = lambda shape: pl.BlockSpec((None,) + shape, lambda i: (layer, 0, 0), pipeline_mode=pl.Buffered(1))
    tile = pl.BlockSpec((tm, D), lambda i: (i, 0))
    row = lambda a: a.reshape(1, D)
    in_specs = [pl.BlockSpec((tm, K), lambda i: (i, 0)), resident((K, D)), resident((1, D)), tile,
                resident((1, D)), of_layer((D, F)), of_layer((F, D)), resident((1, D))]
    args = [a2d, wo_bf16, row(g_post), x2d, row(g1), wu_all, wd_all, row(g2)]
    out_specs, out_shape = tile, jax.ShapeDtypeStruct((T, D), F32)
    if g_next is not None:
        in_specs.append(resident((1, D)))
        args.append(row(g_next))
        out_specs, out_shape = [tile, tile], [out_shape, jax.ShapeDtypeStruct((T, D), BF16)]
    return pl.pallas_call(
        _tail_kernel,
        grid=(T // tm,),
        in_specs=in_specs,
        out_specs=out_specs,
        out_shape=out_shape,
        compiler_params=_cparams("parallel"),
        name="sublayer_tail",
    )(*args)


def _ret_kernel(q_ref, k_ref, v_ref, g_ref, gn_ref, di_ref, dq_ref, dk_ref, dc_ref, o_ref, st_ref):
    H, DK, DV = RET_HEADS, RET_DK, RET_DV

    @pl.when(pl.program_id(1) == 0)
    def _():
        st_ref[...] = jnp.zeros_like(st_ref)

    qb = [q_ref[:, h * DK:(h + 1) * DK] for h in range(H)]
    kb = [k_ref[:, h * DK:(h + 1) * DK] * (DK ** -0.5) for h in range(H)]
    vb = [v_ref[:, h * DV:(h + 1) * DV] for h in range(H)]
    st = [st_ref[h] for h in range(H)]
    scores = [_dot_nt(qb[h], kb[h]) for h in range(H)]
    inter = [_dot((qb[h].astype(F32) * dq_ref[h]).astype(BF16), st[h].astype(BF16)) for h in range(H)]
    for h in range(H):
        st_ref[h] = st[h] * dc_ref[h] + _dot_tn((kb[h].astype(F32) * dk_ref[h]).astype(BF16), vb[h])
    for h in range(H):
        o = _dot((scores[h] * di_ref[h]).astype(BF16), vb[h]) + inter[h]
        o = o - jnp.mean(o, axis=-1, keepdims=True)
        hs = slice(h * DV, (h + 1) * DV)
        gate = g_ref[:, hs].astype(F32)
        o_ref[:, hs] = (_rms(o, gn_ref[:, hs]) * (gate * _sigmoid(gate))).astype(o_ref.dtype)


def retention_mix(proj, gn, B, S):
    H, DK, DV, C = RET_HEADS, RET_DK, RET_DV, RET_CHUNK
    assert (DK ** -0.5) == 2.0 ** round(math.log2(DK ** -0.5))
    log_gamma = jnp.log1p(-jnp.exp2(-5.0 - jnp.arange(H, dtype=F32)))
    pos = jnp.arange(C, dtype=F32)
    diff = pos[:, None] - pos[None, :]
    d_intra = jnp.where(diff >= 0, jnp.exp(log_gamma[:, None, None] * jnp.maximum(diff, 0.0)), 0.0)
    d_q = jnp.exp(log_gamma[:, None] * (pos + 1.0))[:, :, None]
    d_k = jnp.exp(log_gamma[:, None] * (C - 1.0 - pos))[:, :, None]
    d_c = jnp.exp(log_gamma * C)[:, None, None]
    qk_w, v_w = H * DK, H * DV
    whole = lambda a: pl.BlockSpec(a.shape, lambda b, c: (0,) * a.ndim)
    return pl.pallas_call(
        _ret_kernel,
        grid=(B, S // C),
        in_specs=[
            pl.BlockSpec((None, C, qk_w), lambda b, c: (b, c, 0)),
            pl.BlockSpec((None, C, qk_w), lambda b, c: (b, c, 1)),
            pl.BlockSpec((None, C, v_w), lambda b, c: (b, c, (2 * qk_w) // v_w)),
            pl.BlockSpec((None, C, v_w), lambda b, c: (b, c, (2 * qk_w) // v_w + 1)),
            pl.BlockSpec((1, v_w), lambda b, c: (0, 0)),
            whole(d_intra), whole(d_q), whole(d_k), whole(d_c),
        ],
        out_specs=pl.BlockSpec((None, C, v_w), lambda b, c: (b, c, 0)),
        out_shape=jax.ShapeDtypeStruct((B, S, v_w), BF16),
        scratch_shapes=[pltpu.VMEM((H, DK, DV), F32)],
        compiler_params=_cparams("parallel", "arbitrary"),
        name="retention",
    )(proj, proj, proj, proj, gn.reshape(1, v_w), d_intra, d_q, d_k, d_c)


def _gla_level_sizes(C):
    sizes, s = [], 2
    while s <= C:
        sizes.append(s)
        s *= 2
    return sizes


def _gla_constants(C):
    t = np.arange(C)[:, None]
    r = np.arange(C)[None, :]
    mats = [(r <= t)]
    mats.append(r > t)
    masks = [np.eye(C, dtype=bool)]
    for bs in _gla_level_sizes(C):
        half = bs // 2
        mid = (t // bs) * bs + half - 1
        second = (t % bs) >= half
        m = np.where(second, (r > mid) & (r <= t), (r > t) & (r <= mid))
        mats.append(m)
        same = (t // bs) == (r // bs)
        masks.append(same & second & ((r % bs) < half))
    return (np.concatenate(mats, 0).astype(np.float32), np.stack(masks, 0).astype(np.float32))


def _gla_kernel(q_ref, k_ref, v_ref, r_ref, gl_ref, wg_ref, bg_ref, gn_ref, mall_ref, mask_ref, o_ref, st_ref):
    C, DK = GLA_CHUNK, GLA_DK

    H, DV = GLA_HEADS, GLA_DV
    W = H * DK

    @pl.when(pl.program_id(1) == 0)
    def _():
        st_ref[...] = jnp.zeros_like(st_ref)

    z = _dot(gl_ref[...], wg_ref[...]) + bg_ref[...]
    la = (jnp.minimum(z, 0.0) - jnp.log1p(jnp.exp(-jnp.abs(z)))) * (1.0 / GLA_TAU)
    la_hi = la.astype(BF16)
    la_lo = (la - la_hi.astype(F32)).astype(BF16)
    cum = _dot(mall_ref[0:2 * C, :], jnp.concatenate([la_hi, la_lo], axis=1))
    decay_cum = jnp.exp(cum[:, :W] + cum[:, W:])
    decay = jnp.exp(_dot(mall_ref[2 * C:, :], la_hi))
    e_b = decay_cum[0:C]
    e_rest = decay_cum[C:2 * C]
    e_last = decay_cum[C - 1:C]

    q = q_ref[...].astype(F32) * (DK ** -0.5)
    k = k_ref[...].astype(F32)
    nlev = len(_gla_level_sizes(C))
    qe = [q.astype(BF16)] + [(q * decay[l * C:(l + 1) * C]).astype(BF16) for l in range(nlev)]
    ke = [k_ref[...]] + [(k * decay[l * C:(l + 1) * C]).astype(BF16) for l in range(nlev)]
    q_in = (q * e_b).astype(BF16)
    k_out = (k * e_rest).astype(BF16)
    for h in range(H):
        ks = slice(h * DK, (h + 1) * DK)
        vs = slice(h * DV, (h + 1) * DV)
        vb = v_ref[:, vs]
        a = _dot_nt(qe[0][:, ks], ke[0][:, ks]) * mask_ref[0]
        for l in range(nlev):
            a = a + _dot_nt(qe[1 + l][:, ks], ke[1 + l][:, ks]) * mask_ref[1 + l]
        st = st_ref[h]
        o = _dot(a.astype(BF16), vb) + _dot_nt(q_in[:, ks], st.astype(BF16))
        st_ref[h] = st * e_last[:, ks] + _dot_tn(vb, k_out[:, ks])
        gate = r_ref[:, vs].astype(F32)
        o_ref[:, vs] = (_rms(o, gn_ref[:, vs]) * (gate * _sigmoid(gate))).astype(o_ref.dtype)


def gla_mix(proj, w_gate_pad, b_gate, gn, B, S):
    H, DK, DV, C = GLA_HEADS, GLA_DK, GLA_DV, GLA_CHUNK
    mall, masks = _gla_constants(C)
    qk_w, v_w = H * DK, H * DV
    gl_blk = (2 * qk_w + 2 * v_w) // LANES
    whole = lambda shape: pl.BlockSpec(shape, lambda b, c: (0,) * len(shape))
    return pl.pallas_call(
        _gla_kernel,
        grid=(B, S // C),
        in_specs=[
            pl.BlockSpec((None, C, qk_w), lambda b, c: (b, c, 0)),
            pl.BlockSpec((None, C, qk_w), lambda b, c: (b, c, 1)),
            pl.BlockSpec((None, C, v_w), lambda b, c: (b, c, (2 * qk_w) // v_w)),
            pl.BlockSpec((None, C, v_w), lambda b, c: (b, c, (2 * qk_w) // v_w + 1)),
            pl.BlockSpec((None, C, LANES), lambda b, c: (b, c, gl_blk)),
            whole((LANES, qk_w)), whole((1, qk_w)), whole((1, v_w)), whole(mall.shape), whole(masks.shape),
        ],
        out_specs=pl.BlockSpec((None, C, v_w), lambda b, c: (b, c, 0)),
        out_shape=jax.ShapeDtypeStruct((B, S, v_w), BF16),
        scratch_shapes=[pltpu.VMEM((H, DV, DK), F32)],
        compiler_params=_cparams("parallel", "arbitrary"),
        name="gla",
    )(proj, proj, proj, proj, proj, w_gate_pad, b_gate.reshape(1, H * DK), gn.reshape(1, H * DV),
      jnp.asarray(mall, BF16), jnp.asarray(masks, F32))


def _lru_kernel(x_ref, y_ref, sh_ref, cw_ref, cb_ref, wa_ref, ba_ref, wx_ref, bx_ref, lam_ref, o_ref,
                xbuf, xc_s, a_s, u_s, hcar):
    B, TC, W = x_ref.shape
    HIST, SLAB = LRU_HIST, LRU_SLAB

    @pl.when(pl.program_id(0) == 0)
    def _():
        xbuf[...] = jnp.zeros_like(xbuf)
        hcar[...] = jnp.zeros_like(hcar)

    for b in range(B):
        xbuf[b, HIST:HIST + TC, :] = x_ref[b]
        xe = xbuf[b]
        xc = cb_ref[...] + x_ref[b].astype(F32) * cw_ref[CONV_W - 1:CONV_W, :]
        for tap in range(CONV_W - 1):
            xc = xc + _dot(sh_ref[tap], xe) * cw_ref[tap:tap + 1, :]
        xc_s[b * TC:(b + 1) * TC, :] = xc
        xbuf[b, HIST - LRU_KEEP:HIST, :] = xbuf[b, HIST + TC - LRU_KEEP:HIST + TC, :]

    xc = xc_s[...]
    xcb = xc.astype(BF16)
    for n in range(LRU_BLOCKS):
        sl = slice(n * LRU_BS, (n + 1) * LRU_BS)
        xc_s[:, sl] = _dot(xcb[:, sl], wa_ref[n])
    gate_r = _sigmoid(xc_s[...] + ba_ref[...])
    for n in range(LRU_BLOCKS):
        sl = slice(n * LRU_BS, (n + 1) * LRU_BS)
        xc_s[:, sl] = _dot(xcb[:, sl], wx_ref[n])
    gate_i = _sigmoid(xc_s[...] + bx_ref[...])
    nl = -lam_ref[...]
    softplus = jnp.maximum(nl, 0.0) + jnp.log1p(jnp.exp(-jnp.abs(nl)))
    a = jnp.exp(-LRU_C * gate_r * softplus)
    u = jnp.sqrt(1.0 - a * a) * gate_i * xc
    NL = W // LANES
    for j in range(NL):
        for b in range(B):
            a_s[j, b * SLAB:b * SLAB + TC, :] = a[b * TC:(b + 1) * TC, j * LANES:(j + 1) * LANES]
            u_s[j, b * SLAB:b * SLAB + TC, :] = u[b * TC:(b + 1) * TC, j * LANES:(j + 1) * LANES]

    def body(t, hs):
        rows = pl.ds(t, B, stride=SLAB)
        out = []
        for j in range(NL):
            h = a_s[j, rows, :] * hs[j] + u_s[j, rows, :]
            u_s[j, rows, :] = h
            out.append(h)
        return tuple(out)

    h0 = tuple(hcar[:, j * LANES:(j + 1) * LANES] for j in range(NL))
    hT = lax.fori_loop(0, TC, body, h0, unroll=8)
    hcar[...] = jnp.concatenate(hT, axis=1)
    for b in range(B):
        hs = jnp.concatenate([u_s[j, b * SLAB:b * SLAB + TC, :] for j in range(NL)], axis=1)
        o_ref[b] = (hs * _gelu_tanh(y_ref[b].astype(F32))).astype(o_ref.dtype)


def lru_mix(proj, conv_w, conv_b, w_a, b_a, w_x, b_x, lam, B, S):
    TC, W, HIST, SLAB = LRU_TCHUNK, LRU_WIDTH, LRU_HIST, LRU_SLAB
    assert LRU_KEEP >= CONV_W - 1 and S % TC == 0
    t_idx = np.arange(TC)[:, None]
    c_idx = np.arange(HIST + TC)[None, :]
    shift = np.stack([(c_idx == HIST + t_idx - (CONV_W - 1 - tap)) for tap in range(CONV_W - 1)]).astype(np.float32)
    row = lambda a: a.reshape(1, W)
    vec = pl.BlockSpec((1, W), lambda t: (0, 0))
    blk = pl.BlockSpec((LRU_BLOCKS, LRU_BS, LRU_BS), lambda t: (0, 0, 0))
    return pl.pallas_call(
        _lru_kernel,
        grid=(S // TC,),
        in_specs=[
            pl.BlockSpec((B, TC, W), lambda t: (0, t, 0)),
            pl.BlockSpec((B, TC, W), lambda t: (0, t, 1)),
            pl.BlockSpec(shift.shape, lambda t: (0, 0, 0)),
            pl.BlockSpec((CONV_W, W), lambda t: (0, 0)),
            vec, blk, vec, blk, vec, vec,
        ],
        out_specs=pl.BlockSpec((B, TC, W), lambda t: (0, t, 0)),
        out_shape=jax.ShapeDtypeStruct((B, S, W), BF16),
        scratch_shapes=[
            pltpu.VMEM((B, HIST + TC, W), BF16),
            pltpu.VMEM((B * TC, W), F32),
            pltpu.VMEM((W // LANES, B * SLAB, LANES), F32),
            pltpu.VMEM((W // LANES, B * SLAB, LANES), F32),
            pltpu.VMEM((B, W), F32),
        ],
        compiler_params=_cparams("arbitrary"),
        name="rglru",
    )(proj, proj, jnp.asarray(shift, BF16), conv_w, row(conv_b), w_a.astype(BF16), row(b_a),
      w_x.astype(BF16), row(b_x), row(lam))


def _nsa_cmp_kernel(kc_ref, vc_ref, pek_ref, w1k_ref, w2k_ref, pev_ref, w1v_ref, w2v_ref, ko_ref, vo_ref,
                    buf, *, nblk):
    def compress(t_ref, pe_ref, w1_ref, w2_ref, o_ref, d):
        buf[...] = t_ref[...].astype(F32)
        ya = [jnp.zeros((nblk, CMP_HID), F32) for _ in range(NSA_GROUPS)]
        yb = [jnp.zeros((nblk, CMP_HID), F32) for _ in range(NSA_GROUPS)]
        for l in range(CMP_STRIDE):
            both = buf[pl.ds(l, nblk, stride=CMP_STRIDE), :]
            for g in range(NSA_GROUPS):
                rows = both[:, g * d:(g + 1) * d]
                lo = (rows + pe_ref[l:l + 1, :]).astype(BF16)
                hi = (rows + pe_ref[CMP_STRIDE + l:CMP_STRIDE + l + 1, :]).astype(BF16)
                ya[g] = ya[g] + _dot(lo, w1_ref[l * d:(l + 1) * d, :])
                yb[g] = yb[g] + _dot(hi, w1_ref[(CMP_STRIDE + l) * d:(CMP_STRIDE + l + 1) * d, :])
        for g in range(NSA_GROUPS):
            shifted = jnp.concatenate([yb[g][1:], jnp.zeros((1, CMP_HID), F32)], axis=0)
            hid = _gelu_tanh(ya[g] + shifted).astype(BF16)
            o_ref[g] = _dot(hid, w2_ref[...])

    compress(kc_ref, pek_ref, w1k_ref, w2k_ref, ko_ref, NSA_DK)
    compress(vc_ref, pev_ref, w1v_ref, w2v_ref, vo_ref, NSA_DV)


def nsa_compress(proj, pe_k, w1_k, w2_k, pe_v, w1_v, w2_v, B, S):
    nblk = S // CMP_STRIDE
    G = NSA_GROUPS
    kc_blk = (NSA_HEADS * NSA_DK) // LANES
    full2 = lambda a: pl.BlockSpec(a.shape, lambda b: (0, 0))
    w1k, w2k, w1v, w2v = (a.astype(BF16) for a in (w1_k, w2_k, w1_v, w2_v))
    out_sd = jax.ShapeDtypeStruct((B, G, nblk, NSA_DK), F32)
    return pl.pallas_call(
        functools.partial(_nsa_cmp_kernel, nblk=nblk),
        grid=(B,),
        in_specs=[
            pl.BlockSpec((None, S, LANES), lambda b: (b, 0, kc_blk)),
            pl.BlockSpec((None, S, LANES), lambda b: (b, 0, kc_blk + 1)),
            full2(pe_k), full2(w1k), full2(w2k), full2(pe_v), full2(w1v), full2(w2v),
        ],
        out_specs=[pl.BlockSpec((None, G, nblk, NSA_DK), lambda b: (b, 0, 0, 0))] * 2,
        out_shape=[out_sd, out_sd],
        scratch_shapes=[pltpu.VMEM((S, LANES), F32)],
        compiler_params=_cparams("parallel"),
        name="nsa_compress",
    )(proj, proj, pe_k, w1k, w2k, pe_v, w1v, w2v)


def _alibi_slope(head):
    return float(2.0 ** (-8.0 * (head + 1.0) / NSA_HEADS))


def _nsa_kernel(q_ref, gate_ref, kc_ref, vc_ref, ks_ref, vs_ref, kw_ref, vw_ref,
                ovt_ref, qt_ref, kt_ref, ct_ref, vt_ref, kind_ref, o_ref,
                qa_s, kca_s, vct_s, ksa_s, vst_s, kwa_s, vwt_s, part_s, *, seq):
    Q, TK, HPG, DK, DV, G = NSA_Q, NSA_TK, NSA_HPG, NSA_DK, NSA_DV, NSA_GROUPS
    R = HPG * Q
    W = WIN + Q
    ncmp = seq // CMP_STRIDE
    nslc = seq // SLC_L
    scale = DK ** -0.5
    qi = pl.program_id(1)
    q0 = qi * Q
    t_row = q0 + lax.broadcasted_iota(jnp.int32, (1, Q), 1)
    gates_t = _sigmoid(gate_ref[...].astype(F32)).T
    per_head = lambda a: jnp.concatenate([a] * HPG, axis=1)

    @pl.when(qi == 0)
    def _():
        for g in range(G):
            ks = slice(g * DK, (g + 1) * DK)
            vs = slice(g * DV, (g + 1) * DV)
            ksa_s[g, :, 0:LANES] = jnp.concatenate([ks_ref[:, ks], kt_ref[...]], axis=1)
            ksa_s[g, :, LANES:2 * LANES] = kind_ref[...]
            kwa_s[g] = jnp.concatenate([kw_ref[:, ks], kt_ref[...]], axis=1)
            tr = lambda a: a.astype(F32).T[0:NSA_VROWS].astype(BF16)
            vst_s[g] = tr(jnp.concatenate([vs_ref[:, vs], vt_ref[...]], axis=1))
            vwt_s[g] = tr(jnp.concatenate([vw_ref[:, vs], vt_ref[...]], axis=1))
            kca_s[g] = jnp.concatenate([kc_ref[g].astype(BF16), ct_ref[...]], axis=1)
            vct_s[g] = tr(jnp.concatenate([vc_ref[g], jnp.zeros((ncmp, NSA_TAIL), F32)], axis=1))

    for g in range(G):
        for h in range(HPG):
            col = g * HPG + h
            qh = q_ref[:, col * DK:(col + 1) * DK].astype(F32) * (scale * LOG2E)
            tail = jnp.broadcast_to(qt_ref[col:col + 1, :], (Q, NSA_TAIL))
            qa_s[g, h * Q:(h + 1) * Q, 0:LANES] = jnp.concatenate([qh, tail], axis=1).astype(BF16)

    cend = CMP_STRIDE * lax.broadcasted_iota(jnp.int32, (ncmp, 1), 0) + (CMP_L - 1)
    bias_c = per_head(jnp.where(cend <= t_row, 0.0, -NSA_BIG))
    scores_c = [_dot_nt(kca_s[g], qa_s[g, :, 0:LANES]) + bias_c for g in range(G)]
    psum_t = []
    for g in range(G):
        s = scores_c[g]
        e = jnp.exp2(s - jnp.max(s, axis=0, keepdims=True))
        inv = per_head(jnp.where(t_row >= CMP_L - 1, 1.0, 0.0)) / jnp.sum(e, axis=0, keepdims=True)
        p = e * inv
        ps = p[:, 0:Q]
        for h in range(1, HPG):
            ps = ps + p[:, h * Q:(h + 1) * Q]
        psum_t.append(ps)
        o_cmp_t = _dot(vct_s[g], p.astype(BF16))
        for h in range(HPG):
            col = g * HPG + h
            cs = slice(h * Q, (h + 1) * Q)
            part_s[g, :, cs] = gates_t[col:col + 1] * o_cmp_t[0:DV, cs]

    for g in range(G):
        imp_t = jnp.dot(ovt_ref[...], psum_t[g], preferred_element_type=F32, precision=HIGHEST)
        jcol = lax.broadcasted_iota(jnp.int32, (nslc, 1), 0)
        cur = t_row // SLC_L
        forced = (jcol == 0) | (jcol == cur) | (jcol == cur - 1)
        imp_t = jnp.where(forced, FORCE_SCORE, imp_t)
        imp_t = jnp.where(jcol <= cur, imp_t, -1.0)
        nrow = nslc // SUBLANES
        rows = [imp_t[r * SUBLANES:(r + 1) * SUBLANES] for r in range(nrow)]
        ranks = [jnp.zeros((SUBLANES, Q), F32) for _ in range(nrow)]
        jloc = lax.broadcasted_iota(jnp.int32, (SUBLANES, Q), 0)
        for j2 in range(nslc):
            other = jnp.broadcast_to(imp_t[j2:j2 + 1, :], (SUBLANES, Q))
            for r in range(nrow):
                if (r + 1) * SUBLANES - 1 <= j2:
                    ahead = jnp.where(other > rows[r], 1.0, 0.0)
                elif r * SUBLANES > j2:
                    ahead = jnp.where(other >= rows[r], 1.0, 0.0)
                else:
                    ahead = jnp.where(jloc + r * SUBLANES > j2, jnp.where(other >= rows[r], 1.0, 0.0),
                                      jnp.where(other > rows[r], 1.0, 0.0))
                ranks[r] = ranks[r] + ahead
        ntop = float(min(SLC_TOP, nslc))
        bias_t = jnp.concatenate([jnp.where(rk < ntop, 0.0, -NSA_BIG) for rk in ranks]
                                 + [jnp.zeros((LANES - nslc, Q), F32)], axis=0)
        sel_bias = bias_t.T.astype(BF16)
        for h in range(HPG):
            qa_s[g, h * Q:(h + 1) * Q, LANES:2 * LANES] = sel_bias

    state0 = tuple((jnp.full((1, R), -NSA_BIG, F32), jnp.zeros((NSA_VROWS, R), F32)) for _ in range(G))

    def sel_tile(k0, nk, state, causal):
        scores = [_dot_nt(ksa_s[g, pl.ds(k0, nk), :], qa_s[g]) for g in range(G)]
        if causal:
            pos = k0 + lax.broadcasted_iota(jnp.int32, (nk, 1), 0)
            bias = per_head(jnp.where(pos <= t_row, 0.0, -NSA_BIG))
            scores = [s + bias for s in scores]
        out = []
        for g in range(G):
            m_old, acc = state[g]
            m_new = jnp.maximum(m_old, jnp.max(scores[g], axis=0, keepdims=True))
            p = jnp.exp2(scores[g] - m_new).astype(BF16)
            acc = jnp.exp2(m_old - m_new) * acc + _dot(vst_s[g, :, pl.ds(k0, nk)], p)
            out.append((m_new, acc))
        return tuple(out)

    state = lax.fori_loop(0, q0 // TK, lambda kt, st: sel_tile(pl.multiple_of(kt * TK, TK), TK, st, False), state0)
    state = lax.cond(q0 % TK != 0, lambda st: sel_tile(pl.multiple_of(q0 - Q, Q), Q, st, False), lambda st: st,
                     state)
    state = sel_tile(pl.multiple_of(q0, Q), Q, state, True)

    w0 = pl.multiple_of(jnp.maximum(q0 - WIN, 0), Q)
    wpos = w0 + lax.broadcasted_iota(jnp.int32, (W, 1), 0)
    dist_w = t_row - wpos
    bias_w = per_head(jnp.where((dist_w >= 0) & (dist_w < WIN), 0.0, -NSA_BIG))
    scores_w = [_dot_nt(kwa_s[g, pl.ds(w0, W), :], qa_s[g, :, 0:LANES]) + bias_w for g in range(G)]
    for g in range(G):
        s = scores_w[g]
        e = jnp.exp2(s - jnp.max(s, axis=0, keepdims=True))
        pv_w = _dot(vwt_s[g, :, pl.ds(w0, W)], e.astype(BF16))
        o_win_t = pv_w[0:DV] / pv_w[DV:DV + 1]
        for h in range(HPG):
            col = g * HPG + h
            cs = slice(h * Q, (h + 1) * Q)
            part_s[g, :, cs] += gates_t[2 * NSA_HEADS + col:2 * NSA_HEADS + col + 1] * o_win_t[:, cs]

    for g in range(G):
        acc = state[g][1]
        o_sel_t = acc[0:DV] / acc[DV:DV + 1]
        outs = []
        for h in range(HPG):
            col = g * HPG + h
            cs = slice(h * Q, (h + 1) * Q)
            outs.append(part_s[g, :, cs] + gates_t[NSA_HEADS + col:NSA_HEADS + col + 1] * o_sel_t[:, cs])
        for h in range(0, HPG, 2):
            col = g * HPG + h
            o_ref[:, col * DV:(col + 2) * DV] = jnp.concatenate([outs[h], outs[h + 1]], axis=0).T.astype(o_ref.dtype)


def _bf16_split3(x):
    x = np.asarray(x, np.float32)
    rnd = lambda a: a.astype(BF16).astype(np.float32)
    hi = rnd(x)
    mid = rnd(x - hi)
    lo = rnd(x - hi - mid)
    return hi, mid, lo


def _pos_tail(pos):
    pos = np.asarray(pos)
    hi, lo = (pos // SLC_L) * SLC_L, pos % SLC_L
    tail = np.zeros((pos.shape[0], NSA_TAIL), np.float32)
    for c in range(3):
        tail[:, 2 * c] = hi
        tail[:, 2 * c + 1] = lo
    return tail


def nsa_attend(proj, k_cmp, v_cmp, B, S):
    Q, G, HPG = NSA_Q, NSA_GROUPS, NSA_HPG
    ncmp, nslc = S // CMP_STRIDE, S // SLC_L
    assert S >= WIN + Q and S % NSA_TK == 0 and nslc <= LANES and nslc % SUBLANES == 0 and NSA_TK == 2 * Q
    n = np.arange(ncmp)[:, None]
    j = np.arange(nslc)[None, :]
    overlap = ((n * CMP_STRIDE < (j + 1) * SLC_L) & (n * CMP_STRIDE + CMP_L > j * SLC_L)
               & (n < (S - CMP_L) // CMP_STRIDE + 1)).astype(np.float32)
    slopes = np.asarray([_alibi_slope(h) for h in range(NSA_HEADS)], np.float32) * np.float32(LOG2E)
    qtail = np.zeros((NSA_HEADS, NSA_TAIL), np.float32)
    for c, part in enumerate(_bf16_split3(slopes)):
        qtail[:, 2 * c] = part
        qtail[:, 2 * c + 1] = part
    ktail = _pos_tail(np.arange(S))
    ctail = _pos_tail(np.arange(ncmp) * CMP_STRIDE + CMP_L - 1)
    vtail = np.zeros((S, NSA_TAIL), np.float32)
    vtail[:, 0] = 1.0
    kind = (np.arange(S)[:, None] // SLC_L == np.arange(LANES)[None, :]).astype(np.float32)
    qw = NSA_HEADS * NSA_DK
    kv0 = qw // LANES
    seqblk = lambda c: pl.BlockSpec((None, S, LANES), lambda b, i: (b, 0, c))
    cmpblk = pl.BlockSpec((None, G, ncmp, NSA_DK), lambda b, i: (b, 0, 0, 0))
    const = lambda a: pl.BlockSpec(a.shape, lambda b, i: (0, 0))
    consts = [jnp.asarray(overlap.T, F32), jnp.asarray(qtail, F32), jnp.asarray(ktail, BF16),
              jnp.asarray(ctail, BF16), jnp.asarray(vtail, BF16), jnp.asarray(kind, BF16)]
    return pl.pallas_call(
        functools.partial(_nsa_kernel, seq=S),
        grid=(B, S // Q),
        in_specs=[
            pl.BlockSpec((None, Q, qw), lambda b, i: (b, i, 0)),
            pl.BlockSpec((None, Q, LANES), lambda b, i: (b, i, kv0 + 6)),
            cmpblk, cmpblk,
            seqblk(kv0 + 2), seqblk(kv0 + 3), seqblk(kv0 + 4), seqblk(kv0 + 5),
        ] + [const(a) for a in consts],
        out_specs=pl.BlockSpec((None, Q, NSA_HEADS * NSA_DV), lambda b, i: (b, i, 0)),
        out_shape=jax.ShapeDtypeStruct((B, S, NSA_HEADS * NSA_DV), BF16),
        scratch_shapes=[
            pltpu.VMEM((G, HPG * Q, 2 * LANES), BF16),
            pltpu.VMEM((G, ncmp, LANES), BF16),
            pltpu.VMEM((G, NSA_VROWS, ncmp), BF16),
            pltpu.VMEM((G, S, 2 * LANES), BF16),
            pltpu.VMEM((G, NSA_VROWS, S), BF16),
            pltpu.VMEM((G, S, LANES), BF16),
            pltpu.VMEM((G, NSA_VROWS, S), BF16),
            pltpu.VMEM((G, NSA_DV, HPG * Q), F32),
        ],
        compiler_params=_cparams("parallel", "arbitrary"),
        name="nsa_attend",
    )(proj, proj, k_cmp, v_cmp, proj, proj, proj, proj, *consts)


def _pad_cols(w, n):
    return jnp.pad(w, ((0, 0), (0, n - w.shape[1])))


def kernel(x, norm_mix_pre, norm_mix_post, norm_mlp_pre, norm_mlp_post, mlp_w_up, mlp_w_down, ret_w_in, ret_gn, ret_w_out, gla_w_in, gla_w_gate_up, gla_b_gate, gla_gn, gla_w_out, lru_w_in, lru_conv_w, lru_conv_b, lru_w_a, lru_b_a, lru_w_x, lru_b_x, lru_lambda, lru_w_out, nsa_w_in, nsa_pe_k, nsa_w1_k, nsa_w2_k, nsa_pe_v, nsa_w1_v, nsa_w2_v, nsa_w_out):
    B, S, D = x.shape
    T = B * S
    depth = norm_mix_pre.shape[0]
    x2 = x.reshape(T, D)
    h2 = None
    wu_all, wd_all = mlp_w_up.astype(BF16), mlp_w_down.astype(BF16)
    for i in range(depth):
        m, j = i % 4, i // 4

        def project(w):
            w = _pad_cols(w, -(-w.shape[1] // LANES) * LANES).astype(BF16)
            tn = _proj_tn(w.shape[1])
            if h2 is not None:
                return in_proj(h2, w, tn)
            return norm_proj(x2, norm_mix_pre[i], w, tn)

        if m == 0:
            proj = project(ret_w_in[j])
            mix = retention_mix(proj.reshape(B, S, -1), ret_gn[j], B, S)
            w_out = ret_w_out[j]
        elif m == 1:
            proj = project(gla_w_in[j])
            w_gate_pad = jnp.pad(gla_w_gate_up[j], ((0, LANES - GLA_RANK), (0, 0))).astype(BF16)
            mix = gla_mix(proj.reshape(B, S, -1), w_gate_pad, gla_b_gate[j], gla_gn[j], B, S)
            w_out = gla_w_out[j]
        elif m == 2:
            proj = project(lru_w_in[j])
            mix = lru_mix(proj.reshape(B, S, -1), lru_conv_w[j], lru_conv_b[j], lru_w_a[j], lru_b_a[j],
                          lru_w_x[j], lru_b_x[j], lru_lambda[j], B, S)
            w_out = lru_w_out[j]
        else:
            proj = project(nsa_w_in[j]).reshape(B, S, -1)
            k_cmp, v_cmp = nsa_compress(proj, nsa_pe_k[j], nsa_w1_k[j], nsa_w2_k[j],
                                        nsa_pe_v[j], nsa_w1_v[j], nsa_w2_v[j], B, S)
            mix = nsa_attend(proj, k_cmp, v_cmp, B, S)
            w_out = nsa_w_out[j]
        g_next = norm_mix_pre[i + 1] if i + 1 < depth else None
        res = sublayer_tail(mix.reshape(T, -1), w_out.astype(BF16), norm_mix_post[i], x2, norm_mlp_pre[i],
                            wu_all, wd_all, i, norm_mlp_post[i], g_next)
        x2, h2 = res if g_next is not None else (res, None)
    return x2.reshape(B, S, D)
```

```python
import functools
import math

import numpy as np
import jax
import jax.numpy as jnp
from jax import lax
from jax.experimental import pallas as pl
from jax.experimental.pallas import tpu as pltpu

F32 = jnp.float32
BF16 = jnp.bfloat16
HIGHEST = lax.Precision.HIGHEST

D_MODEL = 1024
D_FF = 4 * D_MODEL
NORM_EPS = 1e-6
NEG_INF = -1e30

RET_HEADS, RET_DK, RET_DV = 4, 256, 512
RET_CHUNK = 256

GLA_HEADS, GLA_DK, GLA_DV = 4, 128, 256
GLA_RANK = 16
GLA_TAU = 16.0
GLA_CHUNK = 128

LRU_WIDTH = D_MODEL
LRU_BLOCKS = 8
LRU_BS = LRU_WIDTH // LRU_BLOCKS
CONV_W = 4
LRU_C = 8.0
LRU_TCHUNK = 128
LRU_HIST = 128
LRU_KEEP = 16
LRU_SLAB = LRU_TCHUNK + 8

NSA_HEADS, NSA_GROUPS, NSA_HPG = 16, 2, 8
NSA_DK = 64
NSA_DV = 64
CMP_L, CMP_STRIDE = 32, 16
CMP_HID = 2 * NSA_DK
SLC_L, SLC_TOP = 64, 16
WIN = 512
FORCE_SCORE = 1e6
NSA_Q = 256
NSA_TK = 512
NSA_TAIL = 64
NSA_VROWS = NSA_DV + 16
NSA_BIG = 1e30
LOG2E = math.log2(math.e)

LANES = 128
SUBLANES = 8
VMEM_LIMIT = 56 * 1024 * 1024


def _cparams(*sem):
    return pltpu.CompilerParams(dimension_semantics=sem, vmem_limit_bytes=VMEM_LIMIT)


def _rms(x, g):
    return x * lax.rsqrt(jnp.mean(x * x, axis=-1, keepdims=True) + NORM_EPS) * g


def _dot(a, b):
    return jnp.dot(a, b, preferred_element_type=F32)


def _dot_nt(a, b):
    return lax.dot_general(a, b, (((1,), (1,)), ((), ())), preferred_element_type=F32)


def _dot_tn(a, b):
    return lax.dot_general(a, b, (((0,), (0,)), ((), ())), preferred_element_type=F32)


def _gelu_tanh(x):
    return 0.5 * x * (1.0 + jnp.tanh(math.sqrt(2.0 / math.pi) * (x + 0.044715 * (x * x * x))))


def _sigmoid(x):
    return 0.5 * (jnp.tanh(0.5 * x) + 1.0)


def _silu(x):
    t = 0.5 * x
    return t * jnp.tanh(t) + t


PROJ_TN_MAX = 3200


def _proj_tn(n):
    return max(t for t in range(LANES, min(n, PROJ_TN_MAX) + 1, LANES) if n % t == 0)


def _norm_proj_kernel(x_ref, g_ref, w_ref, o_ref, h_ref):
    @pl.when(pl.program_id(1) == 0)
    def _():
        h_ref[...] = _rms(x_ref[...], g_ref[...]).astype(BF16)

    o_ref[...] = _dot(h_ref[...], w_ref[...]).astype(o_ref.dtype)


def norm_proj(x2d, g, w_bf16, tn, tm=1024):
    T, D = x2d.shape
    N = w_bf16.shape[1]
    assert T % tm == 0 and N % tn == 0
    return pl.pallas_call(
        _norm_proj_kernel,
        grid=(T // tm, N // tn),
        in_specs=[
            pl.BlockSpec((tm, D), lambda i, j: (i, 0)),
            pl.BlockSpec((1, D), lambda i, j: (0, 0)),
            pl.BlockSpec((D, tn), lambda i, j: (0, j)),
        ],
        out_specs=pl.BlockSpec((tm, tn), lambda i, j: (i, j)),
        out_shape=jax.ShapeDtypeStruct((T, N), BF16),
        scratch_shapes=[pltpu.VMEM((tm, D), BF16)],
        compiler_params=_cparams("parallel", "arbitrary"),
        name="norm_proj",
    )(x2d, g.reshape(1, D), w_bf16)


def _proj_kernel(h_ref, w_ref, o_ref):
    o_ref[...] = _dot(h_ref[...], w_ref[...]).astype(o_ref.dtype)


def in_proj(h2d, w_bf16, tn, tm=1024):
    T, D = h2d.shape
    N = w_bf16.shape[1]
    assert T % tm == 0 and N % tn == 0
    return pl.pallas_call(
        _proj_kernel,
        grid=(T // tm, N // tn),
        in_specs=[pl.BlockSpec((tm, D), lambda i, j: (i, 0)), pl.BlockSpec((D, tn), lambda i, j: (0, j))],
        out_specs=pl.BlockSpec((tm, tn), lambda i, j: (i, j)),
        out_shape=jax.ShapeDtypeStruct((T, N), BF16),
        compiler_params=_cparams("parallel", "parallel"),
        name="in_proj",
    )(h2d, w_bf16)


MLP_FCHUNK = 512


def _tail_kernel(a_ref, wo_ref, gp_ref, x_ref, g1_ref, wu_ref, wd_ref, g2_ref, *rest):
    o_ref = rest[-2] if len(rest) == 3 else rest[0]
    x = x_ref[...] + _rms(_dot(a_ref[...], wo_ref[...]), gp_ref[...])
    h = _rms(x, g1_ref[...]).astype(BF16)
    acc = jnp.zeros(x.shape, F32)
    for c in range(D_FF // MLP_FCHUNK):
        sl = slice(c * MLP_FCHUNK, (c + 1) * MLP_FCHUNK)
        u = jnp.maximum(_dot(h, wu_ref[:, sl]), 0.0)
        acc = acc + _dot((u * u).astype(BF16), wd_ref[sl, :])
    out = x + _rms(acc, g2_ref[...])
    o_ref[...] = out
    if len(rest) == 3:
        gn_ref, _, hn_ref = rest
        hn_ref[...] = _rms(out, gn_ref[...]).astype(hn_ref.dtype)


def sublayer_tail(a2d, wo_bf16, g_post, x2d, g1, wu_all, wd_all, layer, g2, g_next=None, tm=512):
    T, D = x2d.shape
    K = a2d.shape[1]
    F = wu_all.shape[2]
    resident = lambda shape: pl.BlockSpec(shape, lambda i: (0, 0), pipeline_mode=pl.Buffered(1))
    of_layer = lambda shape: pl.BlockSpec((None,) + shape, lambda i: (layer, 0, 0), pipeline_mode=pl.Buffered(1))
    tile = pl.BlockSpec((tm, D), lambda i: (i, 0))
    row = lambda a: a.reshape(1, D)
    in_specs = [pl.BlockSpec((tm, K), lambda i: (i, 0)), resident((K, D)), resident((1, D)), tile,
                resident((1, D)), of_layer((D, F)), of_layer((F, D)), resident((1, D))]
    args = [a2d, wo_bf16, row(g_post), x2d, row(g1), wu_all, wd_all, row(g2)]
    out_specs, out_shape = tile, jax.ShapeDtypeStruct((T, D), F32)
    if g_next is not None:
        in_specs.append(resident((1, D)))
        args.append(row(g_next))
        out_specs, out_shape = [tile, tile], [out_shape, jax.ShapeDtypeStruct((T, D), BF16)]
    return pl.pallas_call(
        _tail_kernel,
        grid=(T // tm,),
        in_specs=in_specs,
        out_specs=out_specs,
        out_shape=out_shape,
        compiler_params=_cparams("parallel"),
        name="sublayer_tail",
    )(*args)


def _ret_kernel(q_ref, k_ref, v_ref, g_ref, gn_ref, di_ref, dq_ref, dk_ref, dc_ref, o_ref, st_ref):
    H, DK, DV = RET_HEADS, RET_DK, RET_DV

    @pl.when(pl.program_id(1) == 0)
    def _():
        st_ref[...] = jnp.zeros_like(st_ref)

    qb = [q_ref[:, h * DK:(h + 1) * DK] for h in range(H)]
    kb = [k_ref[:, h * DK:(h + 1) * DK] * (DK ** -0.5) for h in range(H)]
    vb = [v_ref[:, h * DV:(h + 1) * DV] for h in range(H)]
    st = [st_ref[h] for h in range(H)]
    scores = [_dot_nt(qb[h], kb[h]) for h in range(H)]
    inter = [_dot((qb[h].astype(F32) * dq_ref[h]).astype(BF16), st[h].astype(BF16)) for h in range(H)]
    for h in range(H):
        st_ref[h] = st[h] * dc_ref[h] + _dot_tn((kb[h].astype(F32) * dk_ref[h]).astype(BF16), vb[h])
    for h in range(H):
        o = _dot((scores[h] * di_ref[h]).astype(BF16), vb[h]) + inter[h]
        o = o - jnp.mean(o, axis=-1, keepdims=True)
        hs = slice(h * DV, (h + 1) * DV)
        gate = g_ref[:, hs].astype(F32)
        o_ref[:, hs] = (_rms(o, gn_ref[:, hs]) * _silu(gate)).astype(o_ref.dtype)


def retention_mix(proj, gn, B, S):
    H, DK, DV, C = RET_HEADS, RET_DK, RET_DV, RET_CHUNK
    assert (DK ** -0.5) == 2.0 ** round(math.log2(DK ** -0.5))
    log_gamma = jnp.log1p(-jnp.exp2(-5.0 - jnp.arange(H, dtype=F32)))
    pos = jnp.arange(C, dtype=F32)
    diff = pos[:, None] - pos[None, :]
    d_intra = jnp.where(diff >= 0, jnp.exp(log_gamma[:, None, None] * jnp.maximum(diff, 0.0)), 0.0)
    d_q = jnp.exp(log_gamma[:, None] * (pos + 1.0))[:, :, None]
    d_k = jnp.exp(log_gamma[:, None] * (C - 1.0 - pos))[:, :, None]
    d_c = jnp.exp(log_gamma * C)[:, None, None]
    qk_w, v_w = H * DK, H * DV
    whole = lambda a: pl.BlockSpec(a.shape, lambda b, c: (0,) * a.ndim)
    return pl.pallas_call(
        _ret_kernel,
        grid=(B, S // C),
        in_specs=[
            pl.BlockSpec((None, C, qk_w), lambda b, c: (b, c, 0)),
            pl.BlockSpec((None, C, qk_w), lambda b, c: (b, c, 1)),
            pl.BlockSpec((None, C, v_w), lambda b, c: (b, c, (2 * qk_w) // v_w)),
            pl.BlockSpec((None, C, v_w), lambda b, c: (b, c, (2 * qk_w) // v_w + 1)),
            pl.BlockSpec((1, v_w), lambda b, c: (0, 0)),
            whole(d_intra), whole(d_q), whole(d_k), whole(d_c),
        ],
        out_specs=pl.BlockSpec((None, C, v_w), lambda b, c: (b, c, 0)),
        out_shape=jax.ShapeDtypeStruct((B, S, v_w), BF16),
        scratch_shapes=[pltpu.VMEM((H, DK, DV), F32)],
        compiler_params=_cparams("parallel", "arbitrary"),
        name="retention",
    )(proj, proj, proj, proj, gn.reshape(1, v_w), d_intra, d_q, d_k, d_c)


def _gla_level_sizes(C):
    sizes, s = [], 2
    while s <= C:
        sizes.append(s)
        s *= 2
    return sizes


def _gla_constants(C):
    t = np.arange(C)[:, None]
    r = np.arange(C)[None, :]
    mats = [(r <= t)]
    mats.append(r > t)
    masks = [np.eye(C, dtype=bool)]
    for bs in _gla_level_sizes(C):
        half = bs // 2
        mid = (t // bs) * bs + half - 1
        second = (t % bs) >= half
        m = np.where(second, (r > mid) & (r <= t), (r > t) & (r <= mid))
        mats.append(m)
        same = (t // bs) == (r // bs)
        masks.append(same & second & ((r % bs) < half))
    return (np.concatenate(mats, 0).astype(np.float32), np.stack(masks, 0).astype(np.float32))


def _gla_kernel(q_ref, k_ref, v_ref, r_ref, gl_ref, wg_ref, bg_ref, gn_ref, mall_ref, mask_ref, o_ref, st_ref):
    C, DK = GLA_CHUNK, GLA_DK

    H, DV = GLA_HEADS, GLA_DV
    W = H * DK

    @pl.when(pl.program_id(1) == 0)
    def _():
        st_ref[...] = jnp.zeros_like(st_ref)

    z = _dot(gl_ref[...], wg_ref[...]) + bg_ref[...]
    la = (jnp.minimum(z, 0.0) - jnp.log1p(jnp.exp(-jnp.abs(z)))) * (1.0 / GLA_TAU)
    la_hi = la.astype(BF16)
    la_lo = (la - la_hi.astype(F32)).astype(BF16)
    cum = _dot(mall_ref[0:2 * C, :], jnp.concatenate([la_hi, la_lo], axis=1))
    decay_cum = jnp.exp(cum[:, :W] + cum[:, W:])
    decay = jnp.exp(_dot(mall_ref[2 * C:, :], la_hi))
    e_b = decay_cum[0:C]
    e_rest = decay_cum[C:2 * C]
    e_last = decay_cum[C - 1:C]

    q = q_ref[...].astype(F32) * (DK ** -0.5)
    k = k_ref[...].astype(F32)
    nlev = len(_gla_level_sizes(C))
    qe = [q.astype(BF16)] + [(q * decay[l * C:(l + 1) * C]).astype(BF16) for l in range(nlev)]
    ke = [k_ref[...]] + [(k * decay[l * C:(l + 1) * C]).astype(BF16) for l in range(nlev)]
    q_in = (q * e_b).astype(BF16)
    k_out = (k * e_rest).astype(BF16)
    for h in range(H):
        ks = slice(h * DK, (h + 1) * DK)
        vs = slice(h * DV, (h + 1) * DV)
        vb = v_ref[:, vs]
        a = _dot_nt(qe[0][:, ks], ke[0][:, ks]) * mask_ref[0]
        for l in range(nlev):
            a = a + _dot_nt(qe[1 + l][:, ks], ke[1 + l][:, ks]) * mask_ref[1 + l]
        st = st_ref[h]
        o = _dot(a.astype(BF16), vb) + _dot_nt(q_in[:, ks], st.astype(BF16))
        st_ref[h] = st * e_last[:, ks] + _dot_tn(vb, k_out[:, ks])
        gate = r_ref[:, vs].astype(F32)
        o_ref[:, vs] = (_rms(o, gn_ref[:, vs]) * _silu(gate)).astype(o_ref.dtype)


def gla_mix(proj, w_gate_pad, b_gate, gn, B, S):
    H, DK, DV, C = GLA_HEADS, GLA_DK, GLA_DV, GLA_CHUNK
    mall, masks = _gla_constants(C)
    qk_w, v_w = H * DK, H * DV
    gl_blk = (2 * qk_w + 2 * v_w) // LANES
    whole = lambda shape: pl.BlockSpec(shape, lambda b, c: (0,) * len(shape))
    return pl.pallas_call(
        _gla_kernel,
        grid=(B, S // C),
        in_specs=[
            pl.BlockSpec((None, C, qk_w), lambda b, c: (b, c, 0)),
            pl.BlockSpec((None, C, qk_w), lambda b, c: (b, c, 1)),
            pl.BlockSpec((None, C, v_w), lambda b, c: (b, c, (2 * qk_w) // v_w)),
            pl.BlockSpec((None, C, v_w), lambda b, c: (b, c, (2 * qk_w) // v_w + 1)),
            pl.BlockSpec((None, C, LANES), lambda b, c: (b, c, gl_blk)),
            whole((LANES, qk_w)), whole((1, qk_w)), whole((1, v_w)), whole(mall.shape), whole(masks.shape),
        ],
        out_specs=pl.BlockSpec((None, C, v_w), lambda b, c: (b, c, 0)),
        out_shape=jax.ShapeDtypeStruct((B, S, v_w), BF16),
        scratch_shapes=[pltpu.VMEM((H, DV, DK), F32)],
        compiler_params=_cparams("parallel", "arbitrary"),
        name="gla",
    )(proj, proj, proj, proj, proj, w_gate_pad, b_gate.reshape(1, H * DK), gn.reshape(1, H * DV),
      jnp.asarray(mall, BF16), jnp.asarray(masks, F32))


def _lru_kernel(x_ref, y_ref, sh_ref, cw_ref, cb_ref, wa_ref, ba_ref, wx_ref, bx_ref, lam_ref, o_ref,
                xbuf, xc_s, a_s, u_s, hcar):
    B, TC, W = x_ref.shape
    HIST, SLAB = LRU_HIST, LRU_SLAB

    @pl.when(pl.program_id(0) == 0)
    def _():
        xbuf[...] = jnp.zeros_like(xbuf)
        hcar[...] = jnp.zeros_like(hcar)

    for b in range(B):
        xbuf[b, HIST:HIST + TC, :] = x_ref[b]
        xe = xbuf[b]
        xc = cb_ref[...] + x_ref[b].astype(F32) * cw_ref[CONV_W - 1:CONV_W, :]
        for tap in range(CONV_W - 1):
            xc = xc + _dot(sh_ref[tap], xe) * cw_ref[tap:tap + 1, :]
        xc_s[b * TC:(b + 1) * TC, :] = xc
        xbuf[b, HIST - LRU_KEEP:HIST, :] = xbuf[b, HIST + TC - LRU_KEEP:HIST + TC, :]

    xc = xc_s[...]
    xcb = xc.astype(BF16)
    for n in range(LRU_BLOCKS):
        sl = slice(n * LRU_BS, (n + 1) * LRU_BS)
        xc_s[:, sl] = _dot(xcb[:, sl], wa_ref[n])
    th_r = jnp.tanh(xc_s[...] + ba_ref[...])
    for n in range(LRU_BLOCKS):
        sl = slice(n * LRU_BS, (n + 1) * LRU_BS)
        xc_s[:, sl] = _dot(xcb[:, sl], wx_ref[n])
    th_i = jnp.tanh(xc_s[...] + bx_ref[...])
    nl = -lam_ref[...]
    softplus = jnp.maximum(nl, 0.0) + jnp.log1p(jnp.exp(-jnp.abs(nl)))
    kdec = (-0.5 * LRU_C * LOG2E) * softplus
    a = jnp.exp2(kdec * th_r + kdec)
    u = jnp.sqrt(1.0 - a * a) * (0.5 * th_i + 0.5) * xc
    NL = W // LANES
    for j in range(NL):
        for b in range(B):
            a_s[j, b * SLAB:b * SLAB + TC, :] = a[b * TC:(b + 1) * TC, j * LANES:(j + 1) * LANES]
            u_s[j, b * SLAB:b * SLAB + TC, :] = u[b * TC:(b + 1) * TC, j * LANES:(j + 1) * LANES]

    def body(t, hs):
        rows = pl.ds(t, B, stride=SLAB)
        out = []
        for j in range(NL):
            h = a_s[j, rows, :] * hs[j] + u_s[j, rows, :]
            u_s[j, rows, :] = h
            out.append(h)
        return tuple(out)

    h0 = tuple(hcar[:, j * LANES:(j + 1) * LANES] for j in range(NL))
    hT = lax.fori_loop(0, TC, body, h0, unroll=8)
    hcar[...] = jnp.concatenate(hT, axis=1)
    for b in range(B):
        hs = jnp.concatenate([u_s[j, b * SLAB:b * SLAB + TC, :] for j in range(NL)], axis=1)
        o_ref[b] = (hs * _gelu_tanh(y_ref[b].astype(F32))).astype(o_ref.dtype)


def lru_mix(proj, conv_w, conv_b, w_a, b_a, w_x, b_x, lam, B, S):
    TC, W, HIST, SLAB = LRU_TCHUNK, LRU_WIDTH, LRU_HIST, LRU_SLAB
    assert LRU_KEEP >= CONV_W - 1 and S % TC == 0
    t_idx = np.arange(TC)[:, None]
    c_idx = np.arange(HIST + TC)[None, :]
    shift = np.stack([(c_idx == HIST + t_idx - (CONV_W - 1 - tap)) for tap in range(CONV_W - 1)]).astype(np.float32)
    row = lambda a: a.reshape(1, W)
    vec = pl.BlockSpec((1, W), lambda t: (0, 0))
    blk = pl.BlockSpec((LRU_BLOCKS, LRU_BS, LRU_BS), lambda t: (0, 0, 0))
    return pl.pallas_call(
        _lru_kernel,
        grid=(S // TC,),
        in_specs=[
            pl.BlockSpec((B, TC, W), lambda t: (0, t, 0)),
            pl.BlockSpec((B, TC, W), lambda t: (0, t, 1)),
            pl.BlockSpec(shift.shape, lambda t: (0, 0, 0)),
            pl.BlockSpec((CONV_W, W), lambda t: (0, 0)),
            vec, blk, vec, blk, vec, vec,
        ],
        out_specs=pl.BlockSpec((B, TC, W), lambda t: (0, t, 0)),
        out_shape=jax.ShapeDtypeStruct((B, S, W), BF16),
        scratch_shapes=[
            pltpu.VMEM((B, HIST + TC, W), BF16),
            pltpu.VMEM((B * TC, W), F32),
            pltpu.VMEM((W // LANES, B * SLAB, LANES), F32),
            pltpu.VMEM((W // LANES, B * SLAB, LANES), F32),
            pltpu.VMEM((B, W), F32),
        ],
        compiler_params=_cparams("arbitrary"),
        name="rglru",
    )(proj, proj, jnp.asarray(shift, BF16), conv_w, row(conv_b), (0.5 * w_a).astype(BF16), row(0.5 * b_a),
      (0.5 * w_x).astype(BF16), row(0.5 * b_x), row(lam))


def _nsa_cmp_kernel(kc_ref, vc_ref, pek_ref, w1k_ref, w2k_ref, pev_ref, w1v_ref, w2v_ref, ko_ref, vo_ref,
                    buf, *, nblk):
    def compress(t_ref, pe_ref, w1_ref, w2_ref, o_ref, d):
        buf[...] = t_ref[...].astype(F32)
        ya = [jnp.zeros((nblk, CMP_HID), F32) for _ in range(NSA_GROUPS)]
        yb = [jnp.zeros((nblk, CMP_HID), F32) for _ in range(NSA_GROUPS)]
        for l in range(CMP_STRIDE):
            both = buf[pl.ds(l, nblk, stride=CMP_STRIDE), :]
            for g in range(NSA_GROUPS):
                rows = both[:, g * d:(g + 1) * d]
                lo = (rows + pe_ref[l:l + 1, :]).astype(BF16)
                hi = (rows + pe_ref[CMP_STRIDE + l:CMP_STRIDE + l + 1, :]).astype(BF16)
                ya[g] = ya[g] + _dot(lo, w1_ref[l * d:(l + 1) * d, :])
                yb[g] = yb[g] + _dot(hi, w1_ref[(CMP_STRIDE + l) * d:(CMP_STRIDE + l + 1) * d, :])
        for g in range(NSA_GROUPS):
            shifted = jnp.concatenate([yb[g][1:], jnp.zeros((1, CMP_HID), F32)], axis=0)
            hid = _gelu_tanh(ya[g] + shifted).astype(BF16)
            o_ref[g] = _dot(hid, w2_ref[...])

    compress(kc_ref, pek_ref, w1k_ref, w2k_ref, ko_ref, NSA_DK)
    compress(vc_ref, pev_ref, w1v_ref, w2v_ref, vo_ref, NSA_DV)


def nsa_compress(proj, pe_k, w1_k, w2_k, pe_v, w1_v, w2_v, B, S):
    nblk = S // CMP_STRIDE
    G = NSA_GROUPS
    kc_blk = (NSA_HEADS * NSA_DK) // LANES
    full2 = lambda a: pl.BlockSpec(a.shape, lambda b: (0, 0))
    w1k, w2k, w1v, w2v = (a.astype(BF16) for a in (w1_k, w2_k, w1_v, w2_v))
    out_sd = jax.ShapeDtypeStruct((B, G, nblk, NSA_DK), F32)
    return pl.pallas_call(
        functools.partial(_nsa_cmp_kernel, nblk=nblk),
        grid=(B,),
        in_specs=[
            pl.BlockSpec((None, S, LANES), lambda b: (b, 0, kc_blk)),
            pl.BlockSpec((None, S, LANES), lambda b: (b, 0, kc_blk + 1)),
            full2(pe_k), full2(w1k), full2(w2k), full2(pe_v), full2(w1v), full2(w2v),
        ],
        out_specs=[pl.BlockSpec((None, G, nblk, NSA_DK), lambda b: (b, 0, 0, 0))] * 2,
        out_shape=[out_sd, out_sd],
        scratch_shapes=[pltpu.VMEM((S, LANES), F32)],
        compiler_params=_cparams("parallel"),
        name="nsa_compress",
    )(proj, proj, pe_k, w1k, w2k, pe_v, w1v, w2v)


def _alibi_slope(head):
    return float(2.0 ** (-8.0 * (head + 1.0) / NSA_HEADS))


def _nsa_kernel(q_ref, gate_ref, kc_ref, vc_ref, ks_ref, vs_ref, kw_ref, vw_ref,
                ovt_ref, qt_ref, kt_ref, ct_ref, vt_ref, kind_ref, o_ref,
                qa_s, kca_s, vct_s, ksa_s, vst_s, kwa_s, vwt_s, part_s, *, seq):
    Q, TK, HPG, DK, DV, G = NSA_Q, NSA_TK, NSA_HPG, NSA_DK, NSA_DV, NSA_GROUPS
    R = HPG * Q
    W = WIN + Q
    ncmp = seq // CMP_STRIDE
    nslc = seq // SLC_L
    scale = DK ** -0.5
    qi = pl.program_id(1)
    q0 = qi * Q
    t_row = q0 + lax.broadcasted_iota(jnp.int32, (1, Q), 1)
    gates_t = _sigmoid(gate_ref[...].astype(F32)).T
    per_head = lambda a: jnp.concatenate([a] * HPG, axis=1)

    @pl.when(qi == 0)
    def _():
        for g in range(G):
            ks = slice(g * DK, (g + 1) * DK)
            vs = slice(g * DV, (g + 1) * DV)
            ksa_s[g, :, 0:LANES] = jnp.concatenate([ks_ref[:, ks], kt_ref[...]], axis=1)
            ksa_s[g, :, LANES:2 * LANES] = kind_ref[...]
            kwa_s[g] = jnp.concatenate([kw_ref[:, ks], kt_ref[...]], axis=1)
            tr = lambda a: a.astype(F32).T[0:NSA_VROWS].astype(BF16)
            vst_s[g] = tr(jnp.concatenate([vs_ref[:, vs], vt_ref[...]], axis=1))
            vwt_s[g] = tr(jnp.concatenate([vw_ref[:, vs], vt_ref[...]], axis=1))
            kca_s[g] = jnp.concatenate([kc_ref[g].astype(BF16), ct_ref[...]], axis=1)
            vct_s[g] = tr(jnp.concatenate([vc_ref[g], jnp.zeros((ncmp, NSA_TAIL), F32)], axis=1))

    for g in range(G):
        for h in range(HPG):
            col = g * HPG + h
            qh = q_ref[:, col * DK:(col + 1) * DK].astype(F32) * (scale * LOG2E)
            tail = jnp.broadcast_to(qt_ref[col:col + 1, :], (Q, NSA_TAIL))
            qa_s[g, h * Q:(h + 1) * Q, 0:LANES] = jnp.concatenate([qh, tail], axis=1).astype(BF16)

    def compressed(rows):
        cend = CMP_STRIDE * lax.broadcasted_iota(jnp.int32, (rows, 1), 0) + (CMP_L - 1)
        bias_c = per_head(jnp.where(cend <= t_row, 0.0, -NSA_BIG))
        scores_c = [_dot_nt(kca_s[g, 0:rows, :], qa_s[g, :, 0:LANES]) + bias_c for g in range(G)]
        psum = []
        for g in range(G):
            s = scores_c[g]
            e = jnp.exp2(s - jnp.max(s, axis=0, keepdims=True))
            inv = per_head(jnp.where(t_row >= CMP_L - 1, 1.0, 0.0)) / jnp.sum(e, axis=0, keepdims=True)
            p = e * inv
            ps = p[:, 0:Q]
            for h in range(1, HPG):
                ps = ps + p[:, h * Q:(h + 1) * Q]
            psum.append(jnp.concatenate([ps, jnp.zeros((ncmp - rows, Q), F32)], axis=0) if rows < ncmp else ps)
            o_cmp_t = _dot(vct_s[g, :, 0:rows], p.astype(BF16))
            for h in range(HPG):
                col = g * HPG + h
                cs = slice(h * Q, (h + 1) * Q)
                part_s[g, :, cs] = gates_t[col:col + 1] * o_cmp_t[0:DV, cs]
        return tuple(psum)

    half = ncmp // 2
    ended = (q0 + Q - CMP_L) // CMP_STRIDE + 1
    psum_t = lax.cond(ended <= half, lambda: compressed(half), lambda: compressed(ncmp))

    for g in range(G):
        imp_t = jnp.dot(ovt_ref[...], psum_t[g], preferred_element_type=F32, precision=HIGHEST)
        jcol = lax.broadcasted_iota(jnp.int32, (nslc, 1), 0)
        cur = t_row // SLC_L
        forced = (jcol == 0) | (jcol == cur) | (jcol == cur - 1)
        imp_t = jnp.where(forced, FORCE_SCORE, imp_t)
        imp_t = jnp.where(jcol <= cur, imp_t, -1.0)
        nrow = nslc // SUBLANES
        rows = [imp_t[r * SUBLANES:(r + 1) * SUBLANES] for r in range(nrow)]
        ranks = [jnp.zeros((SUBLANES, Q), F32) for _ in range(nrow)]
        jloc = lax.broadcasted_iota(jnp.int32, (SUBLANES, Q), 0)
        for j2 in range(nslc):
            other = jnp.broadcast_to(imp_t[j2:j2 + 1, :], (SUBLANES, Q))
            for r in range(nrow):
                if (r + 1) * SUBLANES - 1 <= j2:
                    ahead = jnp.where(other > rows[r], 1.0, 0.0)
                elif r * SUBLANES > j2:
                    ahead = jnp.where(other >= rows[r], 1.0, 0.0)
                else:
                    ahead = jnp.where(jloc + r * SUBLANES > j2, jnp.where(other >= rows[r], 1.0, 0.0),
                                      jnp.where(other > rows[r], 1.0, 0.0))
                ranks[r] = ranks[r] + ahead
        ntop = float(min(SLC_TOP, nslc))
        bias_t = jnp.concatenate([jnp.where(rk < ntop, 0.0, -NSA_BIG) for rk in ranks]
                                 + [jnp.zeros((LANES - nslc, Q), F32)], axis=0)
        sel_bias = bias_t.T.astype(BF16)
        for h in range(HPG):
            qa_s[g, h * Q:(h + 1) * Q, LANES:2 * LANES] = sel_bias

    state0 = tuple((jnp.full((1, R), -NSA_BIG, F32), jnp.zeros((NSA_VROWS, R), F32)) for _ in range(G))

    def sel_tile(k0, nk, state, causal):
        scores = [_dot_nt(ksa_s[g, pl.ds(k0, nk), :], qa_s[g]) for g in range(G)]
        if causal:
            pos = k0 + lax.broadcasted_iota(jnp.int32, (nk, 1), 0)
            bias = per_head(jnp.where(pos <= t_row, 0.0, -NSA_BIG))
            scores = [s + bias for s in scores]
        out = []
        for g in range(G):
            m_old, acc = state[g]
            m_new = jnp.maximum(m_old, jnp.max(scores[g], axis=0, keepdims=True))
            p = jnp.exp2(scores[g] - m_new).astype(BF16)
            acc = jnp.exp2(m_old - m_new) * acc + _dot(vst_s[g, :, pl.ds(k0, nk)], p)
            out.append((m_new, acc))
        return tuple(out)

    state = lax.fori_loop(0, q0 // TK, lambda kt, st: sel_tile(pl.multiple_of(kt * TK, TK), TK, st, False), state0)
    state = lax.cond(q0 % TK != 0, lambda st: sel_tile(pl.multiple_of(q0 - Q, Q), Q, st, False), lambda st: st,
                     state)
    state = sel_tile(pl.multiple_of(q0, Q), Q, state, True)

    w0 = pl.multiple_of(jnp.maximum(q0 - WIN, 0), Q)
    wpos = w0 + lax.broadcasted_iota(jnp.int32, (W, 1), 0)
    dist_w = t_row - wpos
    bias_w = per_head(jnp.where((dist_w >= 0) & (dist_w < WIN), 0.0, -NSA_BIG))
    scores_w = [_dot_nt(kwa_s[g, pl.ds(w0, W), :], qa_s[g, :, 0:LANES]) + bias_w for g in range(G)]
    for g in range(G):
        s = scores_w[g]
        e = jnp.exp2(s - jnp.max(s, axis=0, keepdims=True))
        pv_w = _dot(vwt_s[g, :, pl.ds(w0, W)], e.astype(BF16))
        o_win_t = pv_w[0:DV] / pv_w[DV:DV + 1]
        for h in range(HPG):
            col = g * HPG + h
            cs = slice(h * Q, (h + 1) * Q)
            part_s[g, :, cs] += gates_t[2 * NSA_HEADS + col:2 * NSA_HEADS + col + 1] * o_win_t[:, cs]

    for g in range(G):
        acc = state[g][1]
        o_sel_t = acc[0:DV] / acc[DV:DV + 1]
        outs = []
        for h in range(HPG):
            col = g * HPG + h
            cs = slice(h * Q, (h + 1) * Q)
            outs.append(part_s[g, :, cs] + gates_t[NSA_HEADS + col:NSA_HEADS + col + 1] * o_sel_t[:, cs])
        for h in range(0, HPG, 2):
            col = g * HPG + h
            o_ref[:, col * DV:(col + 2) * DV] = jnp.concatenate([outs[h], outs[h + 1]], axis=0).T.astype(o_ref.dtype)


def _bf16_split3(x):
    x = np.asarray(x, np.float32)
    rnd = lambda a: a.astype(BF16).astype(np.float32)
    hi = rnd(x)
    mid = rnd(x - hi)
    lo = rnd(x - hi - mid)
    return hi, mid, lo


def _pos_tail(pos):
    pos = np.asarray(pos)
    hi, lo = (pos // SLC_L) * SLC_L, pos % SLC_L
    tail = np.zeros((pos.shape[0], NSA_TAIL), np.float32)
    for c in range(3):
        tail[:, 2 * c] = hi
        tail[:, 2 * c + 1] = lo
    return tail


def nsa_attend(proj, k_cmp, v_cmp, B, S):
    Q, G, HPG = NSA_Q, NSA_GROUPS, NSA_HPG
    ncmp, nslc = S // CMP_STRIDE, S // SLC_L
    assert S >= WIN + Q and S % NSA_TK == 0 and nslc <= LANES and nslc % SUBLANES == 0 and NSA_TK == 2 * Q
    n = np.arange(ncmp)[:, None]
    j = np.arange(nslc)[None, :]
    overlap = ((n * CMP_STRIDE < (j + 1) * SLC_L) & (n * CMP_STRIDE + CMP_L > j * SLC_L)
               & (n < (S - CMP_L) // CMP_STRIDE + 1)).astype(np.float32)
    slopes = np.asarray([_alibi_slope(h) for h in range(NSA_HEADS)], np.float32) * np.float32(LOG2E)
    qtail = np.zeros((NSA_HEADS, NSA_TAIL), np.float32)
    for c, part in enumerate(_bf16_split3(slopes)):
        qtail[:, 2 * c] = part
        qtail[:, 2 * c + 1] = part
    ktail = _pos_tail(np.arange(S))
    ctail = _pos_tail(np.arange(ncmp) * CMP_STRIDE + CMP_L - 1)
    vtail = np.zeros((S, NSA_TAIL), np.float32)
    vtail[:, 0] = 1.0
    kind = (np.arange(S)[:, None] // SLC_L == np.arange(LANES)[None, :]).astype(np.float32)
    qw = NSA_HEADS * NSA_DK
    kv0 = qw // LANES
    seqblk = lambda c: pl.BlockSpec((None, S, LANES), lambda b, i: (b, 0, c))
    cmpblk = pl.BlockSpec((None, G, ncmp, NSA_DK), lambda b, i: (b, 0, 0, 0))
    const = lambda a: pl.BlockSpec(a.shape, lambda b, i: (0, 0))
    consts = [jnp.asarray(overlap.T, F32), jnp.asarray(qtail, F32), jnp.asarray(ktail, BF16),
              jnp.asarray(ctail, BF16), jnp.asarray(vtail, BF16), jnp.asarray(kind, BF16)]
    return pl.pallas_call(
        functools.partial(_nsa_kernel, seq=S),
        grid=(B, S // Q),
        in_specs=[
            pl.BlockSpec((None, Q, qw), lambda b, i: (b, i, 0)),
            pl.BlockSpec((None, Q, LANES), lambda b, i: (b, i, kv0 + 6)),
            cmpblk, cmpblk,
            seqblk(kv0 + 2), seqblk(kv0 + 3), seqblk(kv0 + 4), seqblk(kv0 + 5),
        ] + [const(a) for a in consts],
        out_specs=pl.BlockSpec((None, Q, NSA_HEADS * NSA_DV), lambda b, i: (b, i, 0)),
        out_shape=jax.ShapeDtypeStruct((B, S, NSA_HEADS * NSA_DV), BF16),
        scratch_shapes=[
            pltpu.VMEM((G, HPG * Q, 2 * LANES), BF16),
            pltpu.VMEM((G, ncmp, LANES), BF16),
            pltpu.VMEM((G, NSA_VROWS, ncmp), BF16),
            pltpu.VMEM((G, S, 2 * LANES), BF16),
            pltpu.VMEM((G, NSA_VROWS, S), BF16),
            pltpu.VMEM((G, S, LANES), BF16),
            pltpu.VMEM((G, NSA_VROWS, S), BF16),
            pltpu.VMEM((G, NSA_DV, HPG * Q), F32),
        ],
        compiler_params=_cparams("parallel", "arbitrary"),
        name="nsa_attend",
    )(proj, proj, k_cmp, v_cmp, proj, proj, proj, proj, *consts)


def _pad_cols(w, n):
    return jnp.pad(w, ((0, 0), (0, n - w.shape[1])))


def kernel(x, norm_mix_pre, norm_mix_post, norm_mlp_pre, norm_mlp_post, mlp_w_up, mlp_w_down, ret_w_in, ret_gn, ret_w_out, gla_w_in, gla_w_gate_up, gla_b_gate, gla_gn, gla_w_out, lru_w_in, lru_conv_w, lru_conv_b, lru_w_a, lru_b_a, lru_w_x, lru_b_x, lru_lambda, lru_w_out, nsa_w_in, nsa_pe_k, nsa_w1_k, nsa_w2_k, nsa_pe_v, nsa_w1_v, nsa_w2_v, nsa_w_out):
    B, S, D = x.shape
    T = B * S
    depth = norm_mix_pre.shape[0]
    x2 = x.reshape(T, D)
    h2 = None
    wu_all, wd_all = mlp_w_up.astype(BF16), mlp_w_down.astype(BF16)
    for i in range(depth):
        m, j = i % 4, i // 4

        def project(w):
            w = _pad_cols(w, -(-w.shape[1] // LANES) * LANES).astype(BF16)
            tn = _proj_tn(w.shape[1])
            if h2 is not None:
                return in_proj(h2, w, tn)
            return norm_proj(x2, norm_mix_pre[i], w, tn)

        if m == 0:
            proj = project(ret_w_in[j])
            mix = retention_mix(proj.reshape(B, S, -1), ret_gn[j], B, S)
            w_out = ret_w_out[j]
        elif m == 1:
            proj = project(gla_w_in[j])
            w_gate_pad = jnp.pad(gla_w_gate_up[j], ((0, LANES - GLA_RANK), (0, 0))).astype(BF16)
            mix = gla_mix(proj.reshape(B, S, -1), w_gate_pad, gla_b_gate[j], gla_gn[j], B, S)
            w_out = gla_w_out[j]
        elif m == 2:
            proj = project(lru_w_in[j])
            mix = lru_mix(proj.reshape(B, S, -1), lru_conv_w[j], lru_conv_b[j], lru_w_a[j], lru_b_a[j],
                          lru_w_x[j], lru_b_x[j], lru_lambda[j], B, S)
            w_out = lru_w_out[j]
        else:
            proj = project(nsa_w_in[j]).reshape(B, S, -1)
            k_cmp, v_cmp = nsa_compress(proj, nsa_pe_k[j], nsa_w1_k[j], nsa_w2_k[j],
                                        nsa_pe_v[j], nsa_w1_v[j], nsa_w2_v[j], B, S)
            mix = nsa_attend(proj, k_cmp, v_cmp, B, S)
            w_out = nsa_w_out[j]
        g_next = norm_mix_pre[i + 1] if i + 1 < depth else None
        res = sublayer_tail(mix.reshape(T, -1), w_out.astype(BF16), norm_mix_post[i], x2, norm_mlp_pre[i],
                            wu_all, wd_all, i, norm_mlp_post[i], g_next)
        x2, h2 = res if g_next is not None else (res, None)
    return x2.reshape(B, S, D)
```

```python
import functools
import math

import numpy as np
import jax
import jax.numpy as jnp
from jax import lax
from jax.experimental import pallas as pl
from jax.experimental.pallas import tpu as pltpu

F32 = jnp.float32
BF16 = jnp.bfloat16
HIGHEST = lax.Precision.HIGHEST

LANES = 128
SUBLANES = 8

D_MODEL = 1024
D_FF = 4 * D_MODEL
NORM_EPS = 1e-6

RET_HEADS, RET_DK, RET_DV = 4, 256, 512
RET_CHUNK = 256

GLA_HEADS, GLA_DK, GLA_DV = 4, 128, 256
GLA_RANK = 16
GLA_TAU = 16.0
GLA_CHUNK = 128

LRU_WIDTH = D_MODEL
LRU_BLOCKS = 8
LRU_BS = LRU_WIDTH // LRU_BLOCKS
CONV_W = 4
LRU_C = 8.0
LRU_TCHUNK = 128
LRU_HIST = 128
LRU_KEEP = 16
LRU_SLAB = LRU_TCHUNK + SUBLANES

NSA_HEADS, NSA_GROUPS, NSA_HPG = 16, 2, 8
NSA_DK = 64
NSA_DV = 64
CMP_L, CMP_STRIDE = 32, 16
CMP_HID = 2 * NSA_DK
SLC_L, SLC_TOP = 64, 16
WIN = 512
FORCE_SCORE = 1e6
NSA_Q = 256
NSA_TK = 512
NSA_TAIL = 64
NSA_VROWS = NSA_DV + 16
NSA_BIG = 1e30
LOG2E = math.log2(math.e)

VMEM_LIMIT = 56 * 1024 * 1024


def _cparams(*sem):
    return pltpu.CompilerParams(dimension_semantics=sem, vmem_limit_bytes=VMEM_LIMIT)


def _rms(x, g):
    return x * lax.rsqrt(jnp.mean(x * x, axis=-1, keepdims=True) + NORM_EPS) * g


def _dot(a, b):
    return jnp.dot(a, b, preferred_element_type=F32)


def _dot_nt(a, b):
    return lax.dot_general(a, b, (((1,), (1,)), ((), ())), preferred_element_type=F32)


def _dot_tn(a, b):
    return lax.dot_general(a, b, (((0,), (0,)), ((), ())), preferred_element_type=F32)


def _gelu_tanh(x):
    return 0.5 * x * (1.0 + jnp.tanh(math.sqrt(2.0 / math.pi) * (x + 0.044715 * (x * x * x))))


def _sigmoid(x):
    return 0.5 * (jnp.tanh(0.5 * x) + 1.0)


def _silu(x):
    t = 0.5 * x
    return t * jnp.tanh(t) + t


PROJ_TN_MAX = 3200


def _proj_tn(n):
    return max(t for t in range(LANES, min(n, PROJ_TN_MAX) + 1, LANES) if n % t == 0)


def _norm_proj_kernel(x_ref, g_ref, w_ref, o_ref, h_ref):
    @pl.when(pl.program_id(1) == 0)
    def _():
        h_ref[...] = _rms(x_ref[...], g_ref[...]).astype(BF16)

    o_ref[...] = _dot(h_ref[...], w_ref[...]).astype(o_ref.dtype)


def norm_proj(x2d, g, w_bf16, tn, tm=1024):
    T, D = x2d.shape
    N = w_bf16.shape[1]
    assert T % tm == 0 and N % tn == 0
    return pl.pallas_call(
        _norm_proj_kernel,
        grid=(T // tm, N // tn),
        in_specs=[
            pl.BlockSpec((tm, D), lambda i, j: (i, 0)),
            pl.BlockSpec((1, D), lambda i, j: (0, 0)),
            pl.BlockSpec((D, tn), lambda i, j: (0, j)),
        ],
        out_specs=pl.BlockSpec((tm, tn), lambda i, j: (i, j)),
        out_shape=jax.ShapeDtypeStruct((T, N), BF16),
        scratch_shapes=[pltpu.VMEM((tm, D), BF16)],
        compiler_params=_cparams("parallel", "arbitrary"),
        name="norm_proj",
    )(x2d, g.reshape(1, D), w_bf16)


def _proj_kernel(h_ref, w_ref, o_ref):
    o_ref[...] = _dot(h_ref[...], w_ref[...]).astype(o_ref.dtype)


def in_proj(h2d, w_bf16, tn, tm=1024):
    T, D = h2d.shape
    N = w_bf16.shape[1]
    assert T % tm == 0 and N % tn == 0
    return pl.pallas_call(
        _proj_kernel,
        grid=(T // tm, N // tn),
        in_specs=[pl.BlockSpec((tm, D), lambda i, j: (i, 0)), pl.BlockSpec((D, tn), lambda i, j: (0, j))],
        out_specs=pl.BlockSpec((tm, tn), lambda i, j: (i, j)),
        out_shape=jax.ShapeDtypeStruct((T, N), BF16),
        compiler_params=_cparams("parallel", "parallel"),
        name="in_proj",
    )(h2d, w_bf16)


MLP_FCHUNK = 512


def _tail_kernel(a_ref, wo_ref, gp_ref, x_ref, g1_ref, wu_ref, wd_ref, g2_ref, *rest):
    o_ref = rest[-2] if len(rest) == 3 else rest[0]
    x = x_ref[...] + _rms(_dot(a_ref[...], wo_ref[...]), gp_ref[...])
    h = _rms(x, g1_ref[...]).astype(BF16)
    acc = jnp.zeros(x.shape, F32)
    for c in range(D_FF // MLP_FCHUNK):
        sl = slice(c * MLP_FCHUNK, (c + 1) * MLP_FCHUNK)
        u = jnp.maximum(_dot(h, wu_ref[:, sl]), 0.0)
        acc = acc + _dot((u * u).astype(BF16), wd_ref[sl, :])
    out = x + _rms(acc, g2_ref[...])
    o_ref[...] = out
    if len(rest) == 3:
        gn_ref, _, hn_ref = rest
        hn_ref[...] = _rms(out, gn_ref[...]).astype(hn_ref.dtype)


def sublayer_tail(a2d, wo_bf16, g_post, x2d, g1, wu_all, wd_all, layer, g2, g_next=None, tm=512):
    T, D = x2d.shape
    K = a2d.shape[1]
    F = wu_all.shape[2]
    resident = lambda shape: pl.BlockSpec(shape, lambda i: (0, 0), pipeline_mode=pl.Buffered(1))
    of_layer = lambda shape: pl.BlockSpec((None,) + shape, lambda i: (layer, 0, 0), pipeline_mode=pl.Buffered(1))
    tile = pl.BlockSpec((tm, D), lambda i: (i, 0))
    row = lambda a: a.reshape(1, D)
    in_specs = [pl.BlockSpec((tm, K), lambda i: (i, 0)), resident((K, D)), resident((1, D)), tile,
                resident((1, D)), of_layer((D, F)), of_layer((F, D)), resident((1, D))]
    args = [a2d, wo_bf16, row(g_post), x2d, row(g1), wu_all, wd_all, row(g2)]
    out_specs, out_shape = tile, jax.ShapeDtypeStruct((T, D), F32)
    if g_next is not None:
        in_specs.append(resident((1, D)))
        args.append(row(g_next))
        out_specs, out_shape = [tile, tile], [out_shape, jax.ShapeDtypeStruct((T, D), BF16)]
    return pl.pallas_call(
        _tail_kernel,
        grid=(T // tm,),
        in_specs=in_specs,
        out_specs=out_specs,
        out_shape=out_shape,
        compiler_params=_cparams("parallel"),
        name="sublayer_tail",
    )(*args)


def _ret_kernel(q_ref, k_ref, v_ref, g_ref, gn_ref, di_ref, dq_ref, dk_ref, dc_ref, o_ref, st_ref):
    H, DK, DV = RET_HEADS, RET_DK, RET_DV

    @pl.when(pl.program_id(1) == 0)
    def _():
        st_ref[...] = jnp.zeros_like(st_ref)

    qb = [q_ref[:, h * DK:(h + 1) * DK] for h in range(H)]
    kb = [k_ref[:, h * DK:(h + 1) * DK] * (DK ** -0.5) for h in range(H)]
    vb = [v_ref[:, h * DV:(h + 1) * DV] for h in range(H)]
    st = [st_ref[h] for h in range(H)]
    scores = [_dot_nt(qb[h], kb[h]) for h in range(H)]
    inter = [_dot((qb[h].astype(F32) * dq_ref[h]).astype(BF16), st[h].astype(BF16)) for h in range(H)]
    for h in range(H):
        st_ref[h] = st[h] * dc_ref[h] + _dot_tn((kb[h].astype(F32) * dk_ref[h]).astype(BF16), vb[h])
    for h in range(H):
        o = _dot((scores[h] * di_ref[h]).astype(BF16), vb[h]) + inter[h]
        o = o - jnp.mean(o, axis=-1, keepdims=True)
        hs = slice(h * DV, (h + 1) * DV)
        gate = g_ref[:, hs].astype(F32)
        o_ref[:, hs] = (_rms(o, gn_ref[:, hs]) * _silu(gate)).astype(o_ref.dtype)


def retention_mix(proj, gn, B, S):
    H, DK, DV, C = RET_HEADS, RET_DK, RET_DV, RET_CHUNK
    assert (DK ** -0.5) == 2.0 ** round(math.log2(DK ** -0.5))
    log_gamma = jnp.log1p(-jnp.exp2(-5.0 - jnp.arange(H, dtype=F32)))
    pos = jnp.arange(C, dtype=F32)
    diff = pos[:, None] - pos[None, :]
    d_intra = jnp.where(diff >= 0, jnp.exp(log_gamma[:, None, None] * jnp.maximum(diff, 0.0)), 0.0)
    d_q = jnp.exp(log_gamma[:, None] * (pos + 1.0))[:, :, None]
    d_k = jnp.exp(log_gamma[:, None] * (C - 1.0 - pos))[:, :, None]
    d_c = jnp.exp(log_gamma * C)[:, None, None]
    qk_w, v_w = H * DK, H * DV
    whole = lambda a: pl.BlockSpec(a.shape, lambda b, c: (0,) * a.ndim)
    return pl.pallas_call(
        _ret_kernel,
        grid=(B, S // C),
        in_specs=[
            pl.BlockSpec((None, C, qk_w), lambda b, c: (b, c, 0)),
            pl.BlockSpec((None, C, qk_w), lambda b, c: (b, c, 1)),
            pl.BlockSpec((None, C, v_w), lambda b, c: (b, c, (2 * qk_w) // v_w)),
            pl.BlockSpec((None, C, v_w), lambda b, c: (b, c, (2 * qk_w) // v_w + 1)),
            pl.BlockSpec((1, v_w), lambda b, c: (0, 0)),
            whole(d_intra), whole(d_q), whole(d_k), whole(d_c),
        ],
        out_specs=pl.BlockSpec((None, C, v_w), lambda b, c: (b, c, 0)),
        out_shape=jax.ShapeDtypeStruct((B, S, v_w), BF16),
        scratch_shapes=[pltpu.VMEM((H, DK, DV), F32)],
        compiler_params=_cparams("parallel", "arbitrary"),
        name="retention",
    )(proj, proj, proj, proj, gn.reshape(1, v_w), d_intra, d_q, d_k, d_c)


def _gla_level_sizes(C):
    sizes, s = [], 2
    while s <= C:
        sizes.append(s)
        s *= 2
    return sizes


def _gla_constants(C):
    t = np.arange(C)[:, None]
    r = np.arange(C)[None, :]
    mats = [(r <= t)]
    mats.append(r > t)
    masks = [np.eye(C, dtype=bool)]
    for bs in _gla_level_sizes(C):
        half = bs // 2
        mid = (t // bs) * bs + half - 1
        second = (t % bs) >= half
        m = np.where(second, (r > mid) & (r <= t), (r > t) & (r <= mid))
        mats.append(m)
        same = (t // bs) == (r // bs)
        masks.append(same & second & ((r % bs) < half))
    return (np.concatenate(mats, 0).astype(np.float32), np.stack(masks, 0).astype(np.float32))


def _gla_kernel(q_ref, k_ref, v_ref, r_ref, gl_ref, wg_ref, bg_ref, gn_ref, mall_ref, mask_ref, o_ref, st_ref):
    C, DK = GLA_CHUNK, GLA_DK

    H, DV = GLA_HEADS, GLA_DV
    W = H * DK

    @pl.when(pl.program_id(1) == 0)
    def _():
        st_ref[...] = jnp.zeros_like(st_ref)

    z = _dot(gl_ref[...], wg_ref[...]) + bg_ref[...]
    la = (jnp.minimum(z, 0.0) - jnp.log1p(jnp.exp(-jnp.abs(z)))) * (1.0 / GLA_TAU)
    la_hi = la.astype(BF16)
    la_lo = (la - la_hi.astype(F32)).astype(BF16)
    cum = _dot(mall_ref[0:2 * C, :], jnp.concatenate([la_hi, la_lo], axis=1))
    decay_cum = jnp.exp(cum[:, :W] + cum[:, W:])
    decay = jnp.exp(_dot(mall_ref[2 * C:, :], la_hi))
    e_b = decay_cum[0:C]
    e_rest = decay_cum[C:2 * C]
    e_last = decay_cum[C - 1:C]

    q = q_ref[...].astype(F32) * (DK ** -0.5)
    k = k_ref[...].astype(F32)
    nlev = len(_gla_level_sizes(C))
    qe = [q.astype(BF16)] + [(q * decay[l * C:(l + 1) * C]).astype(BF16) for l in range(nlev)]
    ke = [k_ref[...]] + [(k * decay[l * C:(l + 1) * C]).astype(BF16) for l in range(nlev)]
    q_in = (q * e_b).astype(BF16)
    k_out = (k * e_rest).astype(BF16)
    for h in range(H):
        ks = slice(h * DK, (h + 1) * DK)
        vs = slice(h * DV, (h + 1) * DV)
        vb = v_ref[:, vs]
        a = _dot_nt(qe[0][:, ks], ke[0][:, ks]) * mask_ref[0]
        for l in range(nlev):
            a = a + _dot_nt(qe[1 + l][:, ks], ke[1 + l][:, ks]) * mask_ref[1 + l]
        st = st_ref[h]
        o = _dot(a.astype(BF16), vb) + _dot_nt(q_in[:, ks], st.astype(BF16))
        st_ref[h] = st * e_last[:, ks] + _dot_tn(vb, k_out[:, ks])
        gate = r_ref[:, vs].astype(F32)
        o_ref[:, vs] = (_rms(o, gn_ref[:, vs]) * _silu(gate)).astype(o_ref.dtype)


def gla_mix(proj, w_gate_pad, b_gate, gn, B, S):
    H, DK, DV, C = GLA_HEADS, GLA_DK, GLA_DV, GLA_CHUNK
    mall, masks = _gla_constants(C)
    qk_w, v_w = H * DK, H * DV
    gl_blk = (2 * qk_w + 2 * v_w) // LANES
    whole = lambda shape: pl.BlockSpec(shape, lambda b, c: (0,) * len(shape))
    return pl.pallas_call(
        _gla_kernel,
        grid=(B, S // C),
        in_specs=[
            pl.BlockSpec((None, C, qk_w), lambda b, c: (b, c, 0)),
            pl.BlockSpec((None, C, qk_w), lambda b, c: (b, c, 1)),
            pl.BlockSpec((None, C, v_w), lambda b, c: (b, c, (2 * qk_w) // v_w)),
            pl.BlockSpec((None, C, v_w), lambda b, c: (b, c, (2 * qk_w) // v_w + 1)),
            pl.BlockSpec((None, C, LANES), lambda b, c: (b, c, gl_blk)),
            whole((LANES, qk_w)), whole((1, qk_w)), whole((1, v_w)), whole(mall.shape), whole(masks.shape),
        ],
        out_specs=pl.BlockSpec((None, C, v_w), lambda b, c: (b, c, 0)),
        out_shape=jax.ShapeDtypeStruct((B, S, v_w), BF16),
        scratch_shapes=[pltpu.VMEM((H, DV, DK), F32)],
        compiler_params=_cparams("parallel", "arbitrary"),
        name="gla",
    )(proj, proj, proj, proj, proj, w_gate_pad, b_gate.reshape(1, H * DK), gn.reshape(1, H * DV),
      jnp.asarray(mall, BF16), jnp.asarray(masks, F32))


def _lru_kernel(x_ref, y_ref, sh_ref, cw_ref, cb_ref, wa_ref, ba_ref, wx_ref, bx_ref, lam_ref, o_ref,
                xbuf, xc_s, a_s, u_s, hcar):
    B, TC, W = x_ref.shape
    HIST, SLAB = LRU_HIST, LRU_SLAB

    @pl.when(pl.program_id(0) == 0)
    def _():
        xbuf[...] = jnp.zeros_like(xbuf)
        hcar[...] = jnp.zeros_like(hcar)

    for b in range(B):
        xbuf[b, HIST:HIST + TC, :] = x_ref[b]
        xe = xbuf[b]
        xc = cb_ref[...] + x_ref[b].astype(F32) * cw_ref[CONV_W - 1:CONV_W, :]
        for tap in range(CONV_W - 1):
            xc = xc + _dot(sh_ref[tap], xe) * cw_ref[tap:tap + 1, :]
        xc_s[b * TC:(b + 1) * TC, :] = xc
        xbuf[b, HIST - LRU_KEEP:HIST, :] = xbuf[b, HIST + TC - LRU_KEEP:HIST + TC, :]

    xc = xc_s[...]
    xcb = xc.astype(BF16)
    for n in range(LRU_BLOCKS):
        sl = slice(n * LRU_BS, (n + 1) * LRU_BS)
        xc_s[:, sl] = _dot(xcb[:, sl], wa_ref[n])
    th_r = jnp.tanh(xc_s[...] + ba_ref[...])
    for n in range(LRU_BLOCKS):
        sl = slice(n * LRU_BS, (n + 1) * LRU_BS)
        xc_s[:, sl] = _dot(xcb[:, sl], wx_ref[n])
    th_i = jnp.tanh(xc_s[...] + bx_ref[...])
    nl = -lam_ref[...]
    softplus = jnp.maximum(nl, 0.0) + jnp.log1p(jnp.exp(-jnp.abs(nl)))
    kdec = (-0.5 * LRU_C * LOG2E) * softplus
    a = jnp.exp2(kdec * th_r + kdec)
    u = jnp.sqrt(1.0 - a * a) * (0.5 * th_i + 0.5) * xc
    NL = W // LANES
    for j in range(NL):
        for b in range(B):
            a_s[j, b * SLAB:b * SLAB + TC, :] = a[b * TC:(b + 1) * TC, j * LANES:(j + 1) * LANES]
            u_s[j, b * SLAB:b * SLAB + TC, :] = u[b * TC:(b + 1) * TC, j * LANES:(j + 1) * LANES]

    def body(t, hs):
        rows = pl.ds(t, B, stride=SLAB)
        out = []
        for j in range(NL):
            h = a_s[j, rows, :] * hs[j] + u_s[j, rows, :]
            u_s[j, rows, :] = h
            out.append(h)
        return tuple(out)

    h0 = tuple(hcar[:, j * LANES:(j + 1) * LANES] for j in range(NL))
    hT = lax.fori_loop(0, TC, body, h0, unroll=8)
    hcar[...] = jnp.concatenate(hT, axis=1)
    for b in range(B):
        hs = jnp.concatenate([u_s[j, b * SLAB:b * SLAB + TC, :] for j in range(NL)], axis=1)
        o_ref[b] = (hs * _gelu_tanh(y_ref[b].astype(F32))).astype(o_ref.dtype)


def lru_mix(proj, conv_w, conv_b, w_a, b_a, w_x, b_x, lam, B, S):
    TC, W, HIST, SLAB = LRU_TCHUNK, LRU_WIDTH, LRU_HIST, LRU_SLAB
    assert LRU_KEEP >= CONV_W - 1 and S % TC == 0
    t_idx = np.arange(TC)[:, None]
    c_idx = np.arange(HIST + TC)[None, :]
    shift = np.stack([(c_idx == HIST + t_idx - (CONV_W - 1 - tap)) for tap in range(CONV_W - 1)]).astype(np.float32)
    row = lambda a: a.reshape(1, W)
    vec = pl.BlockSpec((1, W), lambda t: (0, 0))
    blk = pl.BlockSpec((LRU_BLOCKS, LRU_BS, LRU_BS), lambda t: (0, 0, 0))
    return pl.pallas_call(
        _lru_kernel,
        grid=(S // TC,),
        in_specs=[
            pl.BlockSpec((B, TC, W), lambda t: (0, t, 0)),
            pl.BlockSpec((B, TC, W), lambda t: (0, t, 1)),
            pl.BlockSpec(shift.shape, lambda t: (0, 0, 0)),
            pl.BlockSpec((CONV_W, W), lambda t: (0, 0)),
            vec, blk, vec, blk, vec, vec,
        ],
        out_specs=pl.BlockSpec((B, TC, W), lambda t: (0, t, 0)),
        out_shape=jax.ShapeDtypeStruct((B, S, W), BF16),
        scratch_shapes=[
            pltpu.VMEM((B, HIST + TC, W), BF16),
            pltpu.VMEM((B * TC, W), F32),
            pltpu.VMEM((W // LANES, B * SLAB, LANES), F32),
            pltpu.VMEM((W // LANES, B * SLAB, LANES), F32),
            pltpu.VMEM((B, W), F32),
        ],
        compiler_params=_cparams("arbitrary"),
        name="rglru",
    )(proj, proj, jnp.asarray(shift, BF16), conv_w, row(conv_b), (0.5 * w_a).astype(BF16), row(0.5 * b_a),
      (0.5 * w_x).astype(BF16), row(0.5 * b_x), row(lam))


def _nsa_cmp_kernel(kc_ref, vc_ref, pek_ref, w1k_ref, w2k_ref, pev_ref, w1v_ref, w2v_ref, ko_ref, vo_ref,
                    buf, *, nblk):
    def compress(t_ref, pe_ref, w1_ref, w2_ref, o_ref, d):
        buf[...] = t_ref[...].astype(F32)
        ya = [jnp.zeros((nblk, CMP_HID), F32) for _ in range(NSA_GROUPS)]
        yb = [jnp.zeros((nblk, CMP_HID), F32) for _ in range(NSA_GROUPS)]
        for l in range(CMP_STRIDE):
            both = buf[pl.ds(l, nblk, stride=CMP_STRIDE), :]
            for g in range(NSA_GROUPS):
                rows = both[:, g * d:(g + 1) * d]
                lo = (rows + pe_ref[l:l + 1, :]).astype(BF16)
                hi = (rows + pe_ref[CMP_STRIDE + l:CMP_STRIDE + l + 1, :]).astype(BF16)
                ya[g] = ya[g] + _dot(lo, w1_ref[l * d:(l + 1) * d, :])
                yb[g] = yb[g] + _dot(hi, w1_ref[(CMP_STRIDE + l) * d:(CMP_STRIDE + l + 1) * d, :])
        for g in range(NSA_GROUPS):
            shifted = jnp.concatenate([yb[g][1:], jnp.zeros((1, CMP_HID), F32)], axis=0)
            hid = _gelu_tanh(ya[g] + shifted).astype(BF16)
            o_ref[g] = _dot(hid, w2_ref[...])

    compress(kc_ref, pek_ref, w1k_ref, w2k_ref, ko_ref, NSA_DK)
    compress(vc_ref, pev_ref, w1v_ref, w2v_ref, vo_ref, NSA_DV)


def nsa_compress(proj, pe_k, w1_k, w2_k, pe_v, w1_v, w2_v, B, S):
    nblk = S // CMP_STRIDE
    G = NSA_GROUPS
    kc_blk = (NSA_HEADS * NSA_DK) // LANES
    full2 = lambda a: pl.BlockSpec(a.shape, lambda b: (0, 0))
    w1k, w2k, w1v, w2v = (a.astype(BF16) for a in (w1_k, w2_k, w1_v, w2_v))
    out_sd = jax.ShapeDtypeStruct((B, G, nblk, NSA_DK), F32)
    return pl.pallas_call(
        functools.partial(_nsa_cmp_kernel, nblk=nblk),
        grid=(B,),
        in_specs=[
            pl.BlockSpec((None, S, LANES), lambda b: (b, 0, kc_blk)),
            pl.BlockSpec((None, S, LANES), lambda b: (b, 0, kc_blk + 1)),
            full2(pe_k), full2(w1k), full2(w2k), full2(pe_v), full2(w1v), full2(w2v),
        ],
        out_specs=[pl.BlockSpec((None, G, nblk, NSA_DK), lambda b: (b, 0, 0, 0))] * 2,
        out_shape=[out_sd, out_sd],
        scratch_shapes=[pltpu.VMEM((S, LANES), F32)],
        compiler_params=_cparams("parallel"),
        name="nsa_compress",
    )(proj, proj, pe_k, w1k, w2k, pe_v, w1v, w2v)


def _alibi_slope(head):
    return float(2.0 ** (-8.0 * (head + 1.0) / NSA_HEADS))


def _nsa_kernel(q_ref, gate_ref, kc_ref, vc_ref, ks_ref, vs_ref, kw_ref, vw_ref,
                ovt_ref, qt_ref, kt_ref, ct_ref, vt_ref, kind_ref, o_ref,
                qa_s, kca_s, vct_s, ksa_s, vst_s, kwa_s, vwt_s, part_s, *, seq):
    Q, TK, HPG, DK, DV, G = NSA_Q, NSA_TK, NSA_HPG, NSA_DK, NSA_DV, NSA_GROUPS
    R = HPG * Q
    W = WIN + Q
    ncmp = seq // CMP_STRIDE
    nslc = seq // SLC_L
    scale = DK ** -0.5
    qi = pl.program_id(1)
    q0 = qi * Q
    t_row = q0 + lax.broadcasted_iota(jnp.int32, (1, Q), 1)
    gates_t = _sigmoid(gate_ref[...].astype(F32)).T
    per_head = lambda a: jnp.concatenate([a] * HPG, axis=1)

    @pl.when(qi == 0)
    def _():
        for g in range(G):
            ks = slice(g * DK, (g + 1) * DK)
            vs = slice(g * DV, (g + 1) * DV)
            ksa_s[g, :, 0:LANES] = jnp.concatenate([ks_ref[:, ks], kt_ref[...]], axis=1)
            ksa_s[g, :, LANES:2 * LANES] = kind_ref[...]
            kwa_s[g] = jnp.concatenate([kw_ref[:, ks], kt_ref[...]], axis=1)
            tr = lambda a: a.astype(F32).T[0:NSA_VROWS].astype(BF16)
            vst_s[g] = tr(jnp.concatenate([vs_ref[:, vs], vt_ref[...]], axis=1))
            vwt_s[g] = tr(jnp.concatenate([vw_ref[:, vs], vt_ref[...]], axis=1))
            kca_s[g] = jnp.concatenate([kc_ref[g].astype(BF16), ct_ref[...]], axis=1)
            vct_s[g] = tr(jnp.concatenate([vc_ref[g], jnp.zeros((ncmp, NSA_TAIL), F32)], axis=1))

    for g in range(G):
        for h in range(HPG):
            col = g * HPG + h
            qh = q_ref[:, col * DK:(col + 1) * DK].astype(F32) * (scale * LOG2E)
            tail = jnp.broadcast_to(qt_ref[col:col + 1, :], (Q, NSA_TAIL))
            qa_s[g, h * Q:(h + 1) * Q, 0:LANES] = jnp.concatenate([qh, tail], axis=1).astype(BF16)

    def compressed(rows):
        cend = CMP_STRIDE * lax.broadcasted_iota(jnp.int32, (rows, 1), 0) + (CMP_L - 1)
        bias_c = per_head(jnp.where(cend <= t_row, 0.0, -NSA_BIG))
        scores_c = [_dot_nt(kca_s[g, 0:rows, :], qa_s[g, :, 0:LANES]) + bias_c for g in range(G)]
        psum = []
        for g in range(G):
            s = scores_c[g]
            e = jnp.exp2(s - jnp.max(s, axis=0, keepdims=True))
            inv = per_head(jnp.where(t_row >= CMP_L - 1, 1.0, 0.0)) / jnp.sum(e, axis=0, keepdims=True)
            p = e * inv
            ps = p[:, 0:Q]
            for h in range(1, HPG):
                ps = ps + p[:, h * Q:(h + 1) * Q]
            psum.append(jnp.concatenate([ps, jnp.zeros((ncmp - rows, Q), F32)], axis=0) if rows < ncmp else ps)
            o_cmp_t = _dot(vct_s[g, :, 0:rows], p.astype(BF16))
            for h in range(HPG):
                col = g * HPG + h
                cs = slice(h * Q, (h + 1) * Q)
                part_s[g, :, cs] = gates_t[col:col + 1] * o_cmp_t[0:DV, cs]
        return tuple(psum)

    half = ncmp // 2
    ended = (q0 + Q - CMP_L) // CMP_STRIDE + 1
    psum_t = lax.cond(ended <= half, lambda: compressed(half), lambda: compressed(ncmp))

    for g in range(G):
        imp_t = jnp.dot(ovt_ref[...], psum_t[g], preferred_element_type=F32, precision=HIGHEST)
        jcol = lax.broadcasted_iota(jnp.int32, (nslc, 1), 0)
        cur = t_row // SLC_L
        forced = (jcol == 0) | (jcol == cur) | (jcol == cur - 1)
        imp_t = jnp.where(forced, FORCE_SCORE, imp_t)
        imp_t = jnp.where(jcol <= cur, imp_t, -1.0)
        nrow = nslc // SUBLANES
        rows = [imp_t[r * SUBLANES:(r + 1) * SUBLANES] for r in range(nrow)]
        ranks = [jnp.zeros((SUBLANES, Q), F32) for _ in range(nrow)]
        jloc = lax.broadcasted_iota(jnp.int32, (SUBLANES, Q), 0)
        for j2 in range(nslc):
            other = jnp.broadcast_to(imp_t[j2:j2 + 1, :], (SUBLANES, Q))
            for r in range(nrow):
                if (r + 1) * SUBLANES - 1 <= j2:
                    ahead = jnp.where(other > rows[r], 1.0, 0.0)
                elif r * SUBLANES > j2:
                    ahead = jnp.where(other >= rows[r], 1.0, 0.0)
                else:
                    ahead = jnp.where(jloc + r * SUBLANES > j2, jnp.where(other >= rows[r], 1.0, 0.0),
                                      jnp.where(other > rows[r], 1.0, 0.0))
                ranks[r] = ranks[r] + ahead
        ntop = float(min(SLC_TOP, nslc))
        bias_t = jnp.concatenate([jnp.where(rk < ntop, 0.0, -NSA_BIG) for rk in ranks]
                                 + [jnp.zeros((LANES - nslc, Q), F32)], axis=0)
        sel_bias = bias_t.T.astype(BF16)
        for h in range(HPG):
            qa_s[g, h * Q:(h + 1) * Q, LANES:2 * LANES] = sel_bias

    state0 = tuple((jnp.full((1, R), -NSA_BIG, F32), jnp.zeros((NSA_VROWS, R), F32)) for _ in range(G))

    def sel_tile(k0, nk, state, causal):
        scores = [_dot_nt(ksa_s[g, pl.ds(k0, nk), :], qa_s[g]) for g in range(G)]
        if causal:
            pos = k0 + lax.broadcasted_iota(jnp.int32, (nk, 1), 0)
            bias = per_head(jnp.where(pos <= t_row, 0.0, -NSA_BIG))
            scores = [s + bias for s in scores]
        out = []
        for g in range(G):
            m_old, acc = state[g]
            m_new = jnp.maximum(m_old, jnp.max(scores[g], axis=0, keepdims=True))
            p = jnp.exp2(scores[g] - m_new).astype(BF16)
            acc = jnp.exp2(m_old - m_new) * acc + _dot(vst_s[g, :, pl.ds(k0, nk)], p)
            out.append((m_new, acc))
        return tuple(out)

    state = lax.fori_loop(0, q0 // TK, lambda kt, st: sel_tile(pl.multiple_of(kt * TK, TK), TK, st, False), state0)
    state = lax.cond(q0 % TK != 0, lambda st: sel_tile(pl.multiple_of(q0 - Q, Q), Q, st, False), lambda st: st,
                     state)
    state = sel_tile(pl.multiple_of(q0, Q), Q, state, True)

    w0 = pl.multiple_of(jnp.maximum(q0 - WIN, 0), Q)
    wpos = w0 + lax.broadcasted_iota(jnp.int32, (W, 1), 0)
    dist_w = t_row - wpos
    bias_w = per_head(jnp.where((dist_w >= 0) & (dist_w < WIN), 0.0, -NSA_BIG))
    scores_w = [_dot_nt(kwa_s[g, pl.ds(w0, W), :], qa_s[g, :, 0:LANES]) + bias_w for g in range(G)]
    for g in range(G):
        s = scores_w[g]
        e = jnp.exp2(s - jnp.max(s, axis=0, keepdims=True))
        pv_w = _dot(vwt_s[g, :, pl.ds(w0, W)], e.astype(BF16))
        o_win_t = pv_w[0:DV] / pv_w[DV:DV + 1]
        for h in range(HPG):
            col = g * HPG + h
            cs = slice(h * Q, (h + 1) * Q)
            part_s[g, :, cs] += gates_t[2 * NSA_HEADS + col:2 * NSA_HEADS + col + 1] * o_win_t[:, cs]

    for g in range(G):
        acc = state[g][1]
        o_sel_t = acc[0:DV] / acc[DV:DV + 1]
        outs = []
        for h in range(HPG):
            col = g * HPG + h
            cs = slice(h * Q, (h + 1) * Q)
            outs.append(part_s[g, :, cs] + gates_t[NSA_HEADS + col:NSA_HEADS + col + 1] * o_sel_t[:, cs])
        for h in range(0, HPG, 2):
            col = g * HPG + h
            o_ref[:, col * DV:(col + 2) * DV] = jnp.concatenate([outs[h], outs[h + 1]], axis=0).T.astype(o_ref.dtype)


def _bf16_split3(x):
    x = np.asarray(x, np.float32)
    rnd = lambda a: a.astype(BF16).astype(np.float32)
    hi = rnd(x)
    mid = rnd(x - hi)
    lo = rnd(x - hi - mid)
    return hi, mid, lo


def _pos_tail(pos):
    pos = np.asarray(pos)
    hi, lo = (pos // SLC_L) * SLC_L, pos % SLC_L
    tail = np.zeros((pos.shape[0], NSA_TAIL), np.float32)
    for c in range(3):
        tail[:, 2 * c] = hi
        tail[:, 2 * c + 1] = lo
    return tail


def nsa_attend(proj, k_cmp, v_cmp, B, S):
    Q, G, HPG = NSA_Q, NSA_GROUPS, NSA_HPG
    ncmp, nslc = S // CMP_STRIDE, S // SLC_L
    assert S >= WIN + Q and S % NSA_TK == 0 and nslc <= LANES and nslc % SUBLANES == 0 and NSA_TK == 2 * Q
    n = np.arange(ncmp)[:, None]
    j = np.arange(nslc)[None, :]
    overlap = ((n * CMP_STRIDE < (j + 1) * SLC_L) & (n * CMP_STRIDE + CMP_L > j * SLC_L)
               & (n < (S - CMP_L) // CMP_STRIDE + 1)).astype(np.float32)
    slopes = np.asarray([_alibi_slope(h) for h in range(NSA_HEADS)], np.float32) * np.float32(LOG2E)
    qtail = np.zeros((NSA_HEADS, NSA_TAIL), np.float32)
    for c, part in enumerate(_bf16_split3(slopes)):
        qtail[:, 2 * c] = part
        qtail[:, 2 * c + 1] = part
    ktail = _pos_tail(np.arange(S))
    ctail = _pos_tail(np.arange(ncmp) * CMP_STRIDE + CMP_L - 1)
    vtail = np.zeros((S, NSA_TAIL), np.float32)
    vtail[:, 0] = 1.0
    kind = (np.arange(S)[:, None] // SLC_L == np.arange(LANES)[None, :]).astype(np.float32)
    qw = NSA_HEADS * NSA_DK
    kv0 = qw // LANES
    seqblk = lambda c: pl.BlockSpec((None, S, LANES), lambda b, i: (b, 0, c))
    cmpblk = pl.BlockSpec((None, G, ncmp, NSA_DK), lambda b, i: (b, 0, 0, 0))
    const = lambda a: pl.BlockSpec(a.shape, lambda b, i: (0, 0))
    consts = [jnp.asarray(overlap.T, F32), jnp.asarray(qtail, F32), jnp.asarray(ktail, BF16),
              jnp.asarray(ctail, BF16), jnp.asarray(vtail, BF16), jnp.asarray(kind, BF16)]
    return pl.pallas_call(
        functools.partial(_nsa_kernel, seq=S),
        grid=(B, S // Q),
        in_specs=[
            pl.BlockSpec((None, Q, qw), lambda b, i: (b, i, 0)),
            pl.BlockSpec((None, Q, LANES), lambda b, i: (b, i, kv0 + 6)),
            cmpblk, cmpblk,
            seqblk(kv0 + 2), seqblk(kv0 + 3), seqblk(kv0 + 4), seqblk(kv0 + 5),
        ] + [const(a) for a in consts],
        out_specs=pl.BlockSpec((None, Q, NSA_HEADS * NSA_DV), lambda b, i: (b, i, 0)),
        out_shape=jax.ShapeDtypeStruct((B, S, NSA_HEADS * NSA_DV), BF16),
        scratch_shapes=[
            pltpu.VMEM((G, HPG * Q, 2 * LANES), BF16),
            pltpu.VMEM((G, ncmp, LANES), BF16),
            pltpu.VMEM((G, NSA_VROWS, ncmp), BF16),
            pltpu.VMEM((G, S, 2 * LANES), BF16),
            pltpu.VMEM((G, NSA_VROWS, S), BF16),
            pltpu.VMEM((G, S, LANES), BF16),
            pltpu.VMEM((G, NSA_VROWS, S), BF16),
            pltpu.VMEM((G, NSA_DV, HPG * Q), F32),
        ],
        compiler_params=_cparams("parallel", "arbitrary"),
        name="nsa_attend",
    )(proj, proj, k_cmp, v_cmp, proj, proj, proj, proj, *consts)


def _pad_cols(w, n):
    return jnp.pad(w, ((0, 0), (0, n - w.shape[1])))


def kernel(x, norm_mix_pre, norm_mix_post, norm_mlp_pre, norm_mlp_post, mlp_w_up, mlp_w_down, ret_w_in, ret_gn, ret_w_out, gla_w_in, gla_w_gate_up, gla_b_gate, gla_gn, gla_w_out, lru_w_in, lru_conv_w, lru_conv_b, lru_w_a, lru_b_a, lru_w_x, lru_b_x, lru_lambda, lru_w_out, nsa_w_in, nsa_pe_k, nsa_w1_k, nsa_w2_k, nsa_pe_v, nsa_w1_v, nsa_w2_v, nsa_w_out):
    B, S, D = x.shape
    T = B * S
    depth = norm_mix_pre.shape[0]
    x2 = x.reshape(T, D)
    h2 = None
    wu_all, wd_all = mlp_w_up.astype(BF16), mlp_w_down.astype(BF16)
    for i in range(depth):
        m, j = i % 4, i // 4

        def project(w):
            w = _pad_cols(w, -(-w.shape[1] // LANES) * LANES).astype(BF16)
            tn = _proj_tn(w.shape[1])
            if h2 is not None:
                return in_proj(h2, w, tn)
            return norm_proj(x2, norm_mix_pre[i], w, tn)

        if m == 0:
            proj = project(ret_w_in[j])
            mix = retention_mix(proj.reshape(B, S, -1), ret_gn[j], B, S)
            w_out = ret_w_out[j]
        elif m == 1:
            proj = project(gla_w_in[j])
            w_gate_pad = jnp.pad(gla_w_gate_up[j], ((0, LANES - GLA_RANK), (0, 0))).astype(BF16)
            mix = gla_mix(proj.reshape(B, S, -1), w_gate_pad, gla_b_gate[j], gla_gn[j], B, S)
            w_out = gla_w_out[j]
        elif m == 2:
            proj = project(lru_w_in[j])
            mix = lru_mix(proj.reshape(B, S, -1), lru_conv_w[j], lru_conv_b[j], lru_w_a[j], lru_b_a[j],
                          lru_w_x[j], lru_b_x[j], lru_lambda[j], B, S)
            w_out = lru_w_out[j]
        else:
            proj = project(nsa_w_in[j]).reshape(B, S, -1)
            k_cmp, v_cmp = nsa_compress(proj, nsa_pe_k[j], nsa_w1_k[j], nsa_w2_k[j],
                                        nsa_pe_v[j], nsa_w1_v[j], nsa_w2_v[j], B, S)
            mix = nsa_attend(proj, k_cmp, v_cmp, B, S)
            w_out = nsa_w_out[j]
        g_next = norm_mix_pre[i + 1] if i + 1 < depth else None
        res = sublayer_tail(mix.reshape(T, -1), w_out.astype(BF16), norm_mix_post[i], x2, norm_mlp_pre[i],
                            wu_all, wd_all, i, norm_mlp_post[i], g_next)
        x2, h2 = res if g_next is not None else (res, None)
    return x2.reshape(B, S, D)
```

```python
import functools
import math

import numpy as np
import jax
import jax.numpy as jnp
from jax import lax
from jax.experimental import pallas as pl
from jax.experimental.pallas import tpu as pltpu

F32 = jnp.float32
BF16 = jnp.bfloat16
HIGHEST = lax.Precision.HIGHEST

LANES = 128
SUBLANES = 8

D_MODEL = 1024
D_FF = 4 * D_MODEL
NORM_EPS = 1e-6

RET_HEADS, RET_DK, RET_DV = 4, 256, 512
RET_CHUNK = 256

GLA_HEADS, GLA_DK, GLA_DV = 4, 128, 256
GLA_RANK = 16
GLA_TAU = 16.0
GLA_CHUNK = 256

LRU_WIDTH = D_MODEL
LRU_BLOCKS = 8
LRU_BS = LRU_WIDTH // LRU_BLOCKS
CONV_W = 4
LRU_C = 8.0
LRU_TCHUNK = 128
LRU_HIST = 128
LRU_KEEP = 16
LRU_SLAB = LRU_TCHUNK + SUBLANES

NSA_HEADS, NSA_GROUPS, NSA_HPG = 16, 2, 8
NSA_DK = 64
NSA_DV = 64
CMP_L, CMP_STRIDE = 32, 16
CMP_HID = 2 * NSA_DK
SLC_L, SLC_TOP = 64, 16
WIN = 512
FORCE_SCORE = 1e6
NSA_Q = 256
NSA_TK = 512
NSA_TAIL = 64
NSA_VROWS = NSA_DV + 16
NSA_BIG = 1e30
LOG2E = math.log2(math.e)

VMEM_LIMIT = 56 * 1024 * 1024


def _cparams(*sem):
    return pltpu.CompilerParams(dimension_semantics=sem, vmem_limit_bytes=VMEM_LIMIT)


def _rms(x, g):
    return x * lax.rsqrt(jnp.mean(x * x, axis=-1, keepdims=True) + NORM_EPS) * g


def _dot(a, b):
    return jnp.dot(a, b, preferred_element_type=F32)


def _dot_nt(a, b):
    return lax.dot_general(a, b, (((1,), (1,)), ((), ())), preferred_element_type=F32)


def _dot_tn(a, b):
    return lax.dot_general(a, b, (((0,), (0,)), ((), ())), preferred_element_type=F32)


def _gelu_tanh(x):
    return 0.5 * x * (1.0 + jnp.tanh(math.sqrt(2.0 / math.pi) * (x + 0.044715 * (x * x * x))))


def _sigmoid(x):
    return 0.5 * (jnp.tanh(0.5 * x) + 1.0)


def _silu(x):
    t = 0.5 * x
    return t * jnp.tanh(t) + t


PROJ_TN_MAX = 3200


def _proj_tn(n):
    return max(t for t in range(LANES, min(n, PROJ_TN_MAX) + 1, LANES) if n % t == 0)


def _norm_proj_kernel(x_ref, g_ref, w_ref, o_ref, h_ref):
    @pl.when(pl.program_id(1) == 0)
    def _():
        h_ref[...] = _rms(x_ref[...], g_ref[...]).astype(BF16)

    o_ref[...] = _dot(h_ref[...], w_ref[...]).astype(o_ref.dtype)


def norm_proj(x2d, g, w_bf16, tn, tm=1024):
    T, D = x2d.shape
    N = w_bf16.shape[1]
    assert T % tm == 0 and N % tn == 0
    return pl.pallas_call(
        _norm_proj_kernel,
        grid=(T // tm, N // tn),
        in_specs=[
            pl.BlockSpec((tm, D), lambda i, j: (i, 0)),
            pl.BlockSpec((1, D), lambda i, j: (0, 0)),
            pl.BlockSpec((D, tn), lambda i, j: (0, j)),
        ],
        out_specs=pl.BlockSpec((tm, tn), lambda i, j: (i, j)),
        out_shape=jax.ShapeDtypeStruct((T, N), BF16),
        scratch_shapes=[pltpu.VMEM((tm, D), BF16)],
        compiler_params=_cparams("parallel", "arbitrary"),
        name="norm_proj",
    )(x2d, g.reshape(1, D), w_bf16)


def _proj_kernel(h_ref, w_ref, o_ref):
    o_ref[...] = _dot(h_ref[...], w_ref[...]).astype(o_ref.dtype)


def in_proj(h2d, w_bf16, tn, tm=1024):
    T, D = h2d.shape
    N = w_bf16.shape[1]
    assert T % tm == 0 and N % tn == 0
    return pl.pallas_call(
        _proj_kernel,
        grid=(T // tm, N // tn),
        in_specs=[pl.BlockSpec((tm, D), lambda i, j: (i, 0)), pl.BlockSpec((D, tn), lambda i, j: (0, j))],
        out_specs=pl.BlockSpec((tm, tn), lambda i, j: (i, j)),
        out_shape=jax.ShapeDtypeStruct((T, N), BF16),
        compiler_params=_cparams("parallel", "parallel"),
        name="in_proj",
    )(h2d, w_bf16)


MLP_FCHUNK = 512


def _tail_kernel(a_ref, wo_ref, gp_ref, x_ref, g1_ref, wu_ref, wd_ref, g2_ref, *rest):
    o_ref = rest[-2] if len(rest) == 3 else rest[0]
    x = x_ref[...] + _rms(_dot(a_ref[...], wo_ref[...]), gp_ref[...])
    h = _rms(x, g1_ref[...]).astype(BF16)
    acc = jnp.zeros(x.shape, F32)
    for c in range(D_FF // MLP_FCHUNK):
        sl = slice(c * MLP_FCHUNK, (c + 1) * MLP_FCHUNK)
        u = jnp.maximum(_dot(h, wu_ref[:, sl]), 0.0)
        acc = acc + _dot((u * u).astype(BF16), wd_ref[sl, :])
    out = x + _rms(acc, g2_ref[...])
    o_ref[...] = out
    if len(rest) == 3:
        gn_ref, _, hn_ref = rest
        hn_ref[...] = _rms(out, gn_ref[...]).astype(hn_ref.dtype)


def sublayer_tail(a2d, wo_bf16, g_post, x2d, g1, wu_all, wd_all, layer, g2, g_next=None, tm=512):
    T, D = x2d.shape
    K = a2d.shape[1]
    F = wu_all.shape[2]
    resident = lambda shape: pl.BlockSpec(shape, lambda i: (0, 0), pipeline_mode=pl.Buffered(1))
    of_layer = lambda shape: pl.BlockSpec((None,) + shape, lambda i: (layer, 0, 0), pipeline_mode=pl.Buffered(1))
    tile = pl.BlockSpec((tm, D), lambda i: (i, 0))
    row = lambda a: a.reshape(1, D)
    in_specs = [pl.BlockSpec((tm, K), lambda i: (i, 0)), resident((K, D)), resident((1, D)), tile,
                resident((1, D)), of_layer((D, F)), of_layer((F, D)), resident((1, D))]
    args = [a2d, wo_bf16, row(g_post), x2d, row(g1), wu_all, wd_all, row(g2)]
    out_specs, out_shape = tile, jax.ShapeDtypeStruct((T, D), F32)
    if g_next is not None:
        in_specs.append(resident((1, D)))
        args.append(row(g_next))
        out_specs, out_shape = [tile, tile], [out_shape, jax.ShapeDtypeStruct((T, D), BF16)]
    return pl.pallas_call(
        _tail_kernel,
        grid=(T // tm,),
        in_specs=in_specs,
        out_specs=out_specs,
        out_shape=out_shape,
        compiler_params=_cparams("parallel"),
        name="sublayer_tail",
    )(*args)


def _ret_kernel(q_ref, k_ref, v_ref, g_ref, gn_ref, di_ref, dq_ref, dk_ref, dc_ref, o_ref, st_ref):
    H, DK, DV = RET_HEADS, RET_DK, RET_DV

    @pl.when(pl.program_id(1) == 0)
    def _():
        st_ref[...] = jnp.zeros_like(st_ref)

    qb = [q_ref[:, h * DK:(h + 1) * DK] for h in range(H)]
    kb = [k_ref[:, h * DK:(h + 1) * DK] * (DK ** -0.5) for h in range(H)]
    vb = [v_ref[:, h * DV:(h + 1) * DV] for h in range(H)]
    st = [st_ref[h] for h in range(H)]
    scores = [_dot_nt(qb[h], kb[h]) for h in range(H)]
    inter = [_dot((qb[h].astype(F32) * dq_ref[h]).astype(BF16), st[h].astype(BF16)) for h in range(H)]
    for h in range(H):
        st_ref[h] = st[h] * dc_ref[h] + _dot_tn((kb[h].astype(F32) * dk_ref[h]).astype(BF16), vb[h])
    for h in range(H):
        o = _dot((scores[h] * di_ref[h]).astype(BF16), vb[h]) + inter[h]
        o = o - jnp.mean(o, axis=-1, keepdims=True)
        hs = slice(h * DV, (h + 1) * DV)
        gate = g_ref[:, hs].astype(F32)
        o_ref[:, hs] = (_rms(o, gn_ref[:, hs]) * _silu(gate)).astype(o_ref.dtype)


def retention_mix(proj, gn, B, S):
    H, DK, DV, C = RET_HEADS, RET_DK, RET_DV, RET_CHUNK
    assert (DK ** -0.5) == 2.0 ** round(math.log2(DK ** -0.5))
    log_gamma = jnp.log1p(-jnp.exp2(-5.0 - jnp.arange(H, dtype=F32)))
    pos = jnp.arange(C, dtype=F32)
    diff = pos[:, None] - pos[None, :]
    d_intra = jnp.where(diff >= 0, jnp.exp(log_gamma[:, None, None] * jnp.maximum(diff, 0.0)), 0.0)
    d_q = jnp.exp(log_gamma[:, None] * (pos + 1.0))[:, :, None]
    d_k = jnp.exp(log_gamma[:, None] * (C - 1.0 - pos))[:, :, None]
    d_c = jnp.exp(log_gamma * C)[:, None, None]
    qk_w, v_w = H * DK, H * DV
    whole = lambda a: pl.BlockSpec(a.shape, lambda b, c: (0,) * a.ndim)
    return pl.pallas_call(
        _ret_kernel,
        grid=(B, S // C),
        in_specs=[
            pl.BlockSpec((None, C, qk_w), lambda b, c: (b, c, 0)),
            pl.BlockSpec((None, C, qk_w), lambda b, c: (b, c, 1)),
            pl.BlockSpec((None, C, v_w), lambda b, c: (b, c, (2 * qk_w) // v_w)),
            pl.BlockSpec((None, C, v_w), lambda b, c: (b, c, (2 * qk_w) // v_w + 1)),
            pl.BlockSpec((1, v_w), lambda b, c: (0, 0)),
            whole(d_intra), whole(d_q), whole(d_k), whole(d_c),
        ],
        out_specs=pl.BlockSpec((None, C, v_w), lambda b, c: (b, c, 0)),
        out_shape=jax.ShapeDtypeStruct((B, S, v_w), BF16),
        scratch_shapes=[pltpu.VMEM((H, DK, DV), F32)],
        compiler_params=_cparams("parallel", "arbitrary"),
        name="retention",
    )(proj, proj, proj, proj, gn.reshape(1, v_w), d_intra, d_q, d_k, d_c)


def _gla_level_sizes(C):
    sizes, s = [], 2
    while s <= C:
        sizes.append(s)
        s *= 2
    return sizes


def _gla_constants(C):
    t = np.arange(C)[:, None]
    r = np.arange(C)[None, :]
    mats = [(r <= t)]
    mats.append(r > t)
    masks = [np.eye(C, dtype=bool)]
    for bs in _gla_level_sizes(C):
        half = bs // 2
        mid = (t // bs) * bs + half - 1
        second = (t % bs) >= half
        m = np.where(second, (r > mid) & (r <= t), (r > t) & (r <= mid))
        mats.append(m)
        same = (t // bs) == (r // bs)
        masks.append(same & second & ((r % bs) < half))
    return (np.concatenate(mats, 0).astype(np.float32), np.stack(masks, 0).astype(np.float32))


def _gla_kernel(q_ref, k_ref, v_ref, r_ref, gl_ref, wg_ref, bg_ref, gn_ref, mall_ref, mask_ref, o_ref, st_ref):
    C, DK = GLA_CHUNK, GLA_DK

    H, DV = GLA_HEADS, GLA_DV
    W = H * DK

    @pl.when(pl.program_id(1) == 0)
    def _():
        st_ref[...] = jnp.zeros_like(st_ref)

    z = _dot(gl_ref[...], wg_ref[...]) + bg_ref[...]
    la = (jnp.minimum(z, 0.0) - jnp.log1p(jnp.exp(-jnp.abs(z)))) * (1.0 / GLA_TAU)
    la_hi = la.astype(BF16)
    la_lo = (la - la_hi.astype(F32)).astype(BF16)
    cum = _dot(mall_ref[0:2 * C, :], jnp.concatenate([la_hi, la_lo], axis=1))
    decay_cum = jnp.exp(cum[:, :W] + cum[:, W:])
    decay = jnp.exp(_dot(mall_ref[2 * C:, :], la_hi))
    e_b = decay_cum[0:C]
    e_rest = decay_cum[C:2 * C]
    e_last = decay_cum[C - 1:C]

    q = q_ref[...].astype(F32) * (DK ** -0.5)
    k = k_ref[...].astype(F32)
    nlev = len(_gla_level_sizes(C))
    qe = [q.astype(BF16)] + [(q * decay[l * C:(l + 1) * C]).astype(BF16) for l in range(nlev)]
    ke = [k_ref[...]] + [(k * decay[l * C:(l + 1) * C]).astype(BF16) for l in range(nlev)]
    q_in = (q * e_b).astype(BF16)
    k_out = (k * e_rest).astype(BF16)
    for h in range(H):
        ks = slice(h * DK, (h + 1) * DK)
        vs = slice(h * DV, (h + 1) * DV)
        vb = v_ref[:, vs]
        a = _dot_nt(qe[0][:, ks], ke[0][:, ks]) * mask_ref[0]
        for l in range(nlev):
            a = a + _dot_nt(qe[1 + l][:, ks], ke[1 + l][:, ks]) * mask_ref[1 + l]
        st = st_ref[h]
        o = _dot(a.astype(BF16), vb) + _dot_nt(q_in[:, ks], st.astype(BF16))
        st_ref[h] = st * e_last[:, ks] + _dot_tn(vb, k_out[:, ks])
        gate = r_ref[:, vs].astype(F32)
        o_ref[:, vs] = (_rms(o, gn_ref[:, vs]) * _silu(gate)).astype(o_ref.dtype)


def gla_mix(proj, w_gate_pad, b_gate, gn, B, S):
    H, DK, DV, C = GLA_HEADS, GLA_DK, GLA_DV, GLA_CHUNK
    mall, masks = _gla_constants(C)
    qk_w, v_w = H * DK, H * DV
    gl_blk = (2 * qk_w + 2 * v_w) // LANES
    whole = lambda shape: pl.BlockSpec(shape, lambda b, c: (0,) * len(shape))
    return pl.pallas_call(
        _gla_kernel,
        grid=(B, S // C),
        in_specs=[
            pl.BlockSpec((None, C, qk_w), lambda b, c: (b, c, 0)),
            pl.BlockSpec((None, C, qk_w), lambda b, c: (b, c, 1)),
            pl.BlockSpec((None, C, v_w), lambda b, c: (b, c, (2 * qk_w) // v_w)),
            pl.BlockSpec((None, C, v_w), lambda b, c: (b, c, (2 * qk_w) // v_w + 1)),
            pl.BlockSpec((None, C, LANES), lambda b, c: (b, c, gl_blk)),
            whole((LANES, qk_w)), whole((1, qk_w)), whole((1, v_w)), whole(mall.shape), whole(masks.shape),
        ],
        out_specs=pl.BlockSpec((None, C, v_w), lambda b, c: (b, c, 0)),
        out_shape=jax.ShapeDtypeStruct((B, S, v_w), BF16),
        scratch_shapes=[pltpu.VMEM((H, DV, DK), F32)],
        compiler_params=_cparams("parallel", "arbitrary"),
        name="gla",
    )(proj, proj, proj, proj, proj, w_gate_pad, b_gate.reshape(1, H * DK), gn.reshape(1, H * DV),
      jnp.asarray(mall, BF16), jnp.asarray(masks, F32))


def _lru_kernel(x_ref, y_ref, sh_ref, cw_ref, cb_ref, wa_ref, ba_ref, wx_ref, bx_ref, lam_ref, o_ref,
                xbuf, xc_s, a_s, u_s, hcar):
    B, TC, W = x_ref.shape
    HIST, SLAB = LRU_HIST, LRU_SLAB

    @pl.when(pl.program_id(0) == 0)
    def _():
        xbuf[...] = jnp.zeros_like(xbuf)
        hcar[...] = jnp.zeros_like(hcar)

    for b in range(B):
        xbuf[b, HIST:HIST + TC, :] = x_ref[b]
        xe = xbuf[b]
        xc = cb_ref[...] + x_ref[b].astype(F32) * cw_ref[CONV_W - 1:CONV_W, :]
        for tap in range(CONV_W - 1):
            xc = xc + _dot(sh_ref[tap], xe) * cw_ref[tap:tap + 1, :]
        xc_s[b * TC:(b + 1) * TC, :] = xc
        xbuf[b, HIST - LRU_KEEP:HIST, :] = xbuf[b, HIST + TC - LRU_KEEP:HIST + TC, :]

    xc = xc_s[...]
    xcb = xc.astype(BF16)
    for n in range(LRU_BLOCKS):
        sl = slice(n * LRU_BS, (n + 1) * LRU_BS)
        xc_s[:, sl] = _dot(xcb[:, sl], wa_ref[n])
    th_r = jnp.tanh(xc_s[...] + ba_ref[...])
    for n in range(LRU_BLOCKS):
        sl = slice(n * LRU_BS, (n + 1) * LRU_BS)
        xc_s[:, sl] = _dot(xcb[:, sl], wx_ref[n])
    th_i = jnp.tanh(xc_s[...] + bx_ref[...])
    nl = -lam_ref[...]
    softplus = jnp.maximum(nl, 0.0) + jnp.log1p(jnp.exp(-jnp.abs(nl)))
    kdec = (-0.5 * LRU_C * LOG2E) * softplus
    a = jnp.exp2(kdec * th_r + kdec)
    u = jnp.sqrt(1.0 - a * a) * (0.5 * th_i + 0.5) * xc
    NL = W // LANES
    for j in range(NL):
        for b in range(B):
            a_s[j, b * SLAB:b * SLAB + TC, :] = a[b * TC:(b + 1) * TC, j * LANES:(j + 1) * LANES]
            u_s[j, b * SLAB:b * SLAB + TC, :] = u[b * TC:(b + 1) * TC, j * LANES:(j + 1) * LANES]

    def body(t, hs):
        rows = pl.ds(t, B, stride=SLAB)
        out = []
        for j in range(NL):
            h = a_s[j, rows, :] * hs[j] + u_s[j, rows, :]
            u_s[j, rows, :] = h
            out.append(h)
        return tuple(out)

    h0 = tuple(hcar[:, j * LANES:(j + 1) * LANES] for j in range(NL))
    hT = lax.fori_loop(0, TC, body, h0, unroll=8)
    hcar[...] = jnp.concatenate(hT, axis=1)
    for b in range(B):
        hs = jnp.concatenate([u_s[j, b * SLAB:b * SLAB + TC, :] for j in range(NL)], axis=1)
        o_ref[b] = (hs * _gelu_tanh(y_ref[b].astype(F32))).astype(o_ref.dtype)


def lru_mix(proj, conv_w, conv_b, w_a, b_a, w_x, b_x, lam, B, S):
    TC, W, HIST, SLAB = LRU_TCHUNK, LRU_WIDTH, LRU_HIST, LRU_SLAB
    assert LRU_KEEP >= CONV_W - 1 and S % TC == 0
    t_idx = np.arange(TC)[:, None]
    c_idx = np.arange(HIST + TC)[None, :]
    shift = np.stack([(c_idx == HIST + t_idx - (CONV_W - 1 - tap)) for tap in range(CONV_W - 1)]).astype(np.float32)
    row = lambda a: a.reshape(1, W)
    vec = pl.BlockSpec((1, W), lambda t: (0, 0))
    blk = pl.BlockSpec((LRU_BLOCKS, LRU_BS, LRU_BS), lambda t: (0, 0, 0))
    return pl.pallas_call(
        _lru_kernel,
        grid=(S // TC,),
        in_specs=[
            pl.BlockSpec((B, TC, W), lambda t: (0, t, 0)),
            pl.BlockSpec((B, TC, W), lambda t: (0, t, 1)),
            pl.BlockSpec(shift.shape, lambda t: (0, 0, 0)),
            pl.BlockSpec((CONV_W, W), lambda t: (0, 0)),
            vec, blk, vec, blk, vec, vec,
        ],
        out_specs=pl.BlockSpec((B, TC, W), lambda t: (0, t, 0)),
        out_shape=jax.ShapeDtypeStruct((B, S, W), BF16),
        scratch_shapes=[
            pltpu.VMEM((B, HIST + TC, W), BF16),
            pltpu.VMEM((B * TC, W), F32),
            pltpu.VMEM((W // LANES, B * SLAB, LANES), F32),
            pltpu.VMEM((W // LANES, B * SLAB, LANES), F32),
            pltpu.VMEM((B, W), F32),
        ],
        compiler_params=_cparams("arbitrary"),
        name="rglru",
    )(proj, proj, jnp.asarray(shift, BF16), conv_w, row(conv_b), (0.5 * w_a).astype(BF16), row(0.5 * b_a),
      (0.5 * w_x).astype(BF16), row(0.5 * b_x), row(lam))


def _nsa_cmp_kernel(kc_ref, vc_ref, pek_ref, w1k_ref, w2k_ref, pev_ref, w1v_ref, w2v_ref, ko_ref, vo_ref,
                    buf, *, nblk):
    def compress(t_ref, pe_ref, w1_ref, w2_ref, o_ref, d):
        buf[...] = t_ref[...].astype(F32)
        ya = [jnp.zeros((nblk, CMP_HID), F32) for _ in range(NSA_GROUPS)]
        yb = [jnp.zeros((nblk, CMP_HID), F32) for _ in range(NSA_GROUPS)]
        for l in range(CMP_STRIDE):
            both = buf[pl.ds(l, nblk, stride=CMP_STRIDE), :]
            for g in range(NSA_GROUPS):
                rows = both[:, g * d:(g + 1) * d]
                lo = (rows + pe_ref[l:l + 1, :]).astype(BF16)
                hi = (rows + pe_ref[CMP_STRIDE + l:CMP_STRIDE + l + 1, :]).astype(BF16)
                ya[g] = ya[g] + _dot(lo, w1_ref[l * d:(l + 1) * d, :])
                yb[g] = yb[g] + _dot(hi, w1_ref[(CMP_STRIDE + l) * d:(CMP_STRIDE + l + 1) * d, :])
        for g in range(NSA_GROUPS):
            shifted = jnp.concatenate([yb[g][1:], jnp.zeros((1, CMP_HID), F32)], axis=0)
            hid = _gelu_tanh(ya[g] + shifted).astype(BF16)
            o_ref[g] = _dot(hid, w2_ref[...])

    compress(kc_ref, pek_ref, w1k_ref, w2k_ref, ko_ref, NSA_DK)
    compress(vc_ref, pev_ref, w1v_ref, w2v_ref, vo_ref, NSA_DV)


def nsa_compress(proj, pe_k, w1_k, w2_k, pe_v, w1_v, w2_v, B, S):
    nblk = S // CMP_STRIDE
    G = NSA_GROUPS
    kc_blk = (NSA_HEADS * NSA_DK) // LANES
    full2 = lambda a: pl.BlockSpec(a.shape, lambda b: (0, 0))
    w1k, w2k, w1v, w2v = (a.astype(BF16) for a in (w1_k, w2_k, w1_v, w2_v))
    out_sd = jax.ShapeDtypeStruct((B, G, nblk, NSA_DK), F32)
    return pl.pallas_call(
        functools.partial(_nsa_cmp_kernel, nblk=nblk),
        grid=(B,),
        in_specs=[
            pl.BlockSpec((None, S, LANES), lambda b: (b, 0, kc_blk)),
            pl.BlockSpec((None, S, LANES), lambda b: (b, 0, kc_blk + 1)),
            full2(pe_k), full2(w1k), full2(w2k), full2(pe_v), full2(w1v), full2(w2v),
        ],
        out_specs=[pl.BlockSpec((None, G, nblk, NSA_DK), lambda b: (b, 0, 0, 0))] * 2,
        out_shape=[out_sd, out_sd],
        scratch_shapes=[pltpu.VMEM((S, LANES), F32)],
        compiler_params=_cparams("parallel"),
        name="nsa_compress",
    )(proj, proj, pe_k, w1k, w2k, pe_v, w1v, w2v)


def _alibi_slope(head):
    return float(2.0 ** (-8.0 * (head + 1.0) / NSA_HEADS))


def _nsa_kernel(q_ref, gate_ref, kc_ref, vc_ref, ks_ref, vs_ref, kw_ref, vw_ref,
                ovt_ref, qt_ref, kt_ref, ct_ref, vt_ref, kind_ref, o_ref,
                qa_s, kca_s, vct_s, ksa_s, vst_s, kwa_s, vwt_s, part_s, *, seq):
    Q, TK, HPG, DK, DV, G = NSA_Q, NSA_TK, NSA_HPG, NSA_DK, NSA_DV, NSA_GROUPS
    R = HPG * Q
    W = WIN + Q
    ncmp = seq // CMP_STRIDE
    nslc = seq // SLC_L
    scale = DK ** -0.5
    qi = pl.program_id(1)
    q0 = qi * Q
    t_row = q0 + lax.broadcasted_iota(jnp.int32, (1, Q), 1)
    gates_t = _sigmoid(gate_ref[...].astype(F32)).T
    per_head = lambda a: jnp.concatenate([a] * HPG, axis=1)

    @pl.when(qi == 0)
    def _():
        for g in range(G):
            ks = slice(g * DK, (g + 1) * DK)
            vs = slice(g * DV, (g + 1) * DV)
            ksa_s[g, :, 0:LANES] = jnp.concatenate([ks_ref[:, ks], kt_ref[...]], axis=1)
            ksa_s[g, :, LANES:2 * LANES] = kind_ref[...]
            kwa_s[g] = jnp.concatenate([kw_ref[:, ks], kt_ref[...]], axis=1)
            tr = lambda a: a.astype(F32).T[0:NSA_VROWS].astype(BF16)
            vst_s[g] = tr(jnp.concatenate([vs_ref[:, vs], vt_ref[...]], axis=1))
            vwt_s[g] = tr(jnp.concatenate([vw_ref[:, vs], vt_ref[...]], axis=1))
            kca_s[g] = jnp.concatenate([kc_ref[g].astype(BF16), ct_ref[...]], axis=1)
            vct_s[g] = tr(jnp.concatenate([vc_ref[g], jnp.zeros((ncmp, NSA_TAIL), F32)], axis=1))

    for g in range(G):
        for h in range(HPG):
            col = g * HPG + h
            qh = q_ref[:, col * DK:(col + 1) * DK].astype(F32) * (scale * LOG2E)
            tail = jnp.broadcast_to(qt_ref[col:col + 1, :], (Q, NSA_TAIL))
            qa_s[g, h * Q:(h + 1) * Q, 0:LANES] = jnp.concatenate([qh, tail], axis=1).astype(BF16)

    def compressed(rows):
        cend = CMP_STRIDE * lax.broadcasted_iota(jnp.int32, (rows, 1), 0) + (CMP_L - 1)
        bias_c = per_head(jnp.where(cend <= t_row, 0.0, -NSA_BIG))
        scores_c = [_dot_nt(kca_s[g, 0:rows, :], qa_s[g, :, 0:LANES]) + bias_c for g in range(G)]
        psum = []
        for g in range(G):
            s = scores_c[g]
            e = jnp.exp2(s - jnp.max(s, axis=0, keepdims=True))
            inv = per_head(jnp.where(t_row >= CMP_L - 1, 1.0, 0.0)) / jnp.sum(e, axis=0, keepdims=True)
            p = e * inv
            ps = p[:, 0:Q]
            for h in range(1, HPG):
                ps = ps + p[:, h * Q:(h + 1) * Q]
            psum.append(jnp.concatenate([ps, jnp.zeros((ncmp - rows, Q), F32)], axis=0) if rows < ncmp else ps)
            o_cmp_t = _dot(vct_s[g, :, 0:rows], p.astype(BF16))
            for h in range(HPG):
                col = g * HPG + h
                cs = slice(h * Q, (h + 1) * Q)
                part_s[g, :, cs] = gates_t[col:col + 1] * o_cmp_t[0:DV, cs]
        return tuple(psum)

    half = ncmp // 2
    ended = (q0 + Q - CMP_L) // CMP_STRIDE + 1
    psum_t = lax.cond(ended <= half, lambda: compressed(half), lambda: compressed(ncmp))

    for g in range(G):
        imp_t = jnp.dot(ovt_ref[...], psum_t[g], preferred_element_type=F32, precision=HIGHEST)
        jcol = lax.broadcasted_iota(jnp.int32, (nslc, 1), 0)
        cur = t_row // SLC_L
        forced = (jcol == 0) | (jcol == cur) | (jcol == cur - 1)
        imp_t = jnp.where(forced, FORCE_SCORE, imp_t)
        imp_t = jnp.where(jcol <= cur, imp_t, -1.0)
        nrow = nslc // SUBLANES
        rows = [imp_t[r * SUBLANES:(r + 1) * SUBLANES] for r in range(nrow)]
        ranks = [jnp.zeros((SUBLANES, Q), F32) for _ in range(nrow)]
        jloc = lax.broadcasted_iota(jnp.int32, (SUBLANES, Q), 0)
        for j2 in range(nslc):
            other = jnp.broadcast_to(imp_t[j2:j2 + 1, :], (SUBLANES, Q))
            for r in range(nrow):
                if (r + 1) * SUBLANES - 1 <= j2:
                    ahead = jnp.where(other > rows[r], 1.0, 0.0)
                elif r * SUBLANES > j2:
                    ahead = jnp.where(other >= rows[r], 1.0, 0.0)
                else:
                    ahead = jnp.where(jloc + r * SUBLANES > j2, jnp.where(other >= rows[r], 1.0, 0.0),
                                      jnp.where(other > rows[r], 1.0, 0.0))
                ranks[r] = ranks[r] + ahead
        ntop = float(min(SLC_TOP, nslc))
        bias_t = jnp.concatenate([jnp.where(rk < ntop, 0.0, -NSA_BIG) for rk in ranks]
                                 + [jnp.zeros((LANES - nslc, Q), F32)], axis=0)
        sel_bias = bias_t.T.astype(BF16)
        for h in range(HPG):
            qa_s[g, h * Q:(h + 1) * Q, LANES:2 * LANES] = sel_bias

    state0 = tuple((jnp.full((1, R), -NSA_BIG, F32), jnp.zeros((NSA_VROWS, R), F32)) for _ in range(G))

    def sel_tile(k0, nk, state, causal):
        scores = [_dot_nt(ksa_s[g, pl.ds(k0, nk), :], qa_s[g]) for g in range(G)]
        if causal:
            pos = k0 + lax.broadcasted_iota(jnp.int32, (nk, 1), 0)
            bias = per_head(jnp.where(pos <= t_row, 0.0, -NSA_BIG))
            scores = [s + bias for s in scores]
        out = []
        for g in range(G):
            m_old, acc = state[g]
            m_new = jnp.maximum(m_old, jnp.max(scores[g], axis=0, keepdims=True))
            p = jnp.exp2(scores[g] - m_new).astype(BF16)
            acc = jnp.exp2(m_old - m_new) * acc + _dot(vst_s[g, :, pl.ds(k0, nk)], p)
            out.append((m_new, acc))
        return tuple(out)

    state = lax.fori_loop(0, q0 // TK, lambda kt, st: sel_tile(pl.multiple_of(kt * TK, TK), TK, st, False), state0)
    state = lax.cond(q0 % TK != 0, lambda st: sel_tile(pl.multiple_of(q0 - Q, Q), Q, st, False), lambda st: st,
                     state)
    state = sel_tile(pl.multiple_of(q0, Q), Q, state, True)

    w0 = pl.multiple_of(jnp.maximum(q0 - WIN, 0), Q)
    wpos = w0 + lax.broadcasted_iota(jnp.int32, (W, 1), 0)
    dist_w = t_row - wpos
    bias_w = per_head(jnp.where((dist_w >= 0) & (dist_w < WIN), 0.0, -NSA_BIG))
    scores_w = [_dot_nt(kwa_s[g, pl.ds(w0, W), :], qa_s[g, :, 0:LANES]) + bias_w for g in range(G)]
    for g in range(G):
        s = scores_w[g]
        e = jnp.exp2(s - jnp.max(s, axis=0, keepdims=True))
        pv_w = _dot(vwt_s[g, :, pl.ds(w0, W)], e.astype(BF16))
        o_win_t = pv_w[0:DV] / pv_w[DV:DV + 1]
        for h in range(HPG):
            col = g * HPG + h
            cs = slice(h * Q, (h + 1) * Q)
            part_s[g, :, cs] += gates_t[2 * NSA_HEADS + col:2 * NSA_HEADS + col + 1] * o_win_t[:, cs]

    for g in range(G):
        acc = state[g][1]
        o_sel_t = acc[0:DV] / acc[DV:DV + 1]
        outs = []
        for h in range(HPG):
            col = g * HPG + h
            cs = slice(h * Q, (h + 1) * Q)
            outs.append(part_s[g, :, cs] + gates_t[NSA_HEADS + col:NSA_HEADS + col + 1] * o_sel_t[:, cs])
        for h in range(0, HPG, 2):
            col = g * HPG + h
            o_ref[:, col * DV:(col + 2) * DV] = jnp.concatenate([outs[h], outs[h + 1]], axis=0).T.astype(o_ref.dtype)


def _bf16_split3(x):
    x = np.asarray(x, np.float32)
    rnd = lambda a: a.astype(BF16).astype(np.float32)
    hi = rnd(x)
    mid = rnd(x - hi)
    lo = rnd(x - hi - mid)
    return hi, mid, lo


def _pos_tail(pos):
    pos = np.asarray(pos)
    hi, lo = (pos // SLC_L) * SLC_L, pos % SLC_L
    tail = np.zeros((pos.shape[0], NSA_TAIL), np.float32)
    for c in range(3):
        tail[:, 2 * c] = hi
        tail[:, 2 * c + 1] = lo
    return tail


def nsa_attend(proj, k_cmp, v_cmp, B, S):
    Q, G, HPG = NSA_Q, NSA_GROUPS, NSA_HPG
    ncmp, nslc = S // CMP_STRIDE, S // SLC_L
    assert S >= WIN + Q and S % NSA_TK == 0 and nslc <= LANES and nslc % SUBLANES == 0 and NSA_TK == 2 * Q
    n = np.arange(ncmp)[:, None]
    j = np.arange(nslc)[None, :]
    overlap = ((n * CMP_STRIDE < (j + 1) * SLC_L) & (n * CMP_STRIDE + CMP_L > j * SLC_L)
               & (n < (S - CMP_L) // CMP_STRIDE + 1)).astype(np.float32)
    slopes = np.asarray([_alibi_slope(h) for h in range(NSA_HEADS)], np.float32) * np.float32(LOG2E)
    qtail = np.zeros((NSA_HEADS, NSA_TAIL), np.float32)
    for c, part in enumerate(_bf16_split3(slopes)):
        qtail[:, 2 * c] = part
        qtail[:, 2 * c + 1] = part
    ktail = _pos_tail(np.arange(S))
    ctail = _pos_tail(np.arange(ncmp) * CMP_STRIDE + CMP_L - 1)
    vtail = np.zeros((S, NSA_TAIL), np.float32)
    vtail[:, 0] = 1.0
    kind = (np.arange(S)[:, None] // SLC_L == np.arange(LANES)[None, :]).astype(np.float32)
    qw = NSA_HEADS * NSA_DK
    kv0 = qw // LANES
    seqblk = lambda c: pl.BlockSpec((None, S, LANES), lambda b, i: (b, 0, c))
    cmpblk = pl.BlockSpec((None, G, ncmp, NSA_DK), lambda b, i: (b, 0, 0, 0))
    const = lambda a: pl.BlockSpec(a.shape, lambda b, i: (0, 0))
    consts = [jnp.asarray(overlap.T, F32), jnp.asarray(qtail, F32), jnp.asarray(ktail, BF16),
              jnp.asarray(ctail, BF16), jnp.asarray(vtail, BF16), jnp.asarray(kind, BF16)]
    return pl.pallas_call(
        functools.partial(_nsa_kernel, seq=S),
        grid=(B, S // Q),
        in_specs=[
            pl.BlockSpec((None, Q, qw), lambda b, i: (b, i, 0)),
            pl.BlockSpec((None, Q, LANES), lambda b, i: (b, i, kv0 + 6)),
            cmpblk, cmpblk,
            seqblk(kv0 + 2), seqblk(kv0 + 3), seqblk(kv0 + 4), seqblk(kv0 + 5),
        ] + [const(a) for a in consts],
        out_specs=pl.BlockSpec((None, Q, NSA_HEADS * NSA_DV), lambda b, i: (b, i, 0)),
        out_shape=jax.ShapeDtypeStruct((B, S, NSA_HEADS * NSA_DV), BF16),
        scratch_shapes=[
            pltpu.VMEM((G, HPG * Q, 2 * LANES), BF16),
            pltpu.VMEM((G, ncmp, LANES), BF16),
            pltpu.VMEM((G, NSA_VROWS, ncmp), BF16),
            pltpu.VMEM((G, S, 2 * LANES), BF16),
            pltpu.VMEM((G, NSA_VROWS, S), BF16),
            pltpu.VMEM((G, S, LANES), BF16),
            pltpu.VMEM((G, NSA_VROWS, S), BF16),
            pltpu.VMEM((G, NSA_DV, HPG * Q), F32),
        ],
        compiler_params=_cparams("parallel", "arbitrary"),
        name="nsa_attend",
    )(proj, proj, k_cmp, v_cmp, proj, proj, proj, proj, *consts)


def _pad_cols(w, n):
    return jnp.pad(w, ((0, 0), (0, n - w.shape[1])))


def kernel(x, norm_mix_pre, norm_mix_post, norm_mlp_pre, norm_mlp_post, mlp_w_up, mlp_w_down, ret_w_in, ret_gn, ret_w_out, gla_w_in, gla_w_gate_up, gla_b_gate, gla_gn, gla_w_out, lru_w_in, lru_conv_w, lru_conv_b, lru_w_a, lru_b_a, lru_w_x, lru_b_x, lru_lambda, lru_w_out, nsa_w_in, nsa_pe_k, nsa_w1_k, nsa_w2_k, nsa_pe_v, nsa_w1_v, nsa_w2_v, nsa_w_out):
    B, S, D = x.shape
    T = B * S
    depth = norm_mix_pre.shape[0]
    x2 = x.reshape(T, D)
    h2 = None
    wu_all, wd_all = mlp_w_up.astype(BF16), mlp_w_down.astype(BF16)
    for i in range(depth):
        m, j = i % 4, i // 4

        def project(w):
            w = _pad_cols(w, -(-w.shape[1] // LANES) * LANES).astype(BF16)
            tn = _proj_tn(w.shape[1])
            if h2 is not None:
                return in_proj(h2, w, tn)
            return norm_proj(x2, norm_mix_pre[i], w, tn)

        if m == 0:
            proj = project(ret_w_in[j])
            mix = retention_mix(proj.reshape(B, S, -1), ret_gn[j], B, S)
            w_out = ret_w_out[j]
        elif m == 1:
            proj = project(gla_w_in[j])
            w_gate_pad = jnp.pad(gla_w_gate_up[j], ((0, LANES - GLA_RANK), (0, 0))).astype(BF16)
            mix = gla_mix(proj.reshape(B, S, -1), w_gate_pad, gla_b_gate[j], gla_gn[j], B, S)
            w_out = gla_w_out[j]
        elif m == 2:
            proj = project(lru_w_in[j])
            mix = lru_mix(proj.reshape(B, S, -1), lru_conv_w[j], lru_conv_b[j], lru_w_a[j], lru_b_a[j],
                          lru_w_x[j], lru_b_x[j], lru_lambda[j], B, S)
            w_out = lru_w_out[j]
        else:
            proj = project(nsa_w_in[j]).reshape(B, S, -1)
            k_cmp, v_cmp = nsa_compress(proj, nsa_pe_k[j], nsa_w1_k[j], nsa_w2_k[j],
                                        nsa_pe_v[j], nsa_w1_v[j], nsa_w2_v[j], B, S)
            mix = nsa_attend(proj, k_cmp, v_cmp, B, S)
            w_out = nsa_w_out[j]
        g_next = norm_mix_pre[i + 1] if i + 1 < depth else None
        res = sublayer_tail(mix.reshape(T, -1), w_out.astype(BF16), norm_mix_post[i], x2, norm_mlp_pre[i],
                            wu_all, wd_all, i, norm_mlp_post[i], g_next)
        x2, h2 = res if g_next is not None else (res, None)
    return x2.reshape(B, S, D)
```
